```python
import math
import jax
import jax.numpy as jnp
from jax import lax
import numpy as np

D_MODEL = 1024
BATCH = 8
SEQ = 4096
DEPTH = 4
DEC_BATCH = 32
DEC_SEQ = 2048
PAST_LEN = 128

D_MIX = 2 * D_MODEL
W_GROUP = D_MIX // 4
S5_CH = 16
S5_N = 64
S5_G = W_GROUP // S5_CH
S5_CHUNK = 128
SSD_HEADS = 8
SSD_P = W_GROUP // SSD_HEADS
SSD_GROUPS = 2
SSD_N = 64
SSD_CHUNK = 128
CONV_W = 4
LRU_BLOCKS = 8
LRU_BD = W_GROUP // LRU_BLOCKS
LRU_C = 8.0
ML_HEADS = 4
ML_DQK = 64
ML_DV = W_GROUP // ML_HEADS
ML_CHUNK = 128
N_EXPERTS = 32
TOP_K = 4
D_FF = D_MODEL
SWIGLU_LIMIT = 7.0
SWIGLU_ALPHA = 1.702
MOE_BLOCK = 128
ALPHA_DEEPNORM = (2.0 * DEPTH) ** 0.25
BETA_DEEPNORM = (8.0 * DEPTH) ** -0.25
LN_EPS = 1e-5
RMS_EPS = 1e-6

PROJ_SPLITS = (
    W_GROUP,
    W_GROUP,
    SSD_GROUPS * SSD_N,
    SSD_GROUPS * SSD_N,
    W_GROUP,
    2 * SSD_HEADS,
    W_GROUP,
    W_GROUP,
    ML_HEADS * ML_DQK,
    ML_HEADS * ML_DQK,
    W_GROUP,
    W_GROUP,
    4 * ML_HEADS,
)
D_IN_PROJ = sum(PROJ_SPLITS)

kernel_name = 'hybrid_bidir_s5_ssd_rglru_mlstm_moe_encoder'


def layer_norm(x, g, b):
    mu = jnp.mean(x, axis=-1, keepdims=True)
    var = jnp.mean(jnp.square(x - mu), axis=-1, keepdims=True)
    return (x - mu) * lax.rsqrt(var + LN_EPS) * g + b


def rms_norm(x, g):
    return x * lax.rsqrt(jnp.mean(jnp.square(x), axis=-1, keepdims=True) + RMS_EPS) * g


def dwconv_centred(x, w, b):
    L = x.shape[1]
    left = CONV_W // 2
    xp = jnp.pad(x, ((0, 0), (left, CONV_W - 1 - left), (0, 0)))
    out = b
    for k in range(CONV_W):
        out = out + xp[:, k:k + L] * w[k]
    return out


def lin_combine(e1, e2):
    a1, b1 = e1
    a2, b2 = e2
    return a1 * a2, a2 * b1 + b2


def segsum(a):
    T = a.shape[-1]
    x = jnp.broadcast_to(a[..., None], a.shape + (T,))
    x = jnp.where(jnp.tril(jnp.ones((T, T), bool), -1), x, 0.0)
    cs = jnp.cumsum(x, axis=-2)
    return jnp.where(jnp.tril(jnp.ones((T, T), bool), 0), cs, -jnp.inf)


def s5_mixer(u, lam_re, lam_im, log_dt, b_re, b_im, c_re, c_im, d_skip, glu_w, glu_b):
    f32 = jnp.float32
    b, L, _ = u.shape
    nc = L // S5_CHUNK
    u_chunks = u.reshape(b, nc, S5_CHUNK, S5_G, S5_CH).swapaxes(0, 1)
    B_c = lax.complex(b_re.astype(f32), b_im.astype(f32))
    C_c = lax.complex(c_re.astype(f32), c_im.astype(f32))
    y = d_skip.astype(f32) * u
    for d in range(2):
        lam = lax.complex(lam_re[d].astype(f32), lam_im[d].astype(f32))
        dt = jnp.exp(log_dt[d].astype(f32))[:, None]
        lam_bar = jnp.exp(lam * dt)
        B_bar = ((lam_bar - 1.0) / lam)[:, :, None] * B_c
        reverse = d == 1

        def chunk_step(state, uc, lam_bar=lam_bar, B_bar=B_bar, reverse=reverse):
            bu = jnp.einsum('bqgi,gni->bqgn', uc, B_bar)
            edge = -1 if reverse else 0
            bu = bu.at[:, edge].add(lam_bar * state)
            a = jnp.broadcast_to(lam_bar, bu.shape)
            _, states = lax.associative_scan(lin_combine, (a, bu), axis=1, reverse=reverse)
            y_c = jnp.einsum('bqgn,gin->bqgi', states, C_c).real
            return states[:, edge], y_c

        init = jnp.zeros((b, S5_G, S5_N), jnp.complex64)
        _, ys = lax.scan(chunk_step, init, u_chunks, reverse=reverse)
        y = y + ys.swapaxes(0, 1).reshape(b, L, W_GROUP)
    g = jax.nn.gelu(y)
    return g * jax.nn.sigmoid(g @ glu_w + glu_b)


def ssd_chunked(x, dt, A, B, C):
    b, L, H, P = x.shape
    N = B.shape[-1]
    Q = SSD_CHUNK
    nc = L // Q
    xd = (x * dt[..., None]).reshape(b, nc, Q, H, P)
    Bc = B.reshape(b, nc, Q, H, N)
    Cc = C.reshape(b, nc, Q, H, N)
    a = (dt * A).reshape(b, nc, Q, H).transpose(0, 3, 1, 2)
    a_cs = jnp.cumsum(a, axis=-1)
    Lmat = jnp.exp(segsum(a))
    scores = jnp.einsum('bclhn,bcshn->bhcls', Cc, Bc) * Lmat
    y_diag = jnp.einsum('bhcls,bcshp->bclhp', scores, xd)
    decay_states = jnp.exp(a_cs[..., -1:] - a_cs).transpose(0, 2, 3, 1)
    states = jnp.einsum('bclhn,bclhp->bchpn', Bc * decay_states[..., None], xd)
    states = jnp.concatenate([jnp.zeros_like(states[:, :1]), states], axis=1)
    decay_chunk = jnp.exp(segsum(jnp.pad(a_cs[..., -1], ((0, 0), (0, 0), (1, 0)))))
    prev_states = jnp.einsum('bhzc,bchpn->bzhpn', decay_chunk, states)[:, :-1]
    y_off = jnp.einsum('bclhn,bchpn->bclhp', Cc, prev_states) * jnp.exp(a_cs).transpose(0, 2, 3, 1)[..., None]
    return (y_diag + y_off).reshape(b, L, H, P)


def ssd_mixer(x, Bp, Cp, z, dt_raw, conv_w, conv_b, a_log, dt_bias, d_skip, norm_g):
    b, L, _ = x.shape
    xbc = jax.nn.silu(dwconv_centred(jnp.concatenate([x, Bp, Cp], axis=-1), conv_w, conv_b))
    xs, Bs, Cs = jnp.split(xbc, [W_GROUP, W_GROUP + SSD_GROUPS * SSD_N], axis=-1)
    xs = xs.reshape(b, L, SSD_HEADS, SSD_P)
    rep = SSD_HEADS // SSD_GROUPS
    Bs = jnp.repeat(Bs.reshape(b, L, SSD_GROUPS, SSD_N), rep, axis=2)
    Cs = jnp.repeat(Cs.reshape(b, L, SSD_GROUPS, SSD_N), rep, axis=2)
    dt_raw = dt_raw.reshape(b, L, 2, SSD_HEADS)
    y = d_skip.astype(jnp.float32)[:, None] * xs
    for d in range(2):
        dt = jax.nn.softplus(dt_raw[:, :, d] + dt_bias[d])
        A = -jnp.exp(a_log[d].astype(jnp.float32))
        if d == 0:
            y = y + ssd_chunked(xs, dt, A, Bs, Cs)
        else:
            fl = lambda t: jnp.flip(t, axis=1)
            y = y + fl(ssd_chunked(fl(xs), fl(dt), A, fl(Bs), fl(Cs)))
    y = y.reshape(b, L, W_GROUP) * jax.nn.silu(z)
    return rms_norm(y, norm_g)


def rglru_mixer(xr, gate, conv_w, conv_b, wa, ba, wx, bx, lam):
    b, L, _ = xr.shape
    x = dwconv_centred(xr, conv_w, conv_b)
    xb = x.reshape(b, L, LRU_BLOCKS, LRU_BD)
    h = jnp.zeros_like(x)
    for d in range(2):
        r = jax.nn.sigmoid(jnp.einsum('blnd,nde->blne', xb, wa[d]).reshape(b, L, W_GROUP) + ba[d])
        i = jax.nn.sigmoid(jnp.einsum('blnd,nde->blne', xb, wx[d]).reshape(b, L, W_GROUP) + bx[d])
        log_a = -LRU_C * r * jax.nn.softplus(-lam[d].astype(jnp.float32))
        a = jnp.exp(log_a)
        inp = jnp.sqrt(-jnp.expm1(2.0 * log_a)) * (i * x)
        _, hd = lax.associative_scan(lin_combine, (a, inp), axis=1, reverse=(d == 1))
        h = h + hd
    return h * jax.nn.gelu(gate)


def mlstm_chunkwise(q, k, v, i_pre, log_f):
    b, L, H, _ = q.shape
    Q = ML_CHUNK
    nc = L // Q
    to_chunks = lambda t: t.reshape((b, nc, Q) + t.shape[2:]).swapaxes(0, 1)
    causal = jnp.tril(jnp.ones((Q, Q), bool))[None, :, :, None]

    def step(carry, inp):
        Cs, ns, m = carry
        qi, ki, vi, ii, fi = inp
        F = jnp.cumsum(fi, axis=1)
        Dlog = F[:, :, None, :] - F[:, None, :, :] + ii[:, None, :, :]
        Dlog = jnp.where(causal, Dlog, -jnp.inf)
        inter = F + m[:, None, :]
        m_t = jnp.maximum(jnp.max(Dlog, axis=2), inter)
        Wts = jnp.exp(Dlog - m_t[:, :, None, :])
        w_inter = jnp.exp(inter - m_t)
        s = jnp.einsum('bthd,bshd->btsh', qi, ki) * Wts
        num = jnp.einsum('btsh,bshe->bthe', s, vi) + w_inter[..., None] * jnp.einsum('bhed,bthd->bthe', Cs, qi)
        den = jnp.sum(s, axis=2) + w_inter * jnp.einsum('bhd,bthd->bth', ns, qi)
        h = num / jnp.maximum(jnp.abs(den), jnp.exp(-m_t))[..., None]
        F_end = F[:, -1]
        w_log = F_end[:, None, :] - F + ii
        m_new = jnp.maximum(F_end + m, jnp.max(w_log, axis=1))
        w_s = jnp.exp(w_log - m_new[:, None, :])
        carry_scale = jnp.exp(F_end + m - m_new)
        C_new = carry_scale[..., None, None] * Cs + jnp.einsum('bsh,bshe,bshd->bhed', w_s, vi, ki)
        n_new = carry_scale[..., None] * ns + jnp.einsum('bsh,bshd->bhd', w_s, ki)
        return (C_new, n_new, m_new), h

    init = (jnp.zeros((b, H, ML_DV, ML_DQK), jnp.float32), jnp.zeros((b, H, ML_DQK), jnp.float32),
            jnp.zeros((b, H), jnp.float32))
    _, hs = lax.scan(step, init, (to_chunks(q), to_chunks(k), to_chunks(v), to_chunks(i_pre), to_chunks(log_f)))
    return hs.swapaxes(0, 1).reshape(b, L, H, ML_DV)


def mlstm_mixer(q, k, v, o, gates, gate_b, norm_g):
    b, L, _ = q.shape
    q = q.reshape(b, L, ML_HEADS, ML_DQK) * (ML_DQK ** -0.5)
    k = k.reshape(b, L, ML_HEADS, ML_DQK)
    v = v.reshape(b, L, ML_HEADS, ML_DV)
    g = gates.reshape(b, L, 2, 2, ML_HEADS) + gate_b
    h = jnp.zeros_like(v)
    for d in range(2):
        i_pre = g[:, :, d, 0]
        log_f = jax.nn.log_sigmoid(g[:, :, d, 1])
        if d == 0:
            h = h + mlstm_chunkwise(q, k, v, i_pre, log_f)
        else:
            fl = lambda t: jnp.flip(t, axis=1)
            h = h + fl(mlstm_chunkwise(fl(q), fl(k), fl(v), fl(i_pre), fl(log_f)))
    h = rms_norm(h, norm_g.reshape(ML_HEADS, ML_DV)).reshape(b, L, W_GROUP)
    return h * jax.nn.sigmoid(o)


def mixer_layer(h, p, l):
    points = [sum(PROJ_SPLITS[:i + 1]) for i in range(len(PROJ_SPLITS) - 1)]
    proj = h @ p['w_in'][l]
    (u_s5, x_ssd, b_ssd, c_ssd, z_ssd, dt_ssd, x_lru, g_lru,
     q_ml, k_ml, v_ml, o_ml, gates_ml) = jnp.split(proj, points, axis=-1)
    y_a = s5_mixer(u_s5, p['s5_lam_re'][l], p['s5_lam_im'][l], p['s5_log_dt'][l], p['s5_b_re'][l],
                   p['s5_b_im'][l], p['s5_c_re'][l], p['s5_c_im'][l], p['s5_d'][l], p['s5_glu_w'][l], p['s5_glu_b'][l])
    y_b = ssd_mixer(x_ssd, b_ssd, c_ssd, z_ssd, dt_ssd, p['ssd_conv_w'][l], p['ssd_conv_b'][l], p['ssd_a_log'][l],
                    p['ssd_dt_bias'][l], p['ssd_d'][l], p['ssd_norm_g'][l])
    y_c = rglru_mixer(x_lru, g_lru, p['lru_conv_w'][l], p['lru_conv_b'][l], p['lru_wa'][l], p['lru_ba'][l],
                      p['lru_wx'][l], p['lru_bx'][l], p['lru_lam'][l])
    y_d = mlstm_mixer(q_ml, k_ml, v_ml, o_ml, gates_ml, p['ml_gate_b'][l], p['ml_norm_g'][l])
    return jnp.concatenate([y_a, y_b, y_c, y_d], axis=-1) @ p['w_out'][l]


def moe_ffn(x, router_w, router_b, w_gate_up, b_gate_up, w_down, b_down):
    b, L, D = x.shape
    T = b * L
    A = T * TOP_K
    xf = x.reshape(T, D)
    logits = (xf @ router_w + router_b).astype(jnp.float32)
    top_val, top_idx = lax.top_k(logits, TOP_K)
    gates = jax.nn.softmax(top_val, axis=-1)
    e_a = top_idx.reshape(A).astype(jnp.int32)
    tok_a = jnp.repeat(jnp.arange(T, dtype=jnp.int32), TOP_K)
    w_a = gates.reshape(A)
    counts = jnp.zeros((N_EXPERTS,), jnp.int32).at[e_a].add(1)
    padded = (counts + MOE_BLOCK - 1) // MOE_BLOCK * MOE_BLOCK
    pad_end = jnp.cumsum(padded)
    pad_start = pad_end - padded
    raw_start = jnp.cumsum(counts) - counts
    order = jnp.argsort(e_a, stable=True)
    e_s = e_a[order]
    dest = pad_start[e_s] + (jnp.arange(A, dtype=jnp.int32) - raw_start[e_s])
    n_blocks = (A + N_EXPERTS * (MOE_BLOCK - 1) + MOE_BLOCK - 1) // MOE_BLOCK
    P = n_blocks * MOE_BLOCK
    buf_tok = jnp.full((P,), T, jnp.int32).at[dest].set(tok_a[order])
    buf_w = jnp.zeros((P,), jnp.float32).at[dest].set(w_a[order])
    block_start = jnp.arange(n_blocks, dtype=jnp.int32) * MOE_BLOCK
    block_exp = jnp.minimum(jnp.sum(block_start[:, None] >= pad_end[None, :], axis=1), N_EXPERTS - 1)
    x_pad = jnp.concatenate([xf, jnp.zeros((1, D), xf.dtype)], axis=0)

    def expert_block(args):
        toks, e = args
        gu = x_pad[toks] @ w_gate_up[e] + b_gate_up[e]
        g = jnp.minimum(gu[:, 0::2], SWIGLU_LIMIT)
        u = jnp.clip(gu[:, 1::2], -SWIGLU_LIMIT, SWIGLU_LIMIT)
        hdn = (u + 1.0) * g * jax.nn.sigmoid(SWIGLU_ALPHA * g)
        return hdn @ w_down[e] + b_down[e]

    y_buf = lax.map(expert_block, (buf_tok.reshape(n_blocks, MOE_BLOCK), block_exp))
    y = jnp.zeros((T + 1, D), jnp.float32).at[buf_tok].add(y_buf.reshape(P, D) * buf_w[:, None])
    return y[:T].reshape(b, L, D)


def trunk(x, c, p):
    x = x.astype(jnp.float32)
    cs = jax.nn.silu(c.astype(jnp.float32))
    for l in range(DEPTH):
        mod = cs @ p['ada_w'][l] + p['ada_b'][l]
        sh1, sc1, g1, sh2, sc2, g2 = jnp.split(mod[:, None, :], 6, axis=-1)
        h = x * (1.0 + sc1) + sh1
        x = layer_norm(ALPHA_DEEPNORM * x + g1 * mixer_layer(h, p, l), p['ln1_g'][l], p['ln1_b'][l])
        h = x * (1.0 + sc2) + sh2
        y = moe_ffn(h, p['router_w'][l], p['router_b'][l], p['moe_w_gate_up'][l], p['moe_b_gate_up'][l],
                    p['moe_w_down'][l], p['moe_b_down'][l])
        x = layer_norm(ALPHA_DEEPNORM * x + g2 * y, p['ln2_g'][l], p['ln2_b'][l])
    return x


def setup_inputs(seed: int = 0) -> dict:
    key = jax.random.key(seed)
    keys = iter(jax.random.split(key, 64))

    def nrm(shape, scale):
        return scale * jax.random.normal(next(keys), shape, jnp.float32)

    def unif(shape, lo, hi):
        return jax.random.uniform(next(keys), shape, jnp.float32, lo, hi)

    def gain(shape):
        return 1.0 + nrm(shape, 0.02)

    D, E, F, W = D_MODEL, N_EXPERTS, D_FF, W_GROUP
    conv_ch = W + 2 * SSD_GROUPS * SSD_N
    n_idx = jnp.arange(S5_N, dtype=jnp.float32)
    dt0 = jnp.exp(unif((DEPTH, 2, SSD_HEADS), math.log(1e-3), math.log(1e-1)))
    a0 = unif((DEPTH, 2, W), 0.9, 0.999) ** (1.0 / LRU_C)
    i_bias = nrm((DEPTH, 2, ML_HEADS), 0.1)
    f_bias = jnp.linspace(3.0, 6.0, ML_HEADS, dtype=jnp.float32) + nrm((DEPTH, 2, ML_HEADS), 0.1)
    return {
        'x_prompt': nrm((BATCH, SEQ, D), 1.0),
        'x_sample': nrm((DEC_BATCH, DEC_SEQ, D), 1.0),
        'c_prompt': nrm((BATCH, D), 1.0),
        'c_sample': nrm((DEC_BATCH, D), 1.0),
        'ada_w': nrm((DEPTH, D, 6 * D), 0.5 * D ** -0.5),
        'ada_b': nrm((DEPTH, 6 * D), 0.02),
        'w_in': nrm((DEPTH, D, D_IN_PROJ), D ** -0.5),
        's5_lam_re': -0.5 + nrm((DEPTH, 2, S5_G, S5_N), 0.01),
        's5_lam_im': math.pi * n_idx + nrm((DEPTH, 2, S5_G, S5_N), 0.01),
        's5_log_dt': unif((DEPTH, 2, S5_G), math.log(1e-3), math.log(1e-1)),
        's5_b_re': nrm((DEPTH, S5_G, S5_N, S5_CH), (2.0 * S5_CH) ** -0.5),
        's5_b_im': nrm((DEPTH, S5_G, S5_N, S5_CH), (2.0 * S5_CH) ** -0.5),
        's5_c_re': nrm((DEPTH, S5_G, S5_CH, S5_N), (2.0 * S5_N) ** -0.5),
        's5_c_im': nrm((DEPTH, S5_G, S5_CH, S5_N), (2.0 * S5_N) ** -0.5),
        's5_d': nrm((DEPTH, W), 1.0),
        's5_glu_w': nrm((DEPTH, W, W), W ** -0.5),
        's5_glu_b': nrm((DEPTH, W), 0.02),
        'ssd_conv_w': nrm((DEPTH, CONV_W, conv_ch), CONV_W ** -0.5),
        'ssd_conv_b': nrm((DEPTH, conv_ch), 0.02),
        'ssd_a_log': jnp.log(unif((DEPTH, 2, SSD_HEADS), 1.0, 16.0)),
        'ssd_dt_bias': dt0 + jnp.log(-jnp.expm1(-dt0)),
        'ssd_d': 1.0 + nrm((DEPTH, SSD_HEADS), 0.1),
        'ssd_norm_g': gain((DEPTH, W)),
        'lru_conv_w': nrm((DEPTH, CONV_W, W), CONV_W ** -0.5),
        'lru_conv_b': nrm((DEPTH, W), 0.02),
        'lru_wa': nrm((DEPTH, 2, LRU_BLOCKS, LRU_BD, LRU_BD), LRU_BD ** -0.5),
        'lru_ba': nrm((DEPTH, 2, W), 0.02),
        'lru_wx': nrm((DEPTH, 2, LRU_BLOCKS, LRU_BD, LRU_BD), LRU_BD ** -0.5),
        'lru_bx': nrm((DEPTH, 2, W), 0.02),
        'lru_lam': jnp.log(a0) - jnp.log1p(-a0),
        'ml_gate_b': jnp.stack([i_bias, f_bias], axis=2),
        'ml_norm_g': gain((DEPTH, W)),
        'w_out': nrm((DEPTH, D_MIX, D), BETA_DEEPNORM * D_MIX ** -0.5),
        'ln1_g': gain((DEPTH, D)),
        'ln1_b': nrm((DEPTH, D), 0.02),
        'router_w': nrm((DEPTH, D, E), D ** -0.5),
        'router_b': nrm((DEPTH, E), 0.01),
        'moe_w_gate_up': nrm((DEPTH, E, D, 2 * F), D ** -0.5),
        'moe_b_gate_up': nrm((DEPTH, E, 2 * F), 0.02),
        'moe_w_down': nrm((DEPTH, E, F, D), BETA_DEEPNORM * F ** -0.5),
        'moe_b_down': nrm((DEPTH, E, D), 0.02 * BETA_DEEPNORM),
        'ln2_g': gain((DEPTH, D)),
        'ln2_b': nrm((DEPTH, D), 0.02),
    }


def reference(x_prompt, x_sample, c_prompt, c_sample, ada_w, ada_b, w_in, s5_lam_re, s5_lam_im, s5_log_dt,
              s5_b_re, s5_b_im, s5_c_re, s5_c_im, s5_d, s5_glu_w, s5_glu_b, ssd_conv_w, ssd_conv_b, ssd_a_log,
              ssd_dt_bias, ssd_d, ssd_norm_g, lru_conv_w, lru_conv_b, lru_wa, lru_ba, lru_wx, lru_bx, lru_lam,
              ml_gate_b, ml_norm_g, w_out, ln1_g, ln1_b, router_w, router_b, moe_w_gate_up, moe_b_gate_up,
              moe_w_down, moe_b_down, ln2_g, ln2_b):
    p = {
        'ada_w': ada_w, 'ada_b': ada_b, 'w_in': w_in,
        's5_lam_re': s5_lam_re, 's5_lam_im': s5_lam_im, 's5_log_dt': s5_log_dt,
        's5_b_re': s5_b_re, 's5_b_im': s5_b_im, 's5_c_re': s5_c_re, 's5_c_im': s5_c_im,
        's5_d': s5_d, 's5_glu_w': s5_glu_w, 's5_glu_b': s5_glu_b,
        'ssd_conv_w': ssd_conv_w, 'ssd_conv_b': ssd_conv_b, 'ssd_a_log': ssd_a_log,
        'ssd_dt_bias': ssd_dt_bias, 'ssd_d': ssd_d, 'ssd_norm_g': ssd_norm_g,
        'lru_conv_w': lru_conv_w, 'lru_conv_b': lru_conv_b, 'lru_wa': lru_wa, 'lru_ba': lru_ba,
        'lru_wx': lru_wx, 'lru_bx': lru_bx, 'lru_lam': lru_lam,
        'ml_gate_b': ml_gate_b, 'ml_norm_g': ml_norm_g, 'w_out': w_out,
        'ln1_g': ln1_g, 'ln1_b': ln1_b, 'router_w': router_w, 'router_b': router_b,
        'moe_w_gate_up': moe_w_gate_up, 'moe_b_gate_up': moe_b_gate_up,
        'moe_w_down': moe_w_down, 'moe_b_down': moe_b_down, 'ln2_g': ln2_g, 'ln2_b': ln2_b,
    }
    y_prompt = trunk(x_prompt, c_prompt, p).astype(x_prompt.dtype)
    y_sample = trunk(x_sample, c_sample, p).astype(x_sample.dtype)
    return (y_prompt, y_sample)
```

```python
import functools
import math

import numpy as np
import jax
import jax.numpy as jnp
from jax import lax
from jax.experimental import pallas as pl
from jax.experimental.pallas import tpu as pltpu

F32 = jnp.float32
BF16 = jnp.bfloat16
HIGHEST = lax.Precision.HIGHEST

D = 1024
DEPTH = 4
WG = 512
S5_G, S5_CH, S5_N = 32, 16, 64
SSD_H, SSD_P, SSD_NG, SSD_N = 8, 64, 2, 64
LRU_NB, LRU_BD, LRU_C = 8, 64, 8.0
ML_H, ML_DQK, ML_DV = 4, 64, 128
N_EXP, TOP_K, D_FF = 32, 4, 1024
SWIGLU_LIMIT, SWIGLU_ALPHA = 7.0, 1.702
ALPHA_DN = (2.0 * DEPTH) ** 0.25
LN_EPS, RMS_EPS = 1e-5, 1e-6

LANE = 128
NB = 8
CH = 128
ROWS = CH * NB
PW = 4480
K1_TN = 640
K3_TM = 512
MOE_RB = 128
VMEM_LIMIT = 56 * 1024 * 1024

COL_U, COL_XS, COL_Z, COL_XL, COL_GL, COL_V, COL_O = 0, 1, 2, 3, 4, 5, 6
COL_Q, COL_K, COL_BC = 14, 15, 16
COL_SM = 34
SM_DT, SM_GATE = 0, 16


def _sigmoid(x):
    return 1.0 / (1.0 + jnp.exp(-x))


def _silu(x):
    return x * _sigmoid(x)


def _softplus(x):
    return jnp.maximum(x, 0.0) + jnp.log(1.0 + jnp.exp(-jnp.abs(x)))


def _gelu_tanh(x):
    return 0.5 * x * (1.0 + jnp.tanh(math.sqrt(2.0 / math.pi) * (x + 0.044715 * (x * x * x))))


def _layer_norm(v, g, b):
    mu = jnp.mean(v, axis=-1, keepdims=True)
    vc = v - mu
    var = jnp.mean(vc * vc, axis=-1, keepdims=True)
    return vc * lax.rsqrt(var + LN_EPS) * g + b


def _bdot(a, b):
    return jnp.dot(a.astype(BF16), b.astype(BF16), preferred_element_type=F32)


def _per_slab(x, m):
    n = x.shape[0] // NB
    return (x.reshape(n, NB, x.shape[1]) * m[None]).reshape(x.shape)


def _per_slab_add(x, m):
    n = x.shape[0] // NB
    return (x.reshape(n, NB, x.shape[1]) + m[None]).reshape(x.shape)


def _cparams(sem):
    return pltpu.CompilerParams(dimension_semantics=sem, vmem_limit_bytes=VMEM_LIMIT)


def _mod_kernel(c_ref, w_ref, b_ref, o_ref):
    o_ref[...] = _bdot(_silu(c_ref[...]), w_ref[...]) + b_ref[...]


def _mod_call(c_all, w, b):
    n = c_all.shape[0]
    return pl.pallas_call(
        _mod_kernel,
        grid=(6,),
        in_specs=[pl.BlockSpec((n, D), lambda j: (0, 0)),
                  pl.BlockSpec((D, D), lambda j: (0, j)),
                  pl.BlockSpec((1, D), lambda j: (0, j))],
        out_specs=pl.BlockSpec((n, D), lambda j: (0, j)),
        out_shape=jax.ShapeDtypeStruct((n, 6 * D), F32),
        compiler_params=_cparams(("arbitrary",)),
    )(c_all, w, b)


def _inproj_kernel(gid_ref, x_ref, sc_ref, sh_ref, w_ref, o_ref, h_s):
    @pl.when(pl.program_id(1) == 0)
    def _():
        h = _per_slab_add(_per_slab(x_ref[...], 1.0 + sc_ref[0]), sh_ref[0])
        h_s[...] = h.astype(BF16)

    res = jnp.dot(h_s[...], w_ref[...], preferred_element_type=F32)
    for k in range(K1_TN // LANE):
        o_ref[k] = res[:, LANE * k:LANE * (k + 1)]


def _inproj_call(gid, x, mod3, w_in_p):
    R = x.shape[0]
    nch = R // ROWS
    gs = pltpu.PrefetchScalarGridSpec(
        num_scalar_prefetch=1,
        grid=(nch, PW // K1_TN),
        in_specs=[pl.BlockSpec((ROWS, D), lambda i, j, gid: (i, 0)),
                  pl.BlockSpec((1, NB, D), lambda i, j, gid: (gid[i], 0, 1)),
                  pl.BlockSpec((1, NB, D), lambda i, j, gid: (gid[i], 0, 0)),
                  pl.BlockSpec((D, K1_TN), lambda i, j, gid: (0, j))],
        out_specs=pl.BlockSpec((K1_TN // LANE, ROWS, LANE), lambda i, j, gid: (j, i, 0)),
        scratch_shapes=[pltpu.VMEM((ROWS, D), BF16)],
    )
    return pl.pallas_call(
        _inproj_kernel, grid_spec=gs,
        out_shape=jax.ShapeDtypeStruct((PW // LANE, R, LANE), F32),
        compiler_params=_cparams(("arbitrary", "arbitrary")),
    )(gid, x, mod3, mod3, w_in_p)


def _chunk_idx(nch, reverse):
    return (lambda c: nch - 1 - c) if reverse else (lambda c: c)


def _main_spec(width, col, nch, reverse):
    ci = _chunk_idx(nch, reverse)
    return pl.BlockSpec((width // LANE, ROWS, LANE), lambda c, *_: (col, ci(c), 0))


def _prev_spec(width, col, nch, reverse):
    ci = _chunk_idx(nch, reverse)
    per = ROWS // (2 * NB)
    return pl.BlockSpec((width // LANE, 2 * NB, LANE), lambda c, *_: (col, jnp.maximum(ci(c) * per - 1, 0), 0))


def _next_spec(width, col, nch, reverse):
    ci = _chunk_idx(nch, reverse)
    per = ROWS // NB
    return pl.BlockSpec((width // LANE, NB, LANE),
                        lambda c, *_: (col, jnp.minimum((ci(c) + 1) * per, nch * per - 1), 0))


def _const_spec(shape):
    nd = len(shape)
    return pl.BlockSpec(shape, lambda c, *_: (0,) * nd)


def _cat(ref):
    return jnp.concatenate([ref[k] for k in range(ref.shape[0])], axis=1)


def _put(ref, val):
    for k in range(ref.shape[0]):
        ref[k] = val[:, LANE * k:LANE * (k + 1)]


def _fill_ext(ext_s, k0, x_ref, xp_ref, xn_ref, is_first, is_last):
    for k in range(x_ref.shape[0]):
        ext_s[k0 + k, 0:2 * NB, :] = jnp.where(is_first, 0.0, xp_ref[k])
        ext_s[k0 + k, 2 * NB:2 * NB + ROWS, :] = x_ref[k]
        ext_s[k0 + k, 2 * NB + ROWS:3 * NB + ROWS, :] = jnp.where(is_last, 0.0, xn_ref[k])


def _dwconv(ext_s, k, w_ref, b_ref):
    sl = slice(LANE * k, LANE * (k + 1))
    out = b_ref[:, sl] + w_ref[0:1, sl] * ext_s[k, 0:ROWS, :]
    for j in range(1, 4):
        out = out + w_ref[j:j + 1, sl] * ext_s[k, j * NB:j * NB + ROWS, :]
    return out


def _tri(reverse):
    r = lax.broadcasted_iota(jnp.int32, (CH, CH), 0)
    c = lax.broadcasted_iota(jnp.int32, (CH, CH), 1)
    return (c >= r) if reverse else (c <= r)


def _s5_kernel(*refs, nch, reverse, final):
    first_ref, last_ref = refs[0], refs[1]
    if final:
        (u_ref, yp_ref, bt_ref, lr_ref, li_ref, cm_ref, d_ref, gw_ref, gb_ref,
         o_ref, bu_s, st_s, carry_s) = refs[2:]
    else:
        u_ref, bt_ref, lr_ref, li_ref, cm_ref, o_ref, bu_s, st_s, carry_s = refs[2:]
    c = pl.program_id(0)
    cc = nch - 1 - c if reverse else c
    start = last_ref[cc] if reverse else first_ref[cc]

    @pl.when(start == 1)
    def _():
        carry_s[...] = jnp.zeros_like(carry_s)

    half = WG
    for j in range(4):
        bu_s[...] = jnp.dot(u_ref[j].astype(BF16), bt_ref[j], preferred_element_type=F32)
        lr = jnp.broadcast_to(lr_ref[j], (NB, half))
        li = jnp.broadcast_to(li_ref[j], (NB, half))

        def step(i, carry, lr=lr, li=li):
            sr, si = carry
            t = CH - 1 - i if reverse else i
            r0 = pl.multiple_of(t * NB, NB)
            br = bu_s[pl.ds(r0, NB), 0:half]
            bi = bu_s[pl.ds(r0, NB), half:2 * half]
            nr = lr * sr - li * si + br
            ni = lr * si + li * sr + bi
            st_s[pl.ds(r0, NB), 0:half] = nr
            st_s[pl.ds(r0, NB), half:2 * half] = ni
            return nr, ni

        lax.fori_loop(0, CH, step, (carry_s[j, :, 0:half], carry_s[j, :, half:2 * half]), unroll=8)
        e0 = (CH - 1) * NB if reverse else 0
        carry_s[j] = st_s[e0:e0 + NB, :]
        o_ref[j] = jnp.dot(st_s[...].astype(BF16), cm_ref[j], preferred_element_type=F32)

    if final:
        y = _cat(o_ref) + _cat(yp_ref) + d_ref[...] * _cat(u_ref)
        g = _gelu_tanh(y)
        _put(o_ref, g * _sigmoid(_bdot(g, gw_ref[...]) + gb_ref[...]))


def _s5_call(flags, proj, yprev, prm, d, *, reverse, final):
    first, last = flags
    R = proj.shape[1]
    nch = R // ROWS
    in_specs = [_main_spec(WG, COL_U, nch, reverse)]
    args = [proj]
    if final:
        in_specs.append(_main_spec(WG, 0, nch, reverse))
        args.append(yprev)
    in_specs += [_const_spec((4, 128, 2 * WG)), _const_spec((4, 1, WG)), _const_spec((4, 1, WG)),
                 _const_spec((4, 2 * WG, 128))]
    args += [prm['s5_bt'], prm['s5_lr'][d], prm['s5_li'][d], prm['s5_cm'][d]]
    if final:
        in_specs += [_const_spec((1, WG)), _const_spec((WG, WG)), _const_spec((1, WG))]
        args += [prm['s5_d'], prm['s5_glu_w'], prm['s5_glu_b']]
    gs = pltpu.PrefetchScalarGridSpec(
        num_scalar_prefetch=2, grid=(nch,), in_specs=in_specs,
        out_specs=_main_spec(WG, 0, nch, reverse),
        scratch_shapes=[pltpu.VMEM((ROWS, 2 * WG), F32), pltpu.VMEM((ROWS, 2 * WG), F32),
                        pltpu.VMEM((4, NB, 2 * WG), F32)],
    )
    return pl.pallas_call(
        functools.partial(_s5_kernel, nch=nch, reverse=reverse, final=final), grid_spec=gs,
        out_shape=jax.ShapeDtypeStruct((WG // LANE, R, LANE), F32),
        compiler_params=_cparams(("arbitrary",)),
    )(first, last, *args)


def _lru_kernel(*refs, nch, reverse, final):
    first_ref, last_ref = refs[0], refs[1]
    if final:
        (x_ref, xp_ref, xn_ref, gate_ref, hp_ref, cw_ref, cb_ref, wa_ref, ba_ref, wx_ref, bx_ref, sp_ref,
         o_ref, ext_s, a_s, inp_s, carry_s) = refs[2:]
    else:
        (x_ref, xp_ref, xn_ref, cw_ref, cb_ref, wa_ref, ba_ref, wx_ref, bx_ref, sp_ref,
         o_ref, ext_s, a_s, inp_s, carry_s) = refs[2:]
    c = pl.program_id(0)
    cc = nch - 1 - c if reverse else c
    is_first = first_ref[cc] == 1
    is_last = last_ref[cc] == 1
    start = is_last if reverse else is_first

    @pl.when(start)
    def _():
        carry_s[...] = jnp.zeros_like(carry_s)

    _fill_ext(ext_s, 0, x_ref, xp_ref, xn_ref, is_first, is_last)
    xc = jnp.concatenate([_dwconv(ext_s, k, cw_ref, cb_ref) for k in range(WG // LANE)], axis=1)
    xb = xc.astype(BF16)
    r = _sigmoid(jnp.dot(xb, wa_ref[...], preferred_element_type=F32) + ba_ref[...])
    i = _sigmoid(jnp.dot(xb, wx_ref[...], preferred_element_type=F32) + bx_ref[...])
    log_a = -LRU_C * r * sp_ref[...]
    a_s[...] = jnp.exp(log_a)
    inp_s[...] = jnp.sqrt(1.0 - jnp.exp(2.0 * log_a)) * (i * xc)

    def step(k, h):
        t = CH - 1 - k if reverse else k
        r0 = pl.multiple_of(t * NB, NB)
        h = a_s[pl.ds(r0, NB), :] * h + inp_s[pl.ds(r0, NB), :]
        for k in range(WG // LANE):
            o_ref[k, pl.ds(r0, NB), :] = h[:, LANE * k:LANE * (k + 1)]
        return h

    carry_s[...] = lax.fori_loop(0, CH, step, carry_s[...], unroll=8)
    if final:
        for k in range(WG // LANE):
            o_ref[k] = (o_ref[k] + hp_ref[k]) * _gelu_tanh(gate_ref[k])


def _lru_call(flags, proj, hprev, prm, d, *, reverse, final):
    first, last = flags
    R = proj.shape[1]
    nch = R // ROWS
    in_specs = [_main_spec(WG, COL_XL, nch, reverse), _prev_spec(WG, COL_XL, nch, reverse),
                _next_spec(WG, COL_XL, nch, reverse)]
    args = [proj, proj, proj]
    if final:
        in_specs += [_main_spec(WG, COL_GL, nch, reverse), _main_spec(WG, 0, nch, reverse)]
        args += [proj, hprev]
    in_specs += [_const_spec((4, WG)), _const_spec((1, WG)), _const_spec((WG, WG)), _const_spec((1, WG)),
                 _const_spec((WG, WG)), _const_spec((1, WG)), _const_spec((1, WG))]
    args += [prm['lru_conv_w'], prm['lru_conv_b'], prm['lru_wa'][d], prm['lru_ba'][d], prm['lru_wx'][d],
             prm['lru_bx'][d], prm['lru_sp'][d]]
    gs = pltpu.PrefetchScalarGridSpec(
        num_scalar_prefetch=2, grid=(nch,), in_specs=in_specs,
        out_specs=_main_spec(WG, 0, nch, reverse),
        scratch_shapes=[pltpu.VMEM((WG // LANE, ROWS + 3 * NB, LANE), F32), pltpu.VMEM((ROWS, WG), F32),
                        pltpu.VMEM((ROWS, WG), F32), pltpu.VMEM((NB, WG), F32)],
    )
    return pl.pallas_call(
        functools.partial(_lru_kernel, nch=nch, reverse=reverse, final=final), grid_spec=gs,
        out_shape=jax.ShapeDtypeStruct((WG // LANE, R, LANE), F32),
        compiler_params=_cparams(("arbitrary",)),
    )(first, last, *args)


def _ssd_kernel(*refs, nch, reverse, final):
    first_ref, last_ref = refs[0], refs[1]
    if final:
        (x_ref, xp_ref, xn_ref, bc_ref, bcp_ref, bcn_ref, sm_ref, z_ref, yp_ref,
         cw_ref, cb_ref, dtb_ref, arow_ref, drow_ref, ng_ref,
         o_ref, ext_s, xd_s, smd_s, yd_s, st_s) = refs[2:]
    else:
        (x_ref, xp_ref, xn_ref, bc_ref, bcp_ref, bcn_ref, sm_ref,
         cw_ref, cb_ref, dtb_ref, arow_ref,
         o_ref, ext_s, xd_s, smd_s, yd_s, st_s) = refs[2:]
    c = pl.program_id(0)
    cc = nch - 1 - c if reverse else c
    is_first = first_ref[cc] == 1
    is_last = last_ref[cc] == 1
    start = is_last if reverse else is_first

    @pl.when(start)
    def _():
        st_s[...] = jnp.zeros_like(st_s)

    CW = WG + 2 * SSD_NG * SSD_N
    _fill_ext(ext_s, 0, x_ref, xp_ref, xn_ref, is_first, is_last)
    _fill_ext(ext_s, WG // LANE, bc_ref, bcp_ref, bcn_ref, is_first, is_last)
    for k in range(CW // LANE):
        ext_s[k, 0:ROWS, :] = _silu(_dwconv(ext_s, k, cw_ref, cb_ref))
    for b in range(NB):
        for k in range(CW // LANE):
            xd_s[b, :, LANE * k:LANE * (k + 1)] = ext_s[k, pl.ds(b, CH, stride=NB), :]
        smd_s[b] = sm_ref[0, pl.ds(b, CH, stride=NB), :]

    mask = _tri(reverse)
    tri = jnp.where(mask, 1.0, 0.0)
    doff = SM_DT + (SSD_H if reverse else 0)
    edge = 0 if reverse else CH - 1
    NS = SSD_N

    def per_b(b, _):
        xb = xd_s[b]
        xs = xb[:, 0:WG]
        bm = xb[:, WG:WG + SSD_NG * NS]
        cm = xb[:, WG + SSD_NG * NS:CW]
        dtf = _softplus(smd_s[b] + dtb_ref[...])
        af = dtf * arow_ref[...]
        cs = jnp.dot(tri, af, precision=HIGHEST, preferred_element_type=F32)
        cs_t = cs.T
        dt_t = dtf.T
        bm_t = bm.T
        gmat = [_bdot(cm[:, NS * g:NS * (g + 1)], bm_t[NS * g:NS * (g + 1), :]) for g in range(SSD_NG)]
        for h in range(SSD_H):
            ci = doff + h
            g = h // (SSD_H // SSD_NG)
            col = cs[:, ci:ci + 1]
            row = cs_t[ci:ci + 1, :]
            lmat = jnp.where(mask, jnp.exp(jnp.minimum(col - row, 0.0)), 0.0)
            m = gmat[g] * lmat * dt_t[ci:ci + 1, :]
            xh = xs[:, SSD_P * h:SSD_P * (h + 1)]
            s_prev = st_s[b, h]
            y = _bdot(m, xh) + jnp.exp(col) * _bdot(cm[:, NS * g:NS * (g + 1)], s_prev)
            tot = cs[edge:edge + 1, ci:ci + 1]
            wcol = jnp.exp(tot - col) * dtf[:, ci:ci + 1]
            st_s[b, h] = jnp.exp(tot) * s_prev + _bdot(bm_t[NS * g:NS * (g + 1), :], xh * wcol)
            yd_s[b, :, SSD_P * h:SSD_P * (h + 1)] = y
        return 0

    lax.fori_loop(0, NB, per_b, 0)
    for b in range(NB):
        for k in range(WG // LANE):
            o_ref[k, pl.ds(b, CH, stride=NB), :] = yd_s[b, :, LANE * k:LANE * (k + 1)]
    if final:
        xs_all = jnp.concatenate([ext_s[k, 0:ROWS, :] for k in range(WG // LANE)], axis=1)
        y = (_cat(o_ref) + _cat(yp_ref) + drow_ref[...] * xs_all) * _silu(_cat(z_ref))
        ms = jnp.mean(y * y, axis=-1, keepdims=True)
        _put(o_ref, y * lax.rsqrt(ms + RMS_EPS) * ng_ref[...])


def _ssd_call(flags, proj, yprev, prm, *, reverse, final):
    first, last = flags
    R = proj.shape[1]
    nch = R // ROWS
    CW = WG + 2 * SSD_NG * SSD_N
    in_specs = [_main_spec(WG, COL_XS, nch, reverse), _prev_spec(WG, COL_XS, nch, reverse),
                _next_spec(WG, COL_XS, nch, reverse),
                _main_spec(256, COL_BC, nch, reverse), _prev_spec(256, COL_BC, nch, reverse),
                _next_spec(256, COL_BC, nch, reverse),
                _main_spec(128, COL_SM, nch, reverse)]
    args = [proj] * 7
    if final:
        in_specs += [_main_spec(WG, COL_Z, nch, reverse), _main_spec(WG, 0, nch, reverse)]
        args += [proj, yprev]
    in_specs += [_const_spec((4, CW)), _const_spec((1, CW)), _const_spec((1, 128)), _const_spec((1, 128))]
    args += [prm['ssd_conv_w'], prm['ssd_conv_b'], prm['ssd_dtb'], prm['ssd_arow']]
    if final:
        in_specs += [_const_spec((1, WG)), _const_spec((1, WG))]
        args += [prm['ssd_drow'], prm['ssd_norm_g']]
    gs = pltpu.PrefetchScalarGridSpec(
        num_scalar_prefetch=2, grid=(nch,), in_specs=in_specs,
        out_specs=_main_spec(WG, 0, nch, reverse),
        scratch_shapes=[pltpu.VMEM((CW // LANE, ROWS + 3 * NB, LANE), F32), pltpu.VMEM((NB, CH, CW), F32),
                        pltpu.VMEM((NB, CH, 128), F32), pltpu.VMEM((NB, CH, WG), F32),
                        pltpu.VMEM((NB, SSD_H, SSD_N, SSD_P), F32)],
    )
    return pl.pallas_call(
        functools.partial(_ssd_kernel, nch=nch, reverse=reverse, final=final), grid_spec=gs,
        out_shape=jax.ShapeDtypeStruct((WG // LANE, R, LANE), F32),
        compiler_params=_cparams(("arbitrary",)),
    )(first, last, *args)


NEG = -1e30


def _mlstm_kernel(*refs, nch, reverse, final):
    first_ref, last_ref = refs[0], refs[1]
    if final:
        (q_ref, k_ref, v_ref, sm_ref, og_ref, hp_ref, gb_ref, ng_ref,
         o_ref, qd_s, kd_s, vd_s, smd_s, hd_s, c_s, n_s, m_s) = refs[2:]
    else:
        (q_ref, k_ref, v_ref, sm_ref, gb_ref,
         o_ref, qd_s, kd_s, vd_s, smd_s, hd_s, c_s, n_s, m_s) = refs[2:]
    c = pl.program_id(0)
    cc = nch - 1 - c if reverse else c
    start = (last_ref[cc] if reverse else first_ref[cc]) == 1

    @pl.when(start)
    def _():
        c_s[...] = jnp.zeros_like(c_s)
        n_s[...] = jnp.zeros_like(n_s)
        m_s[...] = jnp.zeros_like(m_s)

    for b in range(NB):
        for src, dst in ((q_ref, qd_s), (k_ref, kd_s), (v_ref, vd_s)):
            for k in range(src.shape[0]):
                dst[b, :, LANE * k:LANE * (k + 1)] = src[k, pl.ds(b, CH, stride=NB), :]
        smd_s[b] = sm_ref[0, pl.ds(b, CH, stride=NB), :]

    mask = _tri(reverse)
    tri = jnp.where(mask, 1.0, 0.0)
    goff = SM_GATE + (2 * ML_H if reverse else 0)
    edge = 0 if reverse else CH - 1

    def per_b(b, _):
        sm = smd_s[b] + gb_ref[...]
        logf = -_softplus(-sm)
        fc_all = jnp.dot(tri, logf, precision=HIGHEST, preferred_element_type=F32)
        fr_all = fc_all.T
        sm_t = sm.T
        qb = qd_s[b] * (ML_DQK ** -0.5)
        kb = kd_s[b]
        kb_t = kb.T
        vb = vd_s[b]
        for h in range(ML_H):
            ci = goff + h
            cf = goff + ML_H + h
            fc = fc_all[:, cf:cf + 1]
            fr = fr_all[cf:cf + 1, :]
            ir = sm_t[ci:ci + 1, :]
            ic = sm[:, ci:ci + 1]
            m_prev = m_s[b, h]
            dlog = jnp.where(mask, fc - fr + ir, NEG)
            inter = fc + m_prev
            m_t = jnp.maximum(jnp.max(dlog, axis=1, keepdims=True), inter)
            wts = jnp.exp(dlog - m_t)
            w_inter = jnp.exp(inter - m_t)
            qh = qb[:, ML_DQK * h:ML_DQK * (h + 1)]
            kh = kb[:, ML_DQK * h:ML_DQK * (h + 1)]
            kh_t = kb_t[ML_DQK * h:ML_DQK * (h + 1), :]
            vh = vb[:, ML_DV * h:ML_DV * (h + 1)]
            ct_prev = c_s[b, h]
            n_prev = n_s[b, h]
            s = _bdot(qh, kh_t) * wts
            num = _bdot(s, vh) + w_inter * _bdot(qh, ct_prev)
            den = jnp.sum(s, axis=1, keepdims=True) + w_inter * jnp.sum(qh * n_prev, axis=1, keepdims=True)
            hd_s[b, :, ML_DV * h:ML_DV * (h + 1)] = num / jnp.maximum(jnp.abs(den), jnp.exp(-m_t))
            f_end = fc_all[edge:edge + 1, cf:cf + 1]
            w_log = f_end - fc + ic
            m_new = jnp.maximum(f_end + m_prev, jnp.max(w_log, axis=0, keepdims=True))
            w_s = jnp.exp(w_log - m_new)
            scale = jnp.exp(f_end + m_prev - m_new)
            c_s[b, h] = scale * ct_prev + _bdot(kh_t, vh * w_s)
            n_s[b, h] = scale * n_prev + jnp.sum(kh * w_s, axis=0, keepdims=True)
            m_s[b, h] = m_new
        return 0

    lax.fori_loop(0, NB, per_b, 0)
    for b in range(NB):
        for k in range(WG // LANE):
            o_ref[k, pl.ds(b, CH, stride=NB), :] = hd_s[b, :, LANE * k:LANE * (k + 1)]
    if final:
        assert ML_DV == LANE
        for h in range(ML_H):
            hh = o_ref[h] + hp_ref[h]
            ms = jnp.mean(hh * hh, axis=-1, keepdims=True)
            o_ref[h] = hh * lax.rsqrt(ms + RMS_EPS) * ng_ref[:, LANE * h:LANE * (h + 1)] * _sigmoid(og_ref[h])


def _mlstm_call(flags, proj, hprev, prm, *, reverse, final):
    first, last = flags
    R = proj.shape[1]
    nch = R // ROWS
    in_specs = [_main_spec(256, COL_Q, nch, reverse), _main_spec(256, COL_K, nch, reverse),
                _main_spec(WG, COL_V, nch, reverse), _main_spec(128, COL_SM, nch, reverse)]
    args = [proj] * 4
    if final:
        in_specs += [_main_spec(WG, COL_O, nch, reverse), _main_spec(WG, 0, nch, reverse)]
        args += [proj, hprev]
    in_specs += [_const_spec((1, 128))]
    args += [prm['ml_gb']]
    if final:
        in_specs += [_const_spec((1, WG))]
        args += [prm['ml_norm_g']]
    gs = pltpu.PrefetchScalarGridSpec(
        num_scalar_prefetch=2, grid=(nch,), in_specs=in_specs,
        out_specs=_main_spec(WG, 0, nch, reverse),
        scratch_shapes=[pltpu.VMEM((NB, CH, 256), F32), pltpu.VMEM((NB, CH, 256), F32),
                        pltpu.VMEM((NB, CH, WG), F32), pltpu.VMEM((NB, CH, 128), F32),
                        pltpu.VMEM((NB, CH, WG), F32),
                        pltpu.VMEM((NB, ML_H, ML_DQK, ML_DV), F32), pltpu.VMEM((NB, ML_H, 1, ML_DQK), F32),
                        pltpu.VMEM((NB, ML_H, 1, 1), F32)],
    )
    return pl.pallas_call(
        functools.partial(_mlstm_kernel, nch=nch, reverse=reverse, final=final), grid_spec=gs,
        out_shape=jax.ShapeDtypeStruct((WG // LANE, R, LANE), F32),
        compiler_params=_cparams(("arbitrary",)),
    )(first, last, *args)


def _outproj_kernel(gid_ref, ya_ref, yb_ref, yc_ref, yd_ref, x_ref, g1_ref, sc_ref, sh_ref, wo_ref,
                    lg_ref, lb_ref, rw_ref, rb_ref, x1_ref, h2_ref, ti_ref, tw_ref):
    o = _bdot(_cat(ya_ref), wo_ref[0:WG, :])
    o = o + _bdot(_cat(yb_ref), wo_ref[WG:2 * WG, :])
    o = o + _bdot(_cat(yc_ref), wo_ref[2 * WG:3 * WG, :])
    o = o + _bdot(_cat(yd_ref), wo_ref[3 * WG:4 * WG, :])
    v = ALPHA_DN * x_ref[...] + _per_slab(o, g1_ref[0])
    x1 = _layer_norm(v, lg_ref[...], lb_ref[...])
    x1_ref[...] = x1
    h2 = _per_slab_add(_per_slab(x1, 1.0 + sc_ref[0]), sh_ref[0])
    h2_ref[...] = h2
    logit = jnp.dot(h2, rw_ref[...], precision=HIGHEST, preferred_element_type=F32) + rb_ref[...]
    tm = logit.shape[0]
    lane = lax.broadcasted_iota(jnp.int32, (tm, N_EXP), 1)
    lane8 = lax.broadcasted_iota(jnp.int32, (tm, 8), 1)
    idx8 = jnp.zeros((tm, 8), jnp.int32)
    val8 = jnp.zeros((tm, 8), F32)
    top0 = None
    den = None
    for k in range(TOP_K):
        mx = jnp.max(logit, axis=1, keepdims=True)
        sel = jnp.min(jnp.where(logit == mx, lane, N_EXP), axis=1, keepdims=True)
        if k == 0:
            top0 = mx
        ek = jnp.exp(mx - top0)
        den = ek if k == 0 else den + ek
        idx8 = jnp.where(lane8 == k, sel, idx8)
        val8 = jnp.where(lane8 == k, ek, val8)
        logit = jnp.where(lane == sel, -jnp.inf, logit)
    ti_ref[...] = idx8
    tw_ref[...] = val8 / den


def _outproj_call(gid3, ys, x, mod3, prm):
    R = x.shape[0]
    per = ROWS // K3_TM
    rspec = lambda w: pl.BlockSpec((K3_TM, w), lambda i, gid: (i, 0))
    yspec = pl.BlockSpec((WG // LANE, K3_TM, LANE), lambda i, gid: (0, i, 0))
    mspec = lambda k: pl.BlockSpec((1, NB, D), lambda i, gid: (gid[i // per], 0, k))
    cspec = lambda shape: pl.BlockSpec(shape, lambda i, gid: (0,) * len(shape))
    gs = pltpu.PrefetchScalarGridSpec(
        num_scalar_prefetch=1, grid=(R // K3_TM,),
        in_specs=[yspec, yspec, yspec, yspec, rspec(D), mspec(2), mspec(4), mspec(3),
                  cspec((4 * WG, D)), cspec((1, D)), cspec((1, D)), cspec((D, N_EXP)), cspec((1, N_EXP))],
        out_specs=[rspec(D), rspec(D), rspec(8), rspec(8)],
    )
    return pl.pallas_call(
        _outproj_kernel, grid_spec=gs,
        out_shape=[jax.ShapeDtypeStruct((R, D), F32), jax.ShapeDtypeStruct((R, D), F32),
                   jax.ShapeDtypeStruct((R, 8), jnp.int32), jax.ShapeDtypeStruct((R, 8), F32)],
        compiler_params=_cparams(("arbitrary",)),
    )(gid3, *ys, x, mod3, mod3, mod3, prm['w_out'], prm['ln1_g'], prm['ln1_b'], prm['router_w'], prm['router_b'])


def _moe_kernel(off_ref, tok_ref, gw_ref, h2_ref, wg_ref, wu_ref, wd_ref, bg_ref, bu_ref, bd_ref,
                o_ref, lhs_s, y_s):
    i = pl.program_id(0)
    e = pl.program_id(1)

    @pl.when(e == 0)
    def _():
        o_ref[...] = jnp.zeros_like(o_ref)

    @pl.when((e == 0) & (i == 0))
    def _():
        lhs_s[...] = jnp.zeros_like(lhs_s)

    base = off_ref[i * (N_EXP + 1) + e]
    end = off_ref[i * (N_EXP + 1) + e + 1]
    nblk = (end - base + MOE_RB - 1) // MOE_RB

    def blk(rb, _):
        s0 = base + rb * MOE_RB
        cnt = jnp.minimum(MOE_RB, end - s0)

        def gath(r, _):
            tok = tok_ref[0, 0, s0 + r]
            lhs_s[pl.ds(r, 1), :] = h2_ref[pl.ds(tok, 1), :]
            return 0

        lax.fori_loop(0, cnt, gath, 0)
        xb = lhs_s[...].astype(BF16)
        g = jnp.dot(xb, wg_ref[0], preferred_element_type=F32) + bg_ref[0]
        u = jnp.dot(xb, wu_ref[0], preferred_element_type=F32) + bu_ref[0]
        g = jnp.minimum(g, SWIGLU_LIMIT)
        u = jnp.clip(u, -SWIGLU_LIMIT, SWIGLU_LIMIT)
        hdn = (u + 1.0) * g * _sigmoid(SWIGLU_ALPHA * g)
        y_s[...] = jnp.dot(hdn.astype(BF16), wd_ref[0], preferred_element_type=F32) + bd_ref[0]

        def scat(r, _):
            tok = tok_ref[0, 0, s0 + r]
            w = gw_ref[0, 0, s0 + r]
            o_ref[pl.ds(tok, 1), :] = o_ref[pl.ds(tok, 1), :] + w * y_s[pl.ds(r, 1), :]
            return 0

        lax.fori_loop(0, cnt, scat, 0)
        return 0

    lax.fori_loop(0, nblk, blk, 0)


def _moe_call(off, srt_tok, srt_w, h2, prm, ts):
    R = h2.shape[0]
    nst = R // ts
    na = ts * TOP_K
    gs = pltpu.PrefetchScalarGridSpec(
        num_scalar_prefetch=1, grid=(nst, N_EXP),
        in_specs=[pl.BlockSpec((1, 1, na), lambda i, e, off: (i, 0, 0), memory_space=pltpu.SMEM),
                  pl.BlockSpec((1, 1, na), lambda i, e, off: (i, 0, 0), memory_space=pltpu.SMEM),
                  pl.BlockSpec((ts, D), lambda i, e, off: (i, 0)),
                  pl.BlockSpec((1, D, D_FF), lambda i, e, off: (e, 0, 0)),
                  pl.BlockSpec((1, D, D_FF), lambda i, e, off: (e, 0, 0)),
                  pl.BlockSpec((1, D_FF, D), lambda i, e, off: (e, 0, 0)),
                  pl.BlockSpec((1, 1, D_FF), lambda i, e, off: (e, 0, 0)),
                  pl.BlockSpec((1, 1, D_FF), lambda i, e, off: (e, 0, 0)),
                  pl.BlockSpec((1, 1, D), lambda i, e, off: (e, 0, 0))],
        out_specs=pl.BlockSpec((ts, D), lambda i, e, off: (i, 0)),
        scratch_shapes=[pltpu.VMEM((MOE_RB, D), F32), pltpu.VMEM((MOE_RB, D), F32)],
    )
    return pl.pallas_call(
        _moe_kernel, grid_spec=gs,
        out_shape=jax.ShapeDtypeStruct((R, D), F32),
        compiler_params=_cparams(("arbitrary", "arbitrary")),
    )(off, srt_tok, srt_w, h2, prm['moe_wg'], prm['moe_wu'], prm['moe_wd'], prm['moe_bg'], prm['moe_bu'],
      prm['moe_bd'])


def _moe_plan(topi, topw, ts):
    R = topi.shape[0]
    nst = R // ts
    na = ts * TOP_K
    e_flat = topi[:, :TOP_K].reshape(nst, na)
    w_flat = topw[:, :TOP_K].reshape(nst, na)
    key = e_flat * na + jnp.arange(na, dtype=jnp.int32)[None, :]
    key = jnp.sort(key, axis=1)
    a_sorted = key % na
    srt_tok = (a_sorted // TOP_K).astype(jnp.int32)
    srt_w = jnp.take_along_axis(w_flat, a_sorted, axis=1)
    counts = jnp.sum((key // na)[:, :, None] == jnp.arange(N_EXP, dtype=jnp.int32)[None, None, :], axis=1)
    off = jnp.concatenate([jnp.zeros((nst, 1), jnp.int32), jnp.cumsum(counts, axis=1).astype(jnp.int32)], axis=1)
    return off.reshape(-1), srt_tok.reshape(nst, 1, na), srt_w.reshape(nst, 1, na)


def _ln2_kernel(gid_ref, x1_ref, y_ref, g2_ref, lg_ref, lb_ref, o_ref):
    v = ALPHA_DN * x1_ref[...] + _per_slab(y_ref[...], g2_ref[0])
    o_ref[...] = _layer_norm(v, lg_ref[...], lb_ref[...])


def _ln2_call(gid, x1, y, mod3, prm):
    R = x1.shape[0]
    rspec = pl.BlockSpec((ROWS, D), lambda i, gid: (i, 0))
    cspec = pl.BlockSpec((1, D), lambda i, gid: (0, 0))
    gs = pltpu.PrefetchScalarGridSpec(
        num_scalar_prefetch=1, grid=(R // ROWS,),
        in_specs=[rspec, rspec, pl.BlockSpec((1, NB, D), lambda i, gid: (gid[i], 0, 5)), cspec, cspec],
        out_specs=rspec,
    )
    return pl.pallas_call(
        _ln2_kernel, grid_spec=gs, out_shape=jax.ShapeDtypeStruct((R, D), F32),
        compiler_params=_cparams(("arbitrary",)),
    )(gid, x1, y, mod3, prm['ln2_g'], prm['ln2_b'])


def _block_diag(blocks):
    n, r, c = blocks.shape
    eye = jnp.eye(n, dtype=blocks.dtype)
    return jnp.einsum('nrc,nm->nrmc', blocks, eye).reshape(n * r, n * c)


def _prep_layer(p, l):
    prm = {}
    w = p['w_in'][l]
    pieces = [w[:, 0:512], w[:, 512:1024], w[:, 1280:1792], w[:, 1808:2320], w[:, 2320:2832], w[:, 3344:3856],
              w[:, 3856:4368], w[:, 2832:3088], w[:, 3088:3344], w[:, 1024:1152], w[:, 1152:1280],
              w[:, 1792:1808], w[:, 4368:4384], jnp.zeros((D, 96), w.dtype)]
    prm['w_in'] = jnp.concatenate(pieces, axis=1).astype(BF16)
    b_re, b_im = p['s5_b_re'][l], p['s5_b_im'][l]
    bt = []
    for j in range(4):
        sl = slice(8 * j, 8 * j + 8)
        bre = _block_diag(jnp.swapaxes(b_re[sl], 1, 2))
        bim = _block_diag(jnp.swapaxes(b_im[sl], 1, 2))
        bt.append(jnp.concatenate([bre, bim], axis=1))
    prm['s5_bt'] = jnp.stack(bt).astype(BF16)
    lam = lax.complex(p['s5_lam_re'][l], p['s5_lam_im'][l])
    dt = jnp.exp(p['s5_log_dt'][l])[:, :, None]
    lam_bar = jnp.exp(lam * dt)
    f = (lam_bar - 1.0) / lam
    prm['s5_lr'] = jnp.real(lam_bar).reshape(2, 4, 1, WG)
    prm['s5_li'] = jnp.imag(lam_bar).reshape(2, 4, 1, WG)
    c_c = lax.complex(p['s5_c_re'][l], p['s5_c_im'][l])
    cms = []
    for d in range(2):
        e = c_c * f[d][:, None, :]
        er = jnp.swapaxes(jnp.real(e), 1, 2)
        ei = jnp.swapaxes(jnp.imag(e), 1, 2)
        cm = []
        for j in range(4):
            sl = slice(8 * j, 8 * j + 8)
            cm.append(jnp.concatenate([_block_diag(er[sl]), -_block_diag(ei[sl])], axis=0))
        cms.append(jnp.stack(cm))
    prm['s5_cm'] = jnp.stack(cms).astype(BF16)
    prm['s5_d'] = p['s5_d'][l].reshape(1, WG)
    prm['s5_glu_w'] = p['s5_glu_w'][l].astype(BF16)
    prm['s5_glu_b'] = p['s5_glu_b'][l].reshape(1, WG)
    prm['ssd_conv_w'] = p['ssd_conv_w'][l]
    prm['ssd_conv_b'] = p['ssd_conv_b'][l].reshape(1, -1)
    pad = lambda v: jnp.concatenate([v.reshape(-1), jnp.zeros((128 - v.size,), F32)]).reshape(1, 128)
    prm['ssd_dtb'] = pad(p['ssd_dt_bias'][l])
    prm['ssd_arow'] = pad(-jnp.exp(p['ssd_a_log'][l]))
    prm['ssd_drow'] = jnp.repeat(p['ssd_d'][l], SSD_P).reshape(1, WG)
    prm['ssd_norm_g'] = p['ssd_norm_g'][l].reshape(1, WG)
    prm['lru_conv_w'] = p['lru_conv_w'][l]
    prm['lru_conv_b'] = p['lru_conv_b'][l].reshape(1, WG)
    prm['lru_wa'] = jnp.stack([_block_diag(p['lru_wa'][l][d]) for d in range(2)]).astype(BF16)
    prm['lru_wx'] = jnp.stack([_block_diag(p['lru_wx'][l][d]) for d in range(2)]).astype(BF16)
    prm['lru_ba'] = p['lru_ba'][l].reshape(2, 1, WG)
    prm['lru_bx'] = p['lru_bx'][l].reshape(2, 1, WG)
    prm['lru_sp'] = jax.nn.softplus(-p['lru_lam'][l]).reshape(2, 1, WG)
    prm['ml_gb'] = jnp.concatenate([jnp.zeros((SM_GATE,), F32), p['ml_gate_b'][l].reshape(-1),
                                    jnp.zeros((128 - SM_GATE - 4 * ML_H,), F32)]).reshape(1, 128)
    prm['ml_norm_g'] = p['ml_norm_g'][l].reshape(1, WG)
    prm['w_out'] = p['w_out'][l].astype(BF16)
    for k in ('ln1_g', 'ln1_b', 'ln2_g', 'ln2_b'):
        prm[k] = p[k][l].reshape(1, D)
    prm['router_w'] = p['router_w'][l]
    prm['router_b'] = p['router_b'][l].reshape(1, N_EXP)
    wgu = p['moe_w_gate_up'][l]
    prm['moe_wg'] = wgu[:, :, 0::2].astype(BF16)
    prm['moe_wu'] = wgu[:, :, 1::2].astype(BF16)
    bgu = p['moe_b_gate_up'][l]
    prm['moe_bg'] = bgu[:, 0::2].reshape(N_EXP, 1, D_FF)
    prm['moe_bu'] = bgu[:, 1::2].reshape(N_EXP, 1, D_FF)
    prm['moe_wd'] = p['moe_w_down'][l].astype(BF16)
    prm['moe_bd'] = p['moe_b_down'][l].reshape(N_EXP, 1, D)
    prm['ada_w'] = p['ada_w'][l].astype(BF16)
    prm['ada_b'] = p['ada_b'][l].reshape(1, 6 * D)
    return prm


def _mixers(flags, proj, prm):
    ya = _s5_call(flags, proj, None, prm, 0, reverse=False, final=False)
    ya = _s5_call(flags, proj, ya, prm, 1, reverse=True, final=True)
    yb = _ssd_call(flags, proj, None, prm, reverse=False, final=False)
    yb = _ssd_call(flags, proj, yb, prm, reverse=True, final=True)
    yc = _lru_call(flags, proj, None, prm, 0, reverse=False, final=False)
    yc = _lru_call(flags, proj, yc, prm, 1, reverse=True, final=True)
    yd = _mlstm_call(flags, proj, None, prm, reverse=False, final=False)
    yd = _mlstm_call(flags, proj, yd, prm, reverse=True, final=True)
    return ya, yb, yc, yd


def _to_rows(x):
    b, L, _ = x.shape
    return x.reshape(b // NB, NB, L, D).transpose(0, 2, 1, 3).reshape(b * L, D)


def _from_rows(r, b, L):
    return r.reshape(b // NB, L, NB, D).transpose(0, 2, 1, 3).reshape(b, L, D)


def _trunk(xs, cs, p, depth, moe_ts):
    first, last, gid = [], [], []
    g = 0
    for x in xs:
        b, L, _ = x.shape
        assert b % NB == 0 and L % CH == 0
        n = L // CH
        for _ in range(b // NB):
            first += [1] + [0] * (n - 1)
            last += [0] * (n - 1) + [1]
            gid += [g] * n
            g += 1
    first = jnp.asarray(first, jnp.int32)
    last = jnp.asarray(last, jnp.int32)
    gid = jnp.asarray(gid, jnp.int32)
    flags = (first, last)
    x = jnp.concatenate([_to_rows(x.astype(F32)) for x in xs], axis=0)
    c_all = jnp.concatenate([c.astype(F32) for c in cs], axis=0)
    R = x.shape[0]
    ts = min(moe_ts, R)
    assert R % ts == 0
    for l in range(depth):
        prm = _prep_layer(p, l)
        mod3 = _mod_call(c_all, prm['ada_w'], prm['ada_b']).reshape(g, NB, 6 * D)
        proj = _inproj_call(gid, x, mod3, prm['w_in'])
        ys = _mixers(flags, proj, prm)
        x1, h2, topi, topw = _outproj_call(gid, ys, x, mod3, prm)
        off, srt_tok, srt_w = _moe_plan(topi, topw, ts)
        y = _moe_call(off, srt_tok, srt_w, h2, prm, ts)
        x = _ln2_call(gid, x1, y, mod3, prm)
    outs = []
    r0 = 0
    for xin in xs:
        b, L, _ = xin.shape
        outs.append(_from_rows(x[r0:r0 + b * L], b, L).astype(xin.dtype))
        r0 += b * L
    return outs


def kernel(x_prompt, x_sample, c_prompt, c_sample, ada_w, ada_b, w_in, s5_lam_re, s5_lam_im, s5_log_dt, s5_b_re, s5_b_im, s5_c_re, s5_c_im, s5_d, s5_glu_w, s5_glu_b, ssd_conv_w, ssd_conv_b, ssd_a_log, ssd_dt_bias, ssd_d, ssd_norm_g, lru_conv_w, lru_conv_b, lru_wa, lru_ba, lru_wx, lru_bx, lru_lam, ml_gate_b, ml_norm_g, w_out, ln1_g, ln1_b, router_w, router_b, moe_w_gate_up, moe_b_gate_up, moe_w_down, moe_b_down, ln2_g, ln2_b):
    p = {
        'ada_w': ada_w, 'ada_b': ada_b, 'w_in': w_in,
        's5_lam_re': s5_lam_re, 's5_lam_im': s5_lam_im, 's5_log_dt': s5_log_dt,
        's5_b_re': s5_b_re, 's5_b_im': s5_b_im, 's5_c_re': s5_c_re, 's5_c_im': s5_c_im,
        's5_d': s5_d, 's5_glu_w': s5_glu_w, 's5_glu_b': s5_glu_b,
        'ssd_conv_w': ssd_conv_w, 'ssd_conv_b': ssd_conv_b, 'ssd_a_log': ssd_a_log,
        'ssd_dt_bias': ssd_dt_bias, 'ssd_d': ssd_d, 'ssd_norm_g': ssd_norm_g,
        'lru_conv_w': lru_conv_w, 'lru_conv_b': lru_conv_b, 'lru_wa': lru_wa, 'lru_ba': lru_ba,
        'lru_wx': lru_wx, 'lru_bx': lru_bx, 'lru_lam': lru_lam,
        'ml_gate_b': ml_gate_b, 'ml_norm_g': ml_norm_g, 'w_out': w_out,
        'ln1_g': ln1_g, 'ln1_b': ln1_b, 'router_w': router_w, 'router_b': router_b,
        'moe_w_gate_up': moe_w_gate_up, 'moe_b_gate_up': moe_b_gate_up,
        'moe_w_down': moe_w_down, 'moe_b_down': moe_b_down, 'ln2_g': ln2_g, 'ln2_b': ln2_b,
    }
    y_prompt, y_sample = _trunk([x_prompt, x_sample], [c_prompt, c_sample], p, DEPTH, 1024)
    return (y_prompt, y_sample)
```

```python
import functools
import math

import numpy as np
import jax
import jax.numpy as jnp
from jax import lax
from jax.experimental import pallas as pl
from jax.experimental.pallas import tpu as pltpu

F32 = jnp.float32
BF16 = jnp.bfloat16
HIGHEST = lax.Precision.HIGHEST

D = 1024
DEPTH = 4
WG = 512
S5_G, S5_CH, S5_N = 32, 16, 64
SSD_H, SSD_P, SSD_NG, SSD_N = 8, 64, 2, 64
LRU_NB, LRU_BD, LRU_C = 8, 64, 8.0
ML_H, ML_DQK, ML_DV = 4, 64, 128
N_EXP, TOP_K, D_FF = 32, 4, 1024
SWIGLU_LIMIT, SWIGLU_ALPHA = 7.0, 1.702
ALPHA_DN = (2.0 * DEPTH) ** 0.25
LN_EPS, RMS_EPS = 1e-5, 1e-6

LANE = 128
NB = 8
CH = 128
ROWS = CH * NB
PW = 4480
K1_TN = 640
K3_TM = 512
MOE_RB = 128
MOE_UNROLL = 8
MOE_TS = 4096
VMEM_LIMIT = 56 * 1024 * 1024

COL_U, COL_XS, COL_Z, COL_XL, COL_GL, COL_V, COL_O = 0, 1, 2, 3, 4, 5, 6
COL_Q, COL_K, COL_BC = 14, 15, 16
COL_SM = 34
SM_DT, SM_GATE = 0, 16


def _sigmoid(x):
    return 1.0 / (1.0 + jnp.exp(-x))


def _silu(x):
    return x * _sigmoid(x)


def _softplus(x):
    return jnp.maximum(x, 0.0) + jnp.log(1.0 + jnp.exp(-jnp.abs(x)))


def _gelu_tanh(x):
    return 0.5 * x * (1.0 + jnp.tanh(math.sqrt(2.0 / math.pi) * (x + 0.044715 * (x * x * x))))


def _layer_norm(v, g, b):
    mu = jnp.mean(v, axis=-1, keepdims=True)
    vc = v - mu
    var = jnp.mean(vc * vc, axis=-1, keepdims=True)
    return vc * lax.rsqrt(var + LN_EPS) * g + b


def _bdot(a, b):
    return jnp.dot(a.astype(BF16), b.astype(BF16), preferred_element_type=F32)


def _per_slab(x, m):
    n = x.shape[0] // NB
    return (x.reshape(n, NB, x.shape[1]) * m[None]).reshape(x.shape)


def _per_slab_add(x, m):
    n = x.shape[0] // NB
    return (x.reshape(n, NB, x.shape[1]) + m[None]).reshape(x.shape)


def _cparams(sem):
    return pltpu.CompilerParams(dimension_semantics=sem, vmem_limit_bytes=VMEM_LIMIT)


def _mod_kernel(c_ref, w_ref, b_ref, o_ref):
    o_ref[...] = _bdot(_silu(c_ref[...]), w_ref[...]) + b_ref[...]


def _mod_call(c_all, w, b):
    n = c_all.shape[0]
    return pl.pallas_call(
        _mod_kernel,
        grid=(6,),
        in_specs=[pl.BlockSpec((n, D), lambda j: (0, 0)),
                  pl.BlockSpec((D, D), lambda j: (0, j)),
                  pl.BlockSpec((1, D), lambda j: (0, j))],
        out_specs=pl.BlockSpec((n, D), lambda j: (0, j)),
        out_shape=jax.ShapeDtypeStruct((n, 6 * D), F32),
        compiler_params=_cparams(("arbitrary",)),
    )(c_all, w, b)


def _inproj_kernel(gid_ref, x_ref, sc_ref, sh_ref, w_ref, o_ref, h_s):
    @pl.when(pl.program_id(1) == 0)
    def _():
        h = _per_slab_add(_per_slab(x_ref[...], 1.0 + sc_ref[0]), sh_ref[0])
        h_s[...] = h.astype(BF16)

    res = jnp.dot(h_s[...], w_ref[...], preferred_element_type=F32)
    for k in range(K1_TN // LANE):
        o_ref[k] = res[:, LANE * k:LANE * (k + 1)]


def _inproj_call(gid, x, mod3, w_in_p):
    R = x.shape[0]
    nch = R // ROWS
    gs = pltpu.PrefetchScalarGridSpec(
        num_scalar_prefetch=1,
        grid=(nch, PW // K1_TN),
        in_specs=[pl.BlockSpec((ROWS, D), lambda i, j, gid: (i, 0)),
                  pl.BlockSpec((1, NB, D), lambda i, j, gid: (gid[i], 0, 1)),
                  pl.BlockSpec((1, NB, D), lambda i, j, gid: (gid[i], 0, 0)),
                  pl.BlockSpec((D, K1_TN), lambda i, j, gid: (0, j))],
        out_specs=pl.BlockSpec((K1_TN // LANE, ROWS, LANE), lambda i, j, gid: (j, i, 0)),
        scratch_shapes=[pltpu.VMEM((ROWS, D), BF16)],
    )
    return pl.pallas_call(
        _inproj_kernel, grid_spec=gs,
        out_shape=jax.ShapeDtypeStruct((PW // LANE, R, LANE), F32),
        compiler_params=_cparams(("arbitrary", "arbitrary")),
    )(gid, x, mod3, mod3, w_in_p)


def _chunk_idx(nch, reverse):
    return (lambda c: nch - 1 - c) if reverse else (lambda c: c)


def _main_spec(width, col, nch, reverse):
    ci = _chunk_idx(nch, reverse)
    return pl.BlockSpec((width // LANE, ROWS, LANE), lambda c, *_: (col, ci(c), 0))


def _prev_spec(width, col, nch, reverse):
    ci = _chunk_idx(nch, reverse)
    per = ROWS // (2 * NB)
    return pl.BlockSpec((width // LANE, 2 * NB, LANE), lambda c, *_: (col, jnp.maximum(ci(c) * per - 1, 0), 0))


def _next_spec(width, col, nch, reverse):
    ci = _chunk_idx(nch, reverse)
    per = ROWS // NB
    return pl.BlockSpec((width // LANE, NB, LANE),
                        lambda c, *_: (col, jnp.minimum((ci(c) + 1) * per, nch * per - 1), 0))


def _const_spec(shape):
    nd = len(shape)
    return pl.BlockSpec(shape, lambda c, *_: (0,) * nd)


def _cat(ref):
    return jnp.concatenate([ref[k] for k in range(ref.shape[0])], axis=1)


def _put(ref, val):
    for k in range(ref.shape[0]):
        ref[k] = val[:, LANE * k:LANE * (k + 1)]


def _fill_ext(ext_s, k0, x_ref, xp_ref, xn_ref, is_first, is_last):
    for k in range(x_ref.shape[0]):
        ext_s[k0 + k, 0:2 * NB, :] = jnp.where(is_first, 0.0, xp_ref[k])
        ext_s[k0 + k, 2 * NB:2 * NB + ROWS, :] = x_ref[k]
        ext_s[k0 + k, 2 * NB + ROWS:3 * NB + ROWS, :] = jnp.where(is_last, 0.0, xn_ref[k])


def _dwconv(ext_s, k, w_ref, b_ref):
    sl = slice(LANE * k, LANE * (k + 1))
    out = b_ref[:, sl] + w_ref[0:1, sl] * ext_s[k, 0:ROWS, :]
    for j in range(1, 4):
        out = out + w_ref[j:j + 1, sl] * ext_s[k, j * NB:j * NB + ROWS, :]
    return out


def _tri(reverse):
    r = lax.broadcasted_iota(jnp.int32, (CH, CH), 0)
    c = lax.broadcasted_iota(jnp.int32, (CH, CH), 1)
    return (c >= r) if reverse else (c <= r)


def _s5_kernel(*refs, nch, reverse, final):
    first_ref, last_ref = refs[0], refs[1]
    if final:
        (u_ref, yp_ref, bt_ref, lr_ref, li_ref, cm_ref, d_ref, gw_ref, gb_ref,
         o_ref, bu_s, st_s, carry_s) = refs[2:]
    else:
        u_ref, bt_ref, lr_ref, li_ref, cm_ref, o_ref, bu_s, st_s, carry_s = refs[2:]
    c = pl.program_id(0)
    cc = nch - 1 - c if reverse else c
    start = last_ref[cc] if reverse else first_ref[cc]

    @pl.when(start == 1)
    def _():
        carry_s[...] = jnp.zeros_like(carry_s)

    half = WG
    for j in range(4):
        bu_s[...] = jnp.dot(u_ref[j].astype(BF16), bt_ref[j], preferred_element_type=F32)
        lr = jnp.broadcast_to(lr_ref[j], (NB, half))
        li = jnp.broadcast_to(li_ref[j], (NB, half))

        def step(i, carry, lr=lr, li=li):
            sr, si = carry
            t = CH - 1 - i if reverse else i
            r0 = pl.multiple_of(t * NB, NB)
            br = bu_s[pl.ds(r0, NB), 0:half]
            bi = bu_s[pl.ds(r0, NB), half:2 * half]
            nr = lr * sr - li * si + br
            ni = lr * si + li * sr + bi
            st_s[pl.ds(r0, NB), 0:half] = nr
            st_s[pl.ds(r0, NB), half:2 * half] = ni
            return nr, ni

        lax.fori_loop(0, CH, step, (carry_s[j, :, 0:half], carry_s[j, :, half:2 * half]), unroll=8)
        e0 = (CH - 1) * NB if reverse else 0
        carry_s[j] = st_s[e0:e0 + NB, :]
        o_ref[j] = jnp.dot(st_s[...].astype(BF16), cm_ref[j], preferred_element_type=F32)

    if final:
        y = _cat(o_ref) + _cat(yp_ref) + d_ref[...] * _cat(u_ref)
        g = _gelu_tanh(y)
        _put(o_ref, g * _sigmoid(_bdot(g, gw_ref[...]) + gb_ref[...]))


def _s5_call(flags, proj, yprev, prm, d, *, reverse, final):
    first, last = flags
    R = proj.shape[1]
    nch = R // ROWS
    in_specs = [_main_spec(WG, COL_U, nch, reverse)]
    args = [proj]
    if final:
        in_specs.append(_main_spec(WG, 0, nch, reverse))
        args.append(yprev)
    in_specs += [_const_spec((4, 128, 2 * WG)), _const_spec((4, 1, WG)), _const_spec((4, 1, WG)),
                 _const_spec((4, 2 * WG, 128))]
    args += [prm['s5_bt'], prm['s5_lr'][d], prm['s5_li'][d], prm['s5_cm'][d]]
    if final:
        in_specs += [_const_spec((1, WG)), _const_spec((WG, WG)), _const_spec((1, WG))]
        args += [prm['s5_d'], prm['s5_glu_w'], prm['s5_glu_b']]
    gs = pltpu.PrefetchScalarGridSpec(
        num_scalar_prefetch=2, grid=(nch,), in_specs=in_specs,
        out_specs=_main_spec(WG, 0, nch, reverse),
        scratch_shapes=[pltpu.VMEM((ROWS, 2 * WG), F32), pltpu.VMEM((ROWS, 2 * WG), F32),
                        pltpu.VMEM((4, NB, 2 * WG), F32)],
    )
    return pl.pallas_call(
        functools.partial(_s5_kernel, nch=nch, reverse=reverse, final=final), grid_spec=gs,
        out_shape=jax.ShapeDtypeStruct((WG // LANE, R, LANE), F32),
        compiler_params=_cparams(("arbitrary",)),
    )(first, last, *args)


def _lru_kernel(*refs, nch, reverse, final):
    first_ref, last_ref = refs[0], refs[1]
    if final:
        (x_ref, xp_ref, xn_ref, gate_ref, hp_ref, cw_ref, cb_ref, wa_ref, ba_ref, wx_ref, bx_ref, sp_ref,
         o_ref, ext_s, a_s, inp_s, carry_s) = refs[2:]
    else:
        (x_ref, xp_ref, xn_ref, cw_ref, cb_ref, wa_ref, ba_ref, wx_ref, bx_ref, sp_ref,
         o_ref, ext_s, a_s, inp_s, carry_s) = refs[2:]
    c = pl.program_id(0)
    cc = nch - 1 - c if reverse else c
    is_first = first_ref[cc] == 1
    is_last = last_ref[cc] == 1
    start = is_last if reverse else is_first

    @pl.when(start)
    def _():
        carry_s[...] = jnp.zeros_like(carry_s)

    _fill_ext(ext_s, 0, x_ref, xp_ref, xn_ref, is_first, is_last)
    xc = jnp.concatenate([_dwconv(ext_s, k, cw_ref, cb_ref) for k in range(WG // LANE)], axis=1)
    xb = xc.astype(BF16)
    r = _sigmoid(jnp.dot(xb, wa_ref[...], preferred_element_type=F32) + ba_ref[...])
    i = _sigmoid(jnp.dot(xb, wx_ref[...], preferred_element_type=F32) + bx_ref[...])
    log_a = -LRU_C * r * sp_ref[...]
    a_s[...] = jnp.exp(log_a)
    inp_s[...] = jnp.sqrt(1.0 - jnp.exp(2.0 * log_a)) * (i * xc)

    def step(k, h):
        t = CH - 1 - k if reverse else k
        r0 = pl.multiple_of(t * NB, NB)
        h = a_s[pl.ds(r0, NB), :] * h + inp_s[pl.ds(r0, NB), :]
        for k in range(WG // LANE):
            o_ref[k, pl.ds(r0, NB), :] = h[:, LANE * k:LANE * (k + 1)]
        return h

    carry_s[...] = lax.fori_loop(0, CH, step, carry_s[...], unroll=8)
    if final:
        for k in range(WG // LANE):
            o_ref[k] = (o_ref[k] + hp_ref[k]) * _gelu_tanh(gate_ref[k])


def _lru_call(flags, proj, hprev, prm, d, *, reverse, final):
    first, last = flags
    R = proj.shape[1]
    nch = R // ROWS
    in_specs = [_main_spec(WG, COL_XL, nch, reverse), _prev_spec(WG, COL_XL, nch, reverse),
                _next_spec(WG, COL_XL, nch, reverse)]
    args = [proj, proj, proj]
    if final:
        in_specs += [_main_spec(WG, COL_GL, nch, reverse), _main_spec(WG, 0, nch, reverse)]
        args += [proj, hprev]
    in_specs += [_const_spec((4, WG)), _const_spec((1, WG)), _const_spec((WG, WG)), _const_spec((1, WG)),
                 _const_spec((WG, WG)), _const_spec((1, WG)), _const_spec((1, WG))]
    args += [prm['lru_conv_w'], prm['lru_conv_b'], prm['lru_wa'][d], prm['lru_ba'][d], prm['lru_wx'][d],
             prm['lru_bx'][d], prm['lru_sp'][d]]
    gs = pltpu.PrefetchScalarGridSpec(
        num_scalar_prefetch=2, grid=(nch,), in_specs=in_specs,
        out_specs=_main_spec(WG, 0, nch, reverse),
        scratch_shapes=[pltpu.VMEM((WG // LANE, ROWS + 3 * NB, LANE), F32), pltpu.VMEM((ROWS, WG), F32),
                        pltpu.VMEM((ROWS, WG), F32), pltpu.VMEM((NB, WG), F32)],
    )
    return pl.pallas_call(
        functools.partial(_lru_kernel, nch=nch, reverse=reverse, final=final), grid_spec=gs,
        out_shape=jax.ShapeDtypeStruct((WG // LANE, R, LANE), F32),
        compiler_params=_cparams(("arbitrary",)),
    )(first, last, *args)


def _ssd_kernel(*refs, nch, reverse, final):
    first_ref, last_ref = refs[0], refs[1]
    if final:
        (x_ref, xp_ref, xn_ref, bc_ref, bcp_ref, bcn_ref, sm_ref, z_ref, yp_ref,
         cw_ref, cb_ref, dtb_ref, arow_ref, drow_ref, ng_ref,
         o_ref, ext_s, xd_s, smd_s, yd_s, st_s) = refs[2:]
    else:
        (x_ref, xp_ref, xn_ref, bc_ref, bcp_ref, bcn_ref, sm_ref,
         cw_ref, cb_ref, dtb_ref, arow_ref,
         o_ref, ext_s, xd_s, smd_s, yd_s, st_s) = refs[2:]
    c = pl.program_id(0)
    cc = nch - 1 - c if reverse else c
    is_first = first_ref[cc] == 1
    is_last = last_ref[cc] == 1
    start = is_last if reverse else is_first

    @pl.when(start)
    def _():
        st_s[...] = jnp.zeros_like(st_s)

    CW = WG + 2 * SSD_NG * SSD_N
    _fill_ext(ext_s, 0, x_ref, xp_ref, xn_ref, is_first, is_last)
    _fill_ext(ext_s, WG // LANE, bc_ref, bcp_ref, bcn_ref, is_first, is_last)
    for k in range(CW // LANE):
        ext_s[k, 0:ROWS, :] = _silu(_dwconv(ext_s, k, cw_ref, cb_ref))
    for b in range(NB):
        for k in range(CW // LANE):
            xd_s[b, :, LANE * k:LANE * (k + 1)] = ext_s[k, pl.ds(b, CH, stride=NB), :]
        smd_s[b] = sm_ref[0, pl.ds(b, CH, stride=NB), :]

    mask = _tri(reverse)
    tri = jnp.where(mask, 1.0, 0.0)
    doff = SM_DT + (SSD_H if reverse else 0)
    edge = 0 if reverse else CH - 1
    NS = SSD_N

    def per_b(b, _):
        xb = xd_s[b]
        xs = xb[:, 0:WG]
        bm = xb[:, WG:WG + SSD_NG * NS]
        cm = xb[:, WG + SSD_NG * NS:CW]
        dtf = _softplus(smd_s[b] + dtb_ref[...])
        af = dtf * arow_ref[...]
        cs = jnp.dot(tri, af, precision=HIGHEST, preferred_element_type=F32)
        cs_t = cs.T
        dt_t = dtf.T
        bm_t = bm.T
        gmat = [_bdot(cm[:, NS * g:NS * (g + 1)], bm_t[NS * g:NS * (g + 1), :]) for g in range(SSD_NG)]
        for h in range(SSD_H):
            ci = doff + h
            g = h // (SSD_H // SSD_NG)
            col = cs[:, ci:ci + 1]
            row = cs_t[ci:ci + 1, :]
            lmat = jnp.where(mask, jnp.exp(jnp.minimum(col - row, 0.0)), 0.0)
            m = gmat[g] * lmat * dt_t[ci:ci + 1, :]
            xh = xs[:, SSD_P * h:SSD_P * (h + 1)]
            s_prev = st_s[b, h]
            y = _bdot(m, xh) + jnp.exp(col) * _bdot(cm[:, NS * g:NS * (g + 1)], s_prev)
            tot = cs[edge:edge + 1, ci:ci + 1]
            wcol = jnp.exp(tot - col) * dtf[:, ci:ci + 1]
            st_s[b, h] = jnp.exp(tot) * s_prev + _bdot(bm_t[NS * g:NS * (g + 1), :], xh * wcol)
            yd_s[b, :, SSD_P * h:SSD_P * (h + 1)] = y
        return 0

    lax.fori_loop(0, NB, per_b, 0)
    for b in range(NB):
        for k in range(WG // LANE):
            o_ref[k, pl.ds(b, CH, stride=NB), :] = yd_s[b, :, LANE * k:LANE * (k + 1)]
    if final:
        xs_all = jnp.concatenate([ext_s[k, 0:ROWS, :] for k in range(WG // LANE)], axis=1)
        y = (_cat(o_ref) + _cat(yp_ref) + drow_ref[...] * xs_all) * _silu(_cat(z_ref))
        ms = jnp.mean(y * y, axis=-1, keepdims=True)
        _put(o_ref, y * lax.rsqrt(ms + RMS_EPS) * ng_ref[...])


def _ssd_call(flags, proj, yprev, prm, *, reverse, final):
    first, last = flags
    R = proj.shape[1]
    nch = R // ROWS
    CW = WG + 2 * SSD_NG * SSD_N
    in_specs = [_main_spec(WG, COL_XS, nch, reverse), _prev_spec(WG, COL_XS, nch, reverse),
                _next_spec(WG, COL_XS, nch, reverse),
                _main_spec(256, COL_BC, nch, reverse), _prev_spec(256, COL_BC, nch, reverse),
                _next_spec(256, COL_BC, nch, reverse),
                _main_spec(128, COL_SM, nch, reverse)]
    args = [proj] * 7
    if final:
        in_specs += [_main_spec(WG, COL_Z, nch, reverse), _main_spec(WG, 0, nch, reverse)]
        args += [proj, yprev]
    in_specs += [_const_spec((4, CW)), _const_spec((1, CW)), _const_spec((1, 128)), _const_spec((1, 128))]
    args += [prm['ssd_conv_w'], prm['ssd_conv_b'], prm['ssd_dtb'], prm['ssd_arow']]
    if final:
        in_specs += [_const_spec((1, WG)), _const_spec((1, WG))]
        args += [prm['ssd_drow'], prm['ssd_norm_g']]
    gs = pltpu.PrefetchScalarGridSpec(
        num_scalar_prefetch=2, grid=(nch,), in_specs=in_specs,
        out_specs=_main_spec(WG, 0, nch, reverse),
        scratch_shapes=[pltpu.VMEM((CW // LANE, ROWS + 3 * NB, LANE), F32), pltpu.VMEM((NB, CH, CW), F32),
                        pltpu.VMEM((NB, CH, 128), F32), pltpu.VMEM((NB, CH, WG), F32),
                        pltpu.VMEM((NB, SSD_H, SSD_N, SSD_P), F32)],
    )
    return pl.pallas_call(
        functools.partial(_ssd_kernel, nch=nch, reverse=reverse, final=final), grid_spec=gs,
        out_shape=jax.ShapeDtypeStruct((WG // LANE, R, LANE), F32),
        compiler_params=_cparams(("arbitrary",)),
    )(first, last, *args)


NEG = -1e30


def _mlstm_kernel(*refs, nch, reverse, final):
    first_ref, last_ref = refs[0], refs[1]
    if final:
        (q_ref, k_ref, v_ref, sm_ref, og_ref, hp_ref, gb_ref, ng_ref,
         o_ref, qd_s, kd_s, vd_s, smd_s, hd_s, c_s, n_s, m_s) = refs[2:]
    else:
        (q_ref, k_ref, v_ref, sm_ref, gb_ref,
         o_ref, qd_s, kd_s, vd_s, smd_s, hd_s, c_s, n_s, m_s) = refs[2:]
    c = pl.program_id(0)
    cc = nch - 1 - c if reverse else c
    start = (last_ref[cc] if reverse else first_ref[cc]) == 1

    @pl.when(start)
    def _():
        c_s[...] = jnp.zeros_like(c_s)
        n_s[...] = jnp.zeros_like(n_s)
        m_s[...] = jnp.zeros_like(m_s)

    for b in range(NB):
        for src, dst in ((q_ref, qd_s), (k_ref, kd_s), (v_ref, vd_s)):
            for k in range(src.shape[0]):
                dst[b, :, LANE * k:LANE * (k + 1)] = src[k, pl.ds(b, CH, stride=NB), :]
        smd_s[b] = sm_ref[0, pl.ds(b, CH, stride=NB), :]

    mask = _tri(reverse)
    tri = jnp.where(mask, 1.0, 0.0)
    goff = SM_GATE + (2 * ML_H if reverse else 0)
    edge = 0 if reverse else CH - 1

    def per_b(b, _):
        sm = smd_s[b] + gb_ref[...]
        logf = -_softplus(-sm)
        fc_all = jnp.dot(tri, logf, precision=HIGHEST, preferred_element_type=F32)
        fr_all = fc_all.T
        sm_t = sm.T
        qb = qd_s[b] * (ML_DQK ** -0.5)
        kb = kd_s[b]
        kb_t = kb.T
        vb = vd_s[b]
        for h in range(ML_H):
            ci = goff + h
            cf = goff + ML_H + h
            fc = fc_all[:, cf:cf + 1]
            fr = fr_all[cf:cf + 1, :]
            ir = sm_t[ci:ci + 1, :]
            ic = sm[:, ci:ci + 1]
            m_prev = m_s[b, h]
            dlog = jnp.where(mask, fc - fr + ir, NEG)
            inter = fc + m_prev
            m_t = jnp.maximum(jnp.max(dlog, axis=1, keepdims=True), inter)
            wts = jnp.exp(dlog - m_t)
            w_inter = jnp.exp(inter - m_t)
            qh = qb[:, ML_DQK * h:ML_DQK * (h + 1)]
            kh = kb[:, ML_DQK * h:ML_DQK * (h + 1)]
            kh_t = kb_t[ML_DQK * h:ML_DQK * (h + 1), :]
            vh = vb[:, ML_DV * h:ML_DV * (h + 1)]
            ct_prev = c_s[b, h]
            n_prev = n_s[b, h]
            s = _bdot(qh, kh_t) * wts
            num = _bdot(s, vh) + w_inter * _bdot(qh, ct_prev)
            den = jnp.sum(s, axis=1, keepdims=True) + w_inter * jnp.sum(qh * n_prev, axis=1, keepdims=True)
            hd_s[b, :, ML_DV * h:ML_DV * (h + 1)] = num / jnp.maximum(jnp.abs(den), jnp.exp(-m_t))
            f_end = fc_all[edge:edge + 1, cf:cf + 1]
            w_log = f_end - fc + ic
            m_new = jnp.maximum(f_end + m_prev, jnp.max(w_log, axis=0, keepdims=True))
            w_s = jnp.exp(w_log - m_new)
            scale = jnp.exp(f_end + m_prev - m_new)
            c_s[b, h] = scale * ct_prev + _bdot(kh_t, vh * w_s)
            n_s[b, h] = scale * n_prev + jnp.sum(kh * w_s, axis=0, keepdims=True)
            m_s[b, h] = m_new
        return 0

    lax.fori_loop(0, NB, per_b, 0)
    for b in range(NB):
        for k in range(WG // LANE):
            o_ref[k, pl.ds(b, CH, stride=NB), :] = hd_s[b, :, LANE * k:LANE * (k + 1)]
    if final:
        assert ML_DV == LANE
        for h in range(ML_H):
            hh = o_ref[h] + hp_ref[h]
            ms = jnp.mean(hh * hh, axis=-1, keepdims=True)
            o_ref[h] = hh * lax.rsqrt(ms + RMS_EPS) * ng_ref[:, LANE * h:LANE * (h + 1)] * _sigmoid(og_ref[h])


def _mlstm_call(flags, proj, hprev, prm, *, reverse, final):
    first, last = flags
    R = proj.shape[1]
    nch = R // ROWS
    in_specs = [_main_spec(256, COL_Q, nch, reverse), _main_spec(256, COL_K, nch, reverse),
                _main_spec(WG, COL_V, nch, reverse), _main_spec(128, COL_SM, nch, reverse)]
    args = [proj] * 4
    if final:
        in_specs += [_main_spec(WG, COL_O, nch, reverse), _main_spec(WG, 0, nch, reverse)]
        args += [proj, hprev]
    in_specs += [_const_spec((1, 128))]
    args += [prm['ml_gb']]
    if final:
        in_specs += [_const_spec((1, WG))]
        args += [prm['ml_norm_g']]
    gs = pltpu.PrefetchScalarGridSpec(
        num_scalar_prefetch=2, grid=(nch,), in_specs=in_specs,
        out_specs=_main_spec(WG, 0, nch, reverse),
        scratch_shapes=[pltpu.VMEM((NB, CH, 256), F32), pltpu.VMEM((NB, CH, 256), F32),
                        pltpu.VMEM((NB, CH, WG), F32), pltpu.VMEM((NB, CH, 128), F32),
                        pltpu.VMEM((NB, CH, WG), F32),
                        pltpu.VMEM((NB, ML_H, ML_DQK, ML_DV), F32), pltpu.VMEM((NB, ML_H, 1, ML_DQK), F32),
                        pltpu.VMEM((NB, ML_H, 1, 1), F32)],
    )
    return pl.pallas_call(
        functools.partial(_mlstm_kernel, nch=nch, reverse=reverse, final=final), grid_spec=gs,
        out_shape=jax.ShapeDtypeStruct((WG // LANE, R, LANE), F32),
        compiler_params=_cparams(("arbitrary",)),
    )(first, last, *args)


def _outproj_kernel(gid_ref, ya_ref, yb_ref, yc_ref, yd_ref, x_ref, g1_ref, sc_ref, sh_ref, wo_ref,
                    lg_ref, lb_ref, rw_ref, rb_ref, x1_ref, h2_ref, ti_ref, tw_ref):
    o = _bdot(_cat(ya_ref), wo_ref[0:WG, :])
    o = o + _bdot(_cat(yb_ref), wo_ref[WG:2 * WG, :])
    o = o + _bdot(_cat(yc_ref), wo_ref[2 * WG:3 * WG, :])
    o = o + _bdot(_cat(yd_ref), wo_ref[3 * WG:4 * WG, :])
    v = ALPHA_DN * x_ref[...] + _per_slab(o, g1_ref[0])
    x1 = _layer_norm(v, lg_ref[...], lb_ref[...])
    x1_ref[...] = x1
    h2 = _per_slab_add(_per_slab(x1, 1.0 + sc_ref[0]), sh_ref[0])
    bits = lax.bitcast_convert_type(h2.astype(BF16).astype(F32), jnp.uint32)
    h2_ref[...] = (bits[:, 0:D // 2] & jnp.uint32(0xFFFF0000)) | (bits[:, D // 2:D] >> jnp.uint32(16))
    logit = _bdot(h2, rw_ref[...]) + rb_ref[...]
    tm = logit.shape[0]
    lane = lax.broadcasted_iota(jnp.int32, (tm, N_EXP), 1)
    lane8 = lax.broadcasted_iota(jnp.int32, (tm, 8), 1)
    idx8 = jnp.zeros((tm, 8), jnp.int32)
    val8 = jnp.zeros((tm, 8), F32)
    top0 = None
    den = None
    for k in range(TOP_K):
        mx = jnp.max(logit, axis=1, keepdims=True)
        sel = jnp.min(jnp.where(logit == mx, lane, N_EXP), axis=1, keepdims=True)
        if k == 0:
            top0 = mx
        ek = jnp.exp(mx - top0)
        den = ek if k == 0 else den + ek
        idx8 = jnp.where(lane8 == k, sel, idx8)
        val8 = jnp.where(lane8 == k, ek, val8)
        logit = jnp.where(lane == sel, -jnp.inf, logit)
    ti_ref[...] = idx8
    tw_ref[...] = val8 / den


def _outproj_call(gid3, ys, x, mod3, prm):
    R = x.shape[0]
    per = ROWS // K3_TM
    rspec = lambda w: pl.BlockSpec((K3_TM, w), lambda i, gid: (i, 0))
    yspec = pl.BlockSpec((WG // LANE, K3_TM, LANE), lambda i, gid: (0, i, 0))
    mspec = lambda k: pl.BlockSpec((1, NB, D), lambda i, gid: (gid[i // per], 0, k))
    cspec = lambda shape: pl.BlockSpec(shape, lambda i, gid: (0,) * len(shape))
    gs = pltpu.PrefetchScalarGridSpec(
        num_scalar_prefetch=1, grid=(R // K3_TM,),
        in_specs=[yspec, yspec, yspec, yspec, rspec(D), mspec(2), mspec(4), mspec(3),
                  cspec((4 * WG, D)), cspec((1, D)), cspec((1, D)), cspec((D, N_EXP)), cspec((1, N_EXP))],
        out_specs=[rspec(D), rspec(D // 2), rspec(8), rspec(8)],
    )
    return pl.pallas_call(
        _outproj_kernel, grid_spec=gs,
        out_shape=[jax.ShapeDtypeStruct((R, D), F32), jax.ShapeDtypeStruct((R, D // 2), jnp.uint32),
                   jax.ShapeDtypeStruct((R, 8), jnp.int32), jax.ShapeDtypeStruct((R, 8), F32)],
        compiler_params=_cparams(("arbitrary",)),
    )(gid3, *ys, x, mod3, mod3, mod3, prm['w_out'], prm['ln1_g'], prm['ln1_b'], prm['router_w'], prm['router_b'])


def _moe_gather(idx_ref, h2_ref, dst, blk):
    s0 = (blk + 1) * MOE_RB
    for r in range(MOE_RB):
        tok = idx_ref[0, 0, s0 + r] & 0xFFFF
        dst[r:r + 1, :] = h2_ref[pl.ds(tok, 1), :]


def _moe_scatter(idx_ref, gw_ref, o_ref, ysrc, blk):
    s0 = (blk + 1) * MOE_RB
    for q in range(MOE_RB // MOE_UNROLL):
        toks, vals = [], []
        for j in range(MOE_UNROLL):
            r = q * MOE_UNROLL + j
            tok = idx_ref[0, 0, s0 + r] >> 16
            w = gw_ref[0, 0, s0 + r]
            toks.append(tok)
            vals.append(o_ref[0, pl.ds(tok, 1), :] + w * ysrc[r:r + 1, :])
        for tok, val in zip(toks, vals):
            o_ref[0, pl.ds(tok, 1), :] = val


def _moe_ffn(lhs, wgu_ref, wd_ref, bgu_ref, bd_ref, ydst):
    pk = lhs[...]
    lo = lax.bitcast_convert_type(pk << jnp.uint32(16), F32)
    hi = lax.bitcast_convert_type(pk & jnp.uint32(0xFFFF0000), F32)
    xb = jnp.concatenate([hi, lo], axis=1).astype(BF16)
    gu = jnp.dot(xb, wgu_ref[0], preferred_element_type=F32) + bgu_ref[0]
    g = jnp.minimum(gu[:, 0:D_FF], SWIGLU_LIMIT)
    u = jnp.clip(gu[:, D_FF:2 * D_FF], -SWIGLU_LIMIT, SWIGLU_LIMIT)
    hdn = (u + 1.0) * g * _sigmoid(SWIGLU_ALPHA * g)
    ydst[...] = jnp.dot(hdn.astype(BF16), wd_ref[0], preferred_element_type=F32) + bd_ref[0]


def _moe_kernel(bexp_ref, nblk_ref, idx_ref, gw_ref, h2_ref, wgu0, wd0, bgu0, bd0, wgu1, wd1, bgu1, bd1,
                o_ref, la_s, lb_s, ya_s, yb_s):
    i = pl.program_id(0)
    s = pl.program_id(1)
    b0 = 2 * s

    @pl.when(s == 0)
    def _():
        o_ref[...] = jnp.zeros_like(o_ref)
        yb_s[...] = jnp.zeros_like(yb_s)
        _moe_gather(idx_ref, h2_ref, la_s, b0)

    @pl.when(b0 <= nblk_ref[i])
    def _():
        _moe_scatter(idx_ref, gw_ref, o_ref, yb_s, b0 - 1)
        _moe_gather(idx_ref, h2_ref, lb_s, b0 + 1)
        _moe_ffn(la_s, wgu0, wd0, bgu0, bd0, ya_s)
        _moe_scatter(idx_ref, gw_ref, o_ref, ya_s, b0)
        _moe_gather(idx_ref, h2_ref, la_s, b0 + 2)
        _moe_ffn(lb_s, wgu1, wd1, bgu1, bd1, yb_s)


def _moe_dims(ts):
    nbmax = ts * TOP_K // MOE_RB + N_EXP
    assert nbmax % 2 == 0
    nsteps = nbmax // 2 + 1
    nba = nbmax + 4
    return nbmax, nsteps, nba


def _moe_call(bexp, nblk, idx, wp, h2p, prm, ts):
    R = h2p.shape[0]
    nst = R // ts
    nbmax, nsteps, nba = _moe_dims(ts)
    one = pl.Buffered(1)

    def wspecs(k):
        wspec = lambda shape: pl.BlockSpec(shape, lambda i, s, bexp, nblk: (bexp[i * (nbmax + 2) + 2 * s + k], 0, 0))
        return [wspec((1, D, 2 * D_FF)), wspec((1, D_FF, D)), wspec((1, 1, 2 * D_FF)), wspec((1, 1, D))]

    gs = pltpu.PrefetchScalarGridSpec(
        num_scalar_prefetch=2, grid=(nst, nsteps),
        in_specs=[pl.BlockSpec((1, 1, nba * MOE_RB), lambda i, s, *_: (i, 0, 0), memory_space=pltpu.SMEM),
                  pl.BlockSpec((1, 1, nba * MOE_RB), lambda i, s, *_: (i, 0, 0), memory_space=pltpu.SMEM),
                  pl.BlockSpec((ts, D // 2), lambda i, s, *_: (i, 0), pipeline_mode=one)] + wspecs(0) + wspecs(1),
        out_specs=pl.BlockSpec((1, ts + MOE_UNROLL, D), lambda i, s, *_: (i, 0, 0), pipeline_mode=one),
        scratch_shapes=[pltpu.VMEM((MOE_RB, D // 2), jnp.uint32), pltpu.VMEM((MOE_RB, D // 2), jnp.uint32),
                        pltpu.VMEM((MOE_RB, D), F32), pltpu.VMEM((MOE_RB, D), F32)],
    )
    wts = [prm['moe_wgu'], prm['moe_wd'], prm['moe_bgu'], prm['moe_bd']]
    return pl.pallas_call(
        _moe_kernel, grid_spec=gs,
        out_shape=jax.ShapeDtypeStruct((nst, ts + MOE_UNROLL, D), F32),
        compiler_params=_cparams(("arbitrary", "arbitrary")),
    )(bexp, nblk, idx, wp, h2p, *wts, *wts)


def _moe_plan(topi, topw, ts):
    R = topi.shape[0]
    nst = R // ts
    na = ts * TOP_K
    nbmax, nsteps, nba = _moe_dims(ts)
    e_flat = topi[:, :TOP_K].reshape(nst, na)
    w_flat = topw[:, :TOP_K].reshape(nst, na)
    key = jnp.sort(e_flat * na + jnp.arange(na, dtype=jnp.int32)[None, :], axis=1)
    a_sorted = key % na
    ex = jnp.arange(N_EXP, dtype=jnp.int32)
    counts = jnp.sum(e_flat[:, :, None] == ex[None, None, :], axis=1).astype(jnp.int32)
    padded = (counts + MOE_RB - 1) // MOE_RB * MOE_RB
    pad_end = jnp.cumsum(padded, axis=1)
    pad_start = pad_end - padded
    raw_start = jnp.cumsum(counts, axis=1) - counts
    total = pad_end[:, -1:]
    nblk = (total[:, 0] // MOE_RB).astype(jnp.int32)
    blk0 = jnp.minimum(jnp.arange(nbmax + 2, dtype=jnp.int32)[None, :] * MOE_RB, total - MOE_RB)
    bexp = jnp.minimum(jnp.sum(blk0[:, :, None] >= pad_end[:, None, :], axis=2), N_EXP - 1).astype(jnp.int32)
    slot = jnp.arange(nba * MOE_RB, dtype=jnp.int32)[None, :] - MOE_RB
    e_slot = jnp.repeat(jnp.concatenate([bexp[:, :1], bexp, bexp[:, -1:]], axis=1), MOE_RB, axis=1)
    within = slot - jnp.take_along_axis(pad_start, e_slot, axis=1)
    valid = (slot >= 0) & (slot < total) & (within < jnp.take_along_axis(counts, e_slot, axis=1))
    src = jnp.clip(jnp.take_along_axis(raw_start, e_slot, axis=1) + within, 0, na - 1)
    a_slot = jnp.take_along_axis(a_sorted, src, axis=1)
    tok = a_slot // TOP_K
    spare = ts + (slot & (MOE_UNROLL - 1))
    idx = jnp.where(valid, tok | (tok << 16), spare << 16).astype(jnp.int32)
    wp = jnp.where(valid, jnp.take_along_axis(w_flat, a_slot, axis=1), 0.0)
    return bexp.reshape(-1), nblk, idx.reshape(nst, 1, -1), wp.reshape(nst, 1, -1)


def _wperm_kernel(w_ref, p_ref, o_ref):
    o_ref[0] = jnp.dot(w_ref[0].astype(BF16), p_ref[...], preferred_element_type=F32).astype(BF16)


def _wperm_call(wgu):
    rows = lax.broadcasted_iota(jnp.int32, (2 * D_FF, 2 * D_FF), 0)
    cols = lax.broadcasted_iota(jnp.int32, (2 * D_FF, 2 * D_FF), 1)
    perm = (rows == 2 * (cols % D_FF) + cols // D_FF).astype(BF16)
    return pl.pallas_call(
        _wperm_kernel,
        grid=(N_EXP, 2),
        in_specs=[pl.BlockSpec((1, D, 2 * D_FF), lambda e, h: (e, 0, 0)),
                  pl.BlockSpec((2 * D_FF, D_FF), lambda e, h: (0, h))],
        out_specs=pl.BlockSpec((1, D, D_FF), lambda e, h: (e, 0, h)),
        out_shape=jax.ShapeDtypeStruct((N_EXP, D, 2 * D_FF), BF16),
        compiler_params=_cparams(("arbitrary", "arbitrary")),
    )(wgu, perm)


def _ln2_kernel(gid_ref, x1_ref, y_ref, g2_ref, lg_ref, lb_ref, o_ref):
    v = ALPHA_DN * x1_ref[...] + _per_slab(y_ref[0], g2_ref[0])
    o_ref[...] = _layer_norm(v, lg_ref[...], lb_ref[...])


def _ln2_call(gid, x1, y, mod3, prm):
    R = x1.shape[0]
    per = (y.shape[1] - MOE_UNROLL) // ROWS
    rspec = pl.BlockSpec((ROWS, D), lambda i, gid: (i, 0))
    yspec = pl.BlockSpec((1, ROWS, D), lambda i, gid: (i // per, i % per, 0))
    cspec = pl.BlockSpec((1, D), lambda i, gid: (0, 0))
    gs = pltpu.PrefetchScalarGridSpec(
        num_scalar_prefetch=1, grid=(R // ROWS,),
        in_specs=[rspec, yspec, pl.BlockSpec((1, NB, D), lambda i, gid: (gid[i], 0, 5)), cspec, cspec],
        out_specs=rspec,
    )
    return pl.pallas_call(
        _ln2_kernel, grid_spec=gs, out_shape=jax.ShapeDtypeStruct((R, D), F32),
        compiler_params=_cparams(("arbitrary",)),
    )(gid, x1, y, mod3, prm['ln2_g'], prm['ln2_b'])


def _block_diag(blocks):
    n, r, c = blocks.shape
    eye = jnp.eye(n, dtype=blocks.dtype)
    return jnp.einsum('nrc,nm->nrmc', blocks, eye).reshape(n * r, n * c)


def _prep_layer(p, l):
    prm = {}
    w = p['w_in'][l]
    pieces = [w[:, 0:512], w[:, 512:1024], w[:, 1280:1792], w[:, 1808:2320], w[:, 2320:2832], w[:, 3344:3856],
              w[:, 3856:4368], w[:, 2832:3088], w[:, 3088:3344], w[:, 1024:1152], w[:, 1152:1280],
              w[:, 1792:1808], w[:, 4368:4384], jnp.zeros((D, 96), w.dtype)]
    prm['w_in'] = jnp.concatenate(pieces, axis=1).astype(BF16)
    b_re, b_im = p['s5_b_re'][l], p['s5_b_im'][l]
    bt = []
    for j in range(4):
        sl = slice(8 * j, 8 * j + 8)
        bre = _block_diag(jnp.swapaxes(b_re[sl], 1, 2))
        bim = _block_diag(jnp.swapaxes(b_im[sl], 1, 2))
        bt.append(jnp.concatenate([bre, bim], axis=1))
    prm['s5_bt'] = jnp.stack(bt).astype(BF16)
    lam = lax.complex(p['s5_lam_re'][l], p['s5_lam_im'][l])
    dt = jnp.exp(p['s5_log_dt'][l])[:, :, None]
    lam_bar = jnp.exp(lam * dt)
    f = (lam_bar - 1.0) / lam
    prm['s5_lr'] = jnp.real(lam_bar).reshape(2, 4, 1, WG)
    prm['s5_li'] = jnp.imag(lam_bar).reshape(2, 4, 1, WG)
    c_c = lax.complex(p['s5_c_re'][l], p['s5_c_im'][l])
    cms = []
    for d in range(2):
        e = c_c * f[d][:, None, :]
        er = jnp.swapaxes(jnp.real(e), 1, 2)
        ei = jnp.swapaxes(jnp.imag(e), 1, 2)
        cm = []
        for j in range(4):
            sl = slice(8 * j, 8 * j + 8)
            cm.append(jnp.concatenate([_block_diag(er[sl]), -_block_diag(ei[sl])], axis=0))
        cms.append(jnp.stack(cm))
    prm['s5_cm'] = jnp.stack(cms).astype(BF16)
    prm['s5_d'] = p['s5_d'][l].reshape(1, WG)
    prm['s5_glu_w'] = p['s5_glu_w'][l].astype(BF16)
    prm['s5_glu_b'] = p['s5_glu_b'][l].reshape(1, WG)
    prm['ssd_conv_w'] = p['ssd_conv_w'][l]
    prm['ssd_conv_b'] = p['ssd_conv_b'][l].reshape(1, -1)
    pad = lambda v: jnp.concatenate([v.reshape(-1), jnp.zeros((128 - v.size,), F32)]).reshape(1, 128)
    prm['ssd_dtb'] = pad(p['ssd_dt_bias'][l])
    prm['ssd_arow'] = pad(-jnp.exp(p['ssd_a_log'][l]))
    prm['ssd_drow'] = jnp.repeat(p['ssd_d'][l], SSD_P).reshape(1, WG)
    prm['ssd_norm_g'] = p['ssd_norm_g'][l].reshape(1, WG)
    prm['lru_conv_w'] = p['lru_conv_w'][l]
    prm['lru_conv_b'] = p['lru_conv_b'][l].reshape(1, WG)
    prm['lru_wa'] = jnp.stack([_block_diag(p['lru_wa'][l][d]) for d in range(2)]).astype(BF16)
    prm['lru_wx'] = jnp.stack([_block_diag(p['lru_wx'][l][d]) for d in range(2)]).astype(BF16)
    prm['lru_ba'] = p['lru_ba'][l].reshape(2, 1, WG)
    prm['lru_bx'] = p['lru_bx'][l].reshape(2, 1, WG)
    prm['lru_sp'] = jax.nn.softplus(-p['lru_lam'][l]).reshape(2, 1, WG)
    prm['ml_gb'] = jnp.concatenate([jnp.zeros((SM_GATE,), F32), p['ml_gate_b'][l].reshape(-1),
                                    jnp.zeros((128 - SM_GATE - 4 * ML_H,), F32)]).reshape(1, 128)
    prm['ml_norm_g'] = p['ml_norm_g'][l].reshape(1, WG)
    prm['w_out'] = p['w_out'][l].astype(BF16)
    for k in ('ln1_g', 'ln1_b', 'ln2_g', 'ln2_b'):
        prm[k] = p[k][l].reshape(1, D)
    prm['router_w'] = p['router_w'][l]
    prm['router_b'] = p['router_b'][l].reshape(1, N_EXP)
    prm['moe_wgu'] = _wperm_call(p['moe_w_gate_up'][l])
    bgu = p['moe_b_gate_up'][l]
    prm['moe_bgu'] = jnp.concatenate([bgu[:, 0::2], bgu[:, 1::2]], axis=1).reshape(N_EXP, 1, 2 * D_FF)
    prm['moe_wd'] = p['moe_w_down'][l].astype(BF16)
    prm['moe_bd'] = p['moe_b_down'][l].reshape(N_EXP, 1, D)
    prm['ada_w'] = p['ada_w'][l].astype(BF16)
    prm['ada_b'] = p['ada_b'][l].reshape(1, 6 * D)
    return prm


def _mixers(flags, proj, prm):
    ya = _s5_call(flags, proj, None, prm, 0, reverse=False, final=False)
    ya = _s5_call(flags, proj, ya, prm, 1, reverse=True, final=True)
    yb = _ssd_call(flags, proj, None, prm, reverse=False, final=False)
    yb = _ssd_call(flags, proj, yb, prm, reverse=True, final=True)
    yc = _lru_call(flags, proj, None, prm, 0, reverse=False, final=False)
    yc = _lru_call(flags, proj, yc, prm, 1, reverse=True, final=True)
    yd = _mlstm_call(flags, proj, None, prm, reverse=False, final=False)
    yd = _mlstm_call(flags, proj, yd, prm, reverse=True, final=True)
    return ya, yb, yc, yd


def _to_rows(x):
    b, L, _ = x.shape
    return x.reshape(b // NB, NB, L, D).transpose(0, 2, 1, 3).reshape(b * L, D)


def _from_rows(r, b, L):
    return r.reshape(b // NB, L, NB, D).transpose(0, 2, 1, 3).reshape(b, L, D)


def _trunk(xs, cs, p, depth, moe_ts):
    first, last, gid = [], [], []
    g = 0
    for x in xs:
        b, L, _ = x.shape
        assert b % NB == 0 and L % CH == 0
        n = L // CH
        for _ in range(b // NB):
            first += [1] + [0] * (n - 1)
            last += [0] * (n - 1) + [1]
            gid += [g] * n
            g += 1
    first = jnp.asarray(first, jnp.int32)
    last = jnp.asarray(last, jnp.int32)
    gid = jnp.asarray(gid, jnp.int32)
    flags = (first, last)
    x = jnp.concatenate([_to_rows(x.astype(F32)) for x in xs], axis=0)
    c_all = jnp.concatenate([c.astype(F32) for c in cs], axis=0)
    R = x.shape[0]
    ts = min(moe_ts, R)
    assert R % ts == 0
    for l in range(depth):
        prm = _prep_layer(p, l)
        mod3 = _mod_call(c_all, prm['ada_w'], prm['ada_b']).reshape(g, NB, 6 * D)
        proj = _inproj_call(gid, x, mod3, prm['w_in'])
        ys = _mixers(flags, proj, prm)
        x1, h2, topi, topw = _outproj_call(gid, ys, x, mod3, prm)
        bexp, nblk, idx, wp = _moe_plan(topi, topw, ts)
        y = _moe_call(bexp, nblk, idx, wp, h2, prm, ts)
        x = _ln2_call(gid, x1, y, mod3, prm)
    outs = []
    r0 = 0
    for xin in xs:
        b, L, _ = xin.shape
        outs.append(_from_rows(x[r0:r0 + b * L], b, L).astype(xin.dtype))
        r0 += b * L
    return outs


def kernel(x_prompt, x_sample, c_prompt, c_sample, ada_w, ada_b, w_in, s5_lam_re, s5_lam_im, s5_log_dt, s5_b_re, s5_b_im, s5_c_re, s5_c_im, s5_d, s5_glu_w, s5_glu_b, ssd_conv_w, ssd_conv_b, ssd_a_log, ssd_dt_bias, ssd_d, ssd_norm_g, lru_conv_w, lru_conv_b, lru_wa, lru_ba, lru_wx, lru_bx, lru_lam, ml_gate_b, ml_norm_g, w_out, ln1_g, ln1_b, router_w, router_b, moe_w_gate_up, moe_b_gate_up, moe_w_down, moe_b_down, ln2_g, ln2_b):
    p = {
        'ada_w': ada_w, 'ada_b': ada_b, 'w_in': w_in,
        's5_lam_re': s5_lam_re, 's5_lam_im': s5_lam_im, 's5_log_dt': s5_log_dt,
        's5_b_re': s5_b_re, 's5_b_im': s5_b_im, 's5_c_re': s5_c_re, 's5_c_im': s5_c_im,
        's5_d': s5_d, 's5_glu_w': s5_glu_w, 's5_glu_b': s5_glu_b,
        'ssd_conv_w': ssd_conv_w, 'ssd_conv_b': ssd_conv_b, 'ssd_a_log': ssd_a_log,
        'ssd_dt_bias': ssd_dt_bias, 'ssd_d': ssd_d, 'ssd_norm_g': ssd_norm_g,
        'lru_conv_w': lru_conv_w, 'lru_conv_b': lru_conv_b, 'lru_wa': lru_wa, 'lru_ba': lru_ba,
        'lru_wx': lru_wx, 'lru_bx': lru_bx, 'lru_lam': lru_lam,
        'ml_gate_b': ml_gate_b, 'ml_norm_g': ml_norm_g, 'w_out': w_out,
        'ln1_g': ln1_g, 'ln1_b': ln1_b, 'router_w': router_w, 'router_b': router_b,
        'moe_w_gate_up': moe_w_gate_up, 'moe_b_gate_up': moe_b_gate_up,
        'moe_w_down': moe_w_down, 'moe_b_down': moe_b_down, 'ln2_g': ln2_g, 'ln2_b': ln2_b,
    }
    y_prompt, y_sample = _trunk([x_prompt, x_sample], [c_prompt, c_sample], p, DEPTH, MOE_TS)
    return (y_prompt, y_sample)
```

```python
import functools
import math

import numpy as np
import jax
import jax.numpy as jnp
from jax import lax
from jax.experimental import pallas as pl
from jax.experimental.pallas import tpu as pltpu

F32 = jnp.float32
BF16 = jnp.bfloat16
HIGHEST = lax.Precision.HIGHEST

D = 1024
DEPTH = 4
WG = 512
S5_G, S5_CH, S5_N = 32, 16, 64
SSD_H, SSD_P, SSD_NG, SSD_N = 8, 64, 2, 64
LRU_NB, LRU_BD, LRU_C = 8, 64, 8.0
ML_H, ML_DQK, ML_DV = 4, 64, 128
N_EXP, TOP_K, D_FF = 32, 4, 1024
SWIGLU_LIMIT, SWIGLU_ALPHA = 7.0, 1.702
ALPHA_DN = (2.0 * DEPTH) ** 0.25
LN_EPS, RMS_EPS = 1e-5, 1e-6

LANE = 128
NB = 8
CH = 128
ROWS = CH * NB
PW = 4480
K1_TN = 640
K3_TM = 512
MOE_RB = 128
MOE_UNROLL = 8
MOE_TS = 4096
VMEM_LIMIT = 56 * 1024 * 1024

COL_U, COL_XS, COL_Z, COL_XL, COL_GL, COL_V, COL_O = 0, 1, 2, 3, 4, 5, 6
COL_Q, COL_K, COL_BC = 14, 15, 16
COL_SM = 34
SM_DT, SM_GATE = 0, 16


def _sigmoid(x):
    return 1.0 / (1.0 + jnp.exp(-x))


def _silu(x):
    return x * _sigmoid(x)


def _softplus(x):
    return jnp.maximum(x, 0.0) + jnp.log(1.0 + jnp.exp(-jnp.abs(x)))


def _gelu_tanh(x):
    return 0.5 * x * (1.0 + jnp.tanh(math.sqrt(2.0 / math.pi) * (x + 0.044715 * (x * x * x))))


def _layer_norm(v, g, b):
    mu = jnp.mean(v, axis=-1, keepdims=True)
    vc = v - mu
    var = jnp.mean(vc * vc, axis=-1, keepdims=True)
    return vc * lax.rsqrt(var + LN_EPS) * g + b


def _bdot(a, b):
    return jnp.dot(a.astype(BF16), b.astype(BF16), preferred_element_type=F32)


def _per_slab(x, m):
    n = x.shape[0] // NB
    return (x.reshape(n, NB, x.shape[1]) * m[None]).reshape(x.shape)


def _per_slab_add(x, m):
    n = x.shape[0] // NB
    return (x.reshape(n, NB, x.shape[1]) + m[None]).reshape(x.shape)


def _cparams(sem):
    return pltpu.CompilerParams(dimension_semantics=sem, vmem_limit_bytes=VMEM_LIMIT)


def _mod_kernel(c_ref, w_ref, b_ref, o_ref):
    o_ref[...] = _bdot(_silu(c_ref[...]), w_ref[...]) + b_ref[...]


def _mod_call(c_all, w, b):
    n = c_all.shape[0]
    return pl.pallas_call(
        _mod_kernel,
        grid=(6,),
        in_specs=[pl.BlockSpec((n, D), lambda j: (0, 0)),
                  pl.BlockSpec((D, D), lambda j: (0, j)),
                  pl.BlockSpec((1, D), lambda j: (0, j))],
        out_specs=pl.BlockSpec((n, D), lambda j: (0, j)),
        out_shape=jax.ShapeDtypeStruct((n, 6 * D), F32),
        compiler_params=_cparams(("arbitrary",)),
    )(c_all, w, b)


def _inproj_kernel(gid_ref, x_ref, sc_ref, sh_ref, w_ref, o_ref, h_s):
    @pl.when(pl.program_id(1) == 0)
    def _():
        h = _per_slab_add(_per_slab(x_ref[...], 1.0 + sc_ref[0]), sh_ref[0])
        h_s[...] = h.astype(BF16)

    res = jnp.dot(h_s[...], w_ref[...], preferred_element_type=F32)
    for k in range(K1_TN // LANE):
        o_ref[k] = res[:, LANE * k:LANE * (k + 1)]


def _inproj_call(gid, x, mod3, w_in_p):
    R = x.shape[0]
    nch = R // ROWS
    gs = pltpu.PrefetchScalarGridSpec(
        num_scalar_prefetch=1,
        grid=(nch, PW // K1_TN),
        in_specs=[pl.BlockSpec((ROWS, D), lambda i, j, gid: (i, 0)),
                  pl.BlockSpec((1, NB, D), lambda i, j, gid: (gid[i], 0, 1)),
                  pl.BlockSpec((1, NB, D), lambda i, j, gid: (gid[i], 0, 0)),
                  pl.BlockSpec((D, K1_TN), lambda i, j, gid: (0, j))],
        out_specs=pl.BlockSpec((K1_TN // LANE, ROWS, LANE), lambda i, j, gid: (j, i, 0)),
        scratch_shapes=[pltpu.VMEM((ROWS, D), BF16)],
    )
    return pl.pallas_call(
        _inproj_kernel, grid_spec=gs,
        out_shape=jax.ShapeDtypeStruct((PW // LANE, R, LANE), F32),
        compiler_params=_cparams(("arbitrary", "arbitrary")),
    )(gid, x, mod3, mod3, w_in_p)


def _chunk_idx(nch, reverse):
    return (lambda c: nch - 1 - c) if reverse else (lambda c: c)


def _main_spec(width, col, nch, reverse):
    ci = _chunk_idx(nch, reverse)
    return pl.BlockSpec((width // LANE, ROWS, LANE), lambda c, *_: (col, ci(c), 0))


def _prev_spec(width, col, nch, reverse):
    ci = _chunk_idx(nch, reverse)
    per = ROWS // (2 * NB)
    return pl.BlockSpec((width // LANE, 2 * NB, LANE), lambda c, *_: (col, jnp.maximum(ci(c) * per - 1, 0), 0))


def _next_spec(width, col, nch, reverse):
    ci = _chunk_idx(nch, reverse)
    per = ROWS // NB
    return pl.BlockSpec((width // LANE, NB, LANE),
                        lambda c, *_: (col, jnp.minimum((ci(c) + 1) * per, nch * per - 1), 0))


def _const_spec(shape):
    nd = len(shape)
    return pl.BlockSpec(shape, lambda c, *_: (0,) * nd)


def _cat(ref):
    return jnp.concatenate([ref[k] for k in range(ref.shape[0])], axis=1)


def _put(ref, val):
    for k in range(ref.shape[0]):
        ref[k] = val[:, LANE * k:LANE * (k + 1)]


def _fill_ext(ext_s, k0, x_ref, xp_ref, xn_ref, is_first, is_last):
    for k in range(x_ref.shape[0]):
        ext_s[k0 + k, 0:2 * NB, :] = jnp.where(is_first, 0.0, xp_ref[k])
        ext_s[k0 + k, 2 * NB:2 * NB + ROWS, :] = x_ref[k]
        ext_s[k0 + k, 2 * NB + ROWS:3 * NB + ROWS, :] = jnp.where(is_last, 0.0, xn_ref[k])


def _dwconv(ext_s, k, w_ref, b_ref):
    sl = slice(LANE * k, LANE * (k + 1))
    out = b_ref[:, sl] + w_ref[0:1, sl] * ext_s[k, 0:ROWS, :]
    for j in range(1, 4):
        out = out + w_ref[j:j + 1, sl] * ext_s[k, j * NB:j * NB + ROWS, :]
    return out


def _tri(reverse):
    r = lax.broadcasted_iota(jnp.int32, (CH, CH), 0)
    c = lax.broadcasted_iota(jnp.int32, (CH, CH), 1)
    return (c >= r) if reverse else (c <= r)


def _s5_kernel(*refs, nch, reverse, final):
    first_ref, last_ref = refs[0], refs[1]
    if final:
        (u_ref, yp_ref, bt_ref, lr_ref, li_ref, cm_ref, d_ref, gw_ref, gb_ref,
         o_ref, bu_s, st_s, carry_s) = refs[2:]
    else:
        u_ref, bt_ref, lr_ref, li_ref, cm_ref, o_ref, bu_s, st_s, carry_s = refs[2:]
    c = pl.program_id(0)
    cc = nch - 1 - c if reverse else c
    start = last_ref[cc] if reverse else first_ref[cc]

    @pl.when(start == 1)
    def _():
        carry_s[...] = jnp.zeros_like(carry_s)

    half = WG
    for j in range(4):
        bu_s[...] = jnp.dot(u_ref[j].astype(BF16), bt_ref[j], preferred_element_type=F32)
        lr = jnp.broadcast_to(lr_ref[j], (NB, half))
        li = jnp.broadcast_to(li_ref[j], (NB, half))

        def step(i, carry, lr=lr, li=li):
            sr, si = carry
            t = CH - 1 - i if reverse else i
            r0 = pl.multiple_of(t * NB, NB)
            br = bu_s[pl.ds(r0, NB), 0:half]
            bi = bu_s[pl.ds(r0, NB), half:2 * half]
            nr = lr * sr - li * si + br
            ni = lr * si + li * sr + bi
            st_s[pl.ds(r0, NB), 0:half] = nr
            st_s[pl.ds(r0, NB), half:2 * half] = ni
            return nr, ni

        lax.fori_loop(0, CH, step, (carry_s[j, :, 0:half], carry_s[j, :, half:2 * half]), unroll=8)
        e0 = (CH - 1) * NB if reverse else 0
        carry_s[j] = st_s[e0:e0 + NB, :]
        o_ref[j] = jnp.dot(st_s[...].astype(BF16), cm_ref[j], preferred_element_type=F32)

    if final:
        y = _cat(o_ref) + _cat(yp_ref) + d_ref[...] * _cat(u_ref)
        g = _gelu_tanh(y)
        _put(o_ref, g * _sigmoid(_bdot(g, gw_ref[...]) + gb_ref[...]))


def _s5_call(flags, proj, yprev, prm, d, *, reverse, final):
    first, last = flags
    R = proj.shape[1]
    nch = R // ROWS
    in_specs = [_main_spec(WG, COL_U, nch, reverse)]
    args = [proj]
    if final:
        in_specs.append(_main_spec(WG, 0, nch, reverse))
        args.append(yprev)
    in_specs += [_const_spec((4, 128, 2 * WG)), _const_spec((4, 1, WG)), _const_spec((4, 1, WG)),
                 _const_spec((4, 2 * WG, 128))]
    args += [prm['s5_bt'], prm['s5_lr'][d], prm['s5_li'][d], prm['s5_cm'][d]]
    if final:
        in_specs += [_const_spec((1, WG)), _const_spec((WG, WG)), _const_spec((1, WG))]
        args += [prm['s5_d'], prm['s5_glu_w'], prm['s5_glu_b']]
    gs = pltpu.PrefetchScalarGridSpec(
        num_scalar_prefetch=2, grid=(nch,), in_specs=in_specs,
        out_specs=_main_spec(WG, 0, nch, reverse),
        scratch_shapes=[pltpu.VMEM((ROWS, 2 * WG), F32), pltpu.VMEM((ROWS, 2 * WG), F32),
                        pltpu.VMEM((4, NB, 2 * WG), F32)],
    )
    return pl.pallas_call(
        functools.partial(_s5_kernel, nch=nch, reverse=reverse, final=final), grid_spec=gs,
        out_shape=jax.ShapeDtypeStruct((WG // LANE, R, LANE), F32),
        compiler_params=_cparams(("arbitrary",)),
    )(first, last, *args)


def _lru_kernel(*refs, nch, reverse, final):
    first_ref, last_ref = refs[0], refs[1]
    if final:
        (x_ref, xp_ref, xn_ref, gate_ref, hp_ref, cw_ref, cb_ref, wa_ref, ba_ref, wx_ref, bx_ref, sp_ref,
         o_ref, ext_s, a_s, inp_s, carry_s) = refs[2:]
    else:
        (x_ref, xp_ref, xn_ref, cw_ref, cb_ref, wa_ref, ba_ref, wx_ref, bx_ref, sp_ref,
         o_ref, ext_s, a_s, inp_s, carry_s) = refs[2:]
    c = pl.program_id(0)
    cc = nch - 1 - c if reverse else c
    is_first = first_ref[cc] == 1
    is_last = last_ref[cc] == 1
    start = is_last if reverse else is_first

    @pl.when(start)
    def _():
        carry_s[...] = jnp.zeros_like(carry_s)

    _fill_ext(ext_s, 0, x_ref, xp_ref, xn_ref, is_first, is_last)
    xc = jnp.concatenate([_dwconv(ext_s, k, cw_ref, cb_ref) for k in range(WG // LANE)], axis=1)
    xb = xc.astype(BF16)
    r = _sigmoid(jnp.dot(xb, wa_ref[...], preferred_element_type=F32) + ba_ref[...])
    i = _sigmoid(jnp.dot(xb, wx_ref[...], preferred_element_type=F32) + bx_ref[...])
    log_a = -LRU_C * r * sp_ref[...]
    a_s[...] = jnp.exp(log_a)
    inp_s[...] = jnp.sqrt(1.0 - jnp.exp(2.0 * log_a)) * (i * xc)

    def step(k, h):
        t = CH - 1 - k if reverse else k
        r0 = pl.multiple_of(t * NB, NB)
        h = a_s[pl.ds(r0, NB), :] * h + inp_s[pl.ds(r0, NB), :]
        for k in range(WG // LANE):
            o_ref[k, pl.ds(r0, NB), :] = h[:, LANE * k:LANE * (k + 1)]
        return h

    carry_s[...] = lax.fori_loop(0, CH, step, carry_s[...], unroll=8)
    if final:
        for k in range(WG // LANE):
            o_ref[k] = (o_ref[k] + hp_ref[k]) * _gelu_tanh(gate_ref[k])


def _lru_call(flags, proj, hprev, prm, d, *, reverse, final):
    first, last = flags
    R = proj.shape[1]
    nch = R // ROWS
    in_specs = [_main_spec(WG, COL_XL, nch, reverse), _prev_spec(WG, COL_XL, nch, reverse),
                _next_spec(WG, COL_XL, nch, reverse)]
    args = [proj, proj, proj]
    if final:
        in_specs += [_main_spec(WG, COL_GL, nch, reverse), _main_spec(WG, 0, nch, reverse)]
        args += [proj, hprev]
    in_specs += [_const_spec((4, WG)), _const_spec((1, WG)), _const_spec((WG, WG)), _const_spec((1, WG)),
                 _const_spec((WG, WG)), _const_spec((1, WG)), _const_spec((1, WG))]
    args += [prm['lru_conv_w'], prm['lru_conv_b'], prm['lru_wa'][d], prm['lru_ba'][d], prm['lru_wx'][d],
             prm['lru_bx'][d], prm['lru_sp'][d]]
    gs = pltpu.PrefetchScalarGridSpec(
        num_scalar_prefetch=2, grid=(nch,), in_specs=in_specs,
        out_specs=_main_spec(WG, 0, nch, reverse),
        scratch_shapes=[pltpu.VMEM((WG // LANE, ROWS + 3 * NB, LANE), F32), pltpu.VMEM((ROWS, WG), F32),
                        pltpu.VMEM((ROWS, WG), F32), pltpu.VMEM((NB, WG), F32)],
    )
    return pl.pallas_call(
        functools.partial(_lru_kernel, nch=nch, reverse=reverse, final=final), grid_spec=gs,
        out_shape=jax.ShapeDtypeStruct((WG // LANE, R, LANE), F32),
        compiler_params=_cparams(("arbitrary",)),
    )(first, last, *args)


def _ssd_kernel(*refs, nch, reverse, final):
    first_ref, last_ref = refs[0], refs[1]
    if final:
        (x_ref, xp_ref, xn_ref, bc_ref, bcp_ref, bcn_ref, sm_ref, z_ref, yp_ref,
         cw_ref, cb_ref, dtb_ref, arow_ref, drow_ref, ng_ref,
         o_ref, ext_s, xd_s, smd_s, yd_s, st_s) = refs[2:]
    else:
        (x_ref, xp_ref, xn_ref, bc_ref, bcp_ref, bcn_ref, sm_ref,
         cw_ref, cb_ref, dtb_ref, arow_ref,
         o_ref, ext_s, xd_s, smd_s, yd_s, st_s) = refs[2:]
    c = pl.program_id(0)
    cc = nch - 1 - c if reverse else c
    is_first = first_ref[cc] == 1
    is_last = last_ref[cc] == 1
    start = is_last if reverse else is_first

    @pl.when(start)
    def _():
        st_s[...] = jnp.zeros_like(st_s)

    CW = WG + 2 * SSD_NG * SSD_N
    _fill_ext(ext_s, 0, x_ref, xp_ref, xn_ref, is_first, is_last)
    _fill_ext(ext_s, WG // LANE, bc_ref, bcp_ref, bcn_ref, is_first, is_last)
    for k in range(CW // LANE):
        ext_s[k, 0:ROWS, :] = _silu(_dwconv(ext_s, k, cw_ref, cb_ref))
    for b in range(NB):
        for k in range(CW // LANE):
            xd_s[b, :, LANE * k:LANE * (k + 1)] = ext_s[k, pl.ds(b, CH, stride=NB), :]
        smd_s[b] = sm_ref[0, pl.ds(b, CH, stride=NB), :]

    mask = _tri(reverse)
    tri = jnp.where(mask, 1.0, 0.0)
    doff = SM_DT + (SSD_H if reverse else 0)
    edge = 0 if reverse else CH - 1
    NS = SSD_N

    def per_b(b, _):
        xb = xd_s[b]
        xs = xb[:, 0:WG]
        bm = xb[:, WG:WG + SSD_NG * NS]
        cm = xb[:, WG + SSD_NG * NS:CW]
        dtf = _softplus(smd_s[b] + dtb_ref[...])
        af = dtf * arow_ref[...]
        cs = jnp.dot(tri, af, precision=HIGHEST, preferred_element_type=F32)
        cs_t = cs.T
        dt_t = dtf.T
        bm_t = bm.T
        gmat = [_bdot(cm[:, NS * g:NS * (g + 1)], bm_t[NS * g:NS * (g + 1), :]) for g in range(SSD_NG)]
        for h in range(SSD_H):
            ci = doff + h
            g = h // (SSD_H // SSD_NG)
            col = cs[:, ci:ci + 1]
            row = cs_t[ci:ci + 1, :]
            lmat = jnp.where(mask, jnp.exp(jnp.minimum(col - row, 0.0)), 0.0)
            m = gmat[g] * lmat * dt_t[ci:ci + 1, :]
            xh = xs[:, SSD_P * h:SSD_P * (h + 1)]
            s_prev = st_s[b, h]
            y = _bdot(m, xh) + jnp.exp(col) * _bdot(cm[:, NS * g:NS * (g + 1)], s_prev)
            tot = cs[edge:edge + 1, ci:ci + 1]
            wcol = jnp.exp(tot - col) * dtf[:, ci:ci + 1]
            st_s[b, h] = jnp.exp(tot) * s_prev + _bdot(bm_t[NS * g:NS * (g + 1), :], xh * wcol)
            yd_s[b, :, SSD_P * h:SSD_P * (h + 1)] = y
        return 0

    lax.fori_loop(0, NB, per_b, 0)
    for b in range(NB):
        for k in range(WG // LANE):
            o_ref[k, pl.ds(b, CH, stride=NB), :] = yd_s[b, :, LANE * k:LANE * (k + 1)]
    if final:
        xs_all = jnp.concatenate([ext_s[k, 0:ROWS, :] for k in range(WG // LANE)], axis=1)
        y = (_cat(o_ref) + _cat(yp_ref) + drow_ref[...] * xs_all) * _silu(_cat(z_ref))
        ms = jnp.mean(y * y, axis=-1, keepdims=True)
        _put(o_ref, y * lax.rsqrt(ms + RMS_EPS) * ng_ref[...])


def _ssd_call(flags, proj, yprev, prm, *, reverse, final):
    first, last = flags
    R = proj.shape[1]
    nch = R // ROWS
    CW = WG + 2 * SSD_NG * SSD_N
    in_specs = [_main_spec(WG, COL_XS, nch, reverse), _prev_spec(WG, COL_XS, nch, reverse),
                _next_spec(WG, COL_XS, nch, reverse),
                _main_spec(256, COL_BC, nch, reverse), _prev_spec(256, COL_BC, nch, reverse),
                _next_spec(256, COL_BC, nch, reverse),
                _main_spec(128, COL_SM, nch, reverse)]
    args = [proj] * 7
    if final:
        in_specs += [_main_spec(WG, COL_Z, nch, reverse), _main_spec(WG, 0, nch, reverse)]
        args += [proj, yprev]
    in_specs += [_const_spec((4, CW)), _const_spec((1, CW)), _const_spec((1, 128)), _const_spec((1, 128))]
    args += [prm['ssd_conv_w'], prm['ssd_conv_b'], prm['ssd_dtb'], prm['ssd_arow']]
    if final:
        in_specs += [_const_spec((1, WG)), _const_spec((1, WG))]
        args += [prm['ssd_drow'], prm['ssd_norm_g']]
    gs = pltpu.PrefetchScalarGridSpec(
        num_scalar_prefetch=2, grid=(nch,), in_specs=in_specs,
        out_specs=_main_spec(WG, 0, nch, reverse),
        scratch_shapes=[pltpu.VMEM((CW // LANE, ROWS + 3 * NB, LANE), F32), pltpu.VMEM((NB, CH, CW), F32),
                        pltpu.VMEM((NB, CH, 128), F32), pltpu.VMEM((NB, CH, WG), F32),
                        pltpu.VMEM((NB, SSD_H, SSD_N, SSD_P), F32)],
    )
    return pl.pallas_call(
        functools.partial(_ssd_kernel, nch=nch, reverse=reverse, final=final), grid_spec=gs,
        out_shape=jax.ShapeDtypeStruct((WG // LANE, R, LANE), F32),
        compiler_params=_cparams(("arbitrary",)),
    )(first, last, *args)


NEG = -1e30


def _mlstm_kernel(*refs, nch, reverse, final):
    first_ref, last_ref = refs[0], refs[1]
    if final:
        (q_ref, k_ref, v_ref, sm_ref, og_ref, hp_ref, gb_ref, ng_ref,
         o_ref, qd_s, kd_s, vd_s, smd_s, hd_s, c_s, n_s, m_s) = refs[2:]
    else:
        (q_ref, k_ref, v_ref, sm_ref, gb_ref,
         o_ref, qd_s, kd_s, vd_s, smd_s, hd_s, c_s, n_s, m_s) = refs[2:]
    c = pl.program_id(0)
    cc = nch - 1 - c if reverse else c
    start = (last_ref[cc] if reverse else first_ref[cc]) == 1

    @pl.when(start)
    def _():
        c_s[...] = jnp.zeros_like(c_s)
        n_s[...] = jnp.zeros_like(n_s)
        m_s[...] = jnp.zeros_like(m_s)

    for b in range(NB):
        for src, dst in ((q_ref, qd_s), (k_ref, kd_s), (v_ref, vd_s)):
            for k in range(src.shape[0]):
                dst[b, :, LANE * k:LANE * (k + 1)] = src[k, pl.ds(b, CH, stride=NB), :]
        smd_s[b] = sm_ref[0, pl.ds(b, CH, stride=NB), :]

    mask = _tri(reverse)
    tri = jnp.where(mask, 1.0, 0.0)
    goff = SM_GATE + (2 * ML_H if reverse else 0)
    edge = 0 if reverse else CH - 1

    def per_b(b, _):
        sm = smd_s[b] + gb_ref[...]
        logf = -_softplus(-sm)
        fc_all = jnp.dot(tri, logf, precision=HIGHEST, preferred_element_type=F32)
        fr_all = fc_all.T
        sm_t = sm.T
        qb = qd_s[b] * (ML_DQK ** -0.5)
        kb = kd_s[b]
        kb_t = kb.T
        vb = vd_s[b]
        for h in range(ML_H):
            ci = goff + h
            cf = goff + ML_H + h
            fc = fc_all[:, cf:cf + 1]
            fr = fr_all[cf:cf + 1, :]
            ir = sm_t[ci:ci + 1, :]
            ic = sm[:, ci:ci + 1]
            m_prev = m_s[b, h]
            dlog = jnp.where(mask, fc - fr + ir, NEG)
            inter = fc + m_prev
            m_t = jnp.maximum(jnp.max(dlog, axis=1, keepdims=True), inter)
            wts = jnp.exp(dlog - m_t)
            w_inter = jnp.exp(inter - m_t)
            qh = qb[:, ML_DQK * h:ML_DQK * (h + 1)]
            kh = kb[:, ML_DQK * h:ML_DQK * (h + 1)]
            kh_t = kb_t[ML_DQK * h:ML_DQK * (h + 1), :]
            vh = vb[:, ML_DV * h:ML_DV * (h + 1)]
            ct_prev = c_s[b, h]
            n_prev = n_s[b, h]
            s = _bdot(qh, kh_t) * wts
            num = _bdot(s, vh) + w_inter * _bdot(qh, ct_prev)
            den = jnp.sum(s, axis=1, keepdims=True) + w_inter * jnp.sum(qh * n_prev, axis=1, keepdims=True)
            hd_s[b, :, ML_DV * h:ML_DV * (h + 1)] = num / jnp.maximum(jnp.abs(den), jnp.exp(-m_t))
            f_end = fc_all[edge:edge + 1, cf:cf + 1]
            w_log = f_end - fc + ic
            m_new = jnp.maximum(f_end + m_prev, jnp.max(w_log, axis=0, keepdims=True))
            w_s = jnp.exp(w_log - m_new)
            scale = jnp.exp(f_end + m_prev - m_new)
            c_s[b, h] = scale * ct_prev + _bdot(kh_t, vh * w_s)
            n_s[b, h] = scale * n_prev + jnp.sum(kh * w_s, axis=0, keepdims=True)
            m_s[b, h] = m_new
        return 0

    lax.fori_loop(0, NB, per_b, 0)
    for b in range(NB):
        for k in range(WG // LANE):
            o_ref[k, pl.ds(b, CH, stride=NB), :] = hd_s[b, :, LANE * k:LANE * (k + 1)]
    if final:
        assert ML_DV == LANE
        for h in range(ML_H):
            hh = o_ref[h] + hp_ref[h]
            ms = jnp.mean(hh * hh, axis=-1, keepdims=True)
            o_ref[h] = hh * lax.rsqrt(ms + RMS_EPS) * ng_ref[:, LANE * h:LANE * (h + 1)] * _sigmoid(og_ref[h])


def _mlstm_call(flags, proj, hprev, prm, *, reverse, final):
    first, last = flags
    R = proj.shape[1]
    nch = R // ROWS
    in_specs = [_main_spec(256, COL_Q, nch, reverse), _main_spec(256, COL_K, nch, reverse),
                _main_spec(WG, COL_V, nch, reverse), _main_spec(128, COL_SM, nch, reverse)]
    args = [proj] * 4
    if final:
        in_specs += [_main_spec(WG, COL_O, nch, reverse), _main_spec(WG, 0, nch, reverse)]
        args += [proj, hprev]
    in_specs += [_const_spec((1, 128))]
    args += [prm['ml_gb']]
    if final:
        in_specs += [_const_spec((1, WG))]
        args += [prm['ml_norm_g']]
    gs = pltpu.PrefetchScalarGridSpec(
        num_scalar_prefetch=2, grid=(nch,), in_specs=in_specs,
        out_specs=_main_spec(WG, 0, nch, reverse),
        scratch_shapes=[pltpu.VMEM((NB, CH, 256), F32), pltpu.VMEM((NB, CH, 256), F32),
                        pltpu.VMEM((NB, CH, WG), F32), pltpu.VMEM((NB, CH, 128), F32),
                        pltpu.VMEM((NB, CH, WG), F32),
                        pltpu.VMEM((NB, ML_H, ML_DQK, ML_DV), F32), pltpu.VMEM((NB, ML_H, 1, ML_DQK), F32),
                        pltpu.VMEM((NB, ML_H, 1, 1), F32)],
    )
    return pl.pallas_call(
        functools.partial(_mlstm_kernel, nch=nch, reverse=reverse, final=final), grid_spec=gs,
        out_shape=jax.ShapeDtypeStruct((WG // LANE, R, LANE), F32),
        compiler_params=_cparams(("arbitrary",)),
    )(first, last, *args)


def _outproj_kernel(gid_ref, ya_ref, yb_ref, yc_ref, yd_ref, x_ref, g1_ref, sc_ref, sh_ref, wo_ref,
                    lg_ref, lb_ref, rw_ref, rb_ref, x1_ref, h2_ref, ti_ref, tw_ref):
    o = _bdot(_cat(ya_ref), wo_ref[0:WG, :])
    o = o + _bdot(_cat(yb_ref), wo_ref[WG:2 * WG, :])
    o = o + _bdot(_cat(yc_ref), wo_ref[2 * WG:3 * WG, :])
    o = o + _bdot(_cat(yd_ref), wo_ref[3 * WG:4 * WG, :])
    v = ALPHA_DN * x_ref[...] + _per_slab(o, g1_ref[0])
    x1 = _layer_norm(v, lg_ref[...], lb_ref[...])
    x1_ref[...] = x1
    h2 = _per_slab_add(_per_slab(x1, 1.0 + sc_ref[0]), sh_ref[0])
    bits = lax.bitcast_convert_type(h2.astype(BF16).astype(F32), jnp.uint32)
    h2_ref[...] = (bits[:, 0:D // 2] & jnp.uint32(0xFFFF0000)) | (bits[:, D // 2:D] >> jnp.uint32(16))
    logit = _bdot(h2, rw_ref[...]) + rb_ref[...]
    tm = logit.shape[0]
    lane = lax.broadcasted_iota(jnp.int32, (tm, N_EXP), 1)
    lane8 = lax.broadcasted_iota(jnp.int32, (tm, 8), 1)
    idx8 = jnp.zeros((tm, 8), jnp.int32)
    val8 = jnp.zeros((tm, 8), F32)
    top0 = None
    den = None
    for k in range(TOP_K):
        mx = jnp.max(logit, axis=1, keepdims=True)
        sel = jnp.min(jnp.where(logit == mx, lane, N_EXP), axis=1, keepdims=True)
        if k == 0:
            top0 = mx
        ek = jnp.exp(mx - top0)
        den = ek if k == 0 else den + ek
        idx8 = jnp.where(lane8 == k, sel, idx8)
        val8 = jnp.where(lane8 == k, ek, val8)
        logit = jnp.where(lane == sel, -jnp.inf, logit)
    ti_ref[...] = idx8
    tw_ref[...] = val8 / den


def _outproj_call(gid3, ys, x, mod3, prm):
    R = x.shape[0]
    per = ROWS // K3_TM
    rspec = lambda w: pl.BlockSpec((K3_TM, w), lambda i, gid: (i, 0))
    yspec = pl.BlockSpec((WG // LANE, K3_TM, LANE), lambda i, gid: (0, i, 0))
    mspec = lambda k: pl.BlockSpec((1, NB, D), lambda i, gid: (gid[i // per], 0, k))
    cspec = lambda shape: pl.BlockSpec(shape, lambda i, gid: (0,) * len(shape))
    gs = pltpu.PrefetchScalarGridSpec(
        num_scalar_prefetch=1, grid=(R // K3_TM,),
        in_specs=[yspec, yspec, yspec, yspec, rspec(D), mspec(2), mspec(4), mspec(3),
                  cspec((4 * WG, D)), cspec((1, D)), cspec((1, D)), cspec((D, N_EXP)), cspec((1, N_EXP))],
        out_specs=[rspec(D), rspec(D // 2), rspec(8), rspec(8)],
    )
    return pl.pallas_call(
        _outproj_kernel, grid_spec=gs,
        out_shape=[jax.ShapeDtypeStruct((R, D), F32), jax.ShapeDtypeStruct((R, D // 2), jnp.uint32),
                   jax.ShapeDtypeStruct((R, 8), jnp.int32), jax.ShapeDtypeStruct((R, 8), F32)],
        compiler_params=_cparams(("arbitrary",)),
    )(gid3, *ys, x, mod3, mod3, mod3, prm['w_out'], prm['ln1_g'], prm['ln1_b'], prm['router_w'], prm['router_b'])


def _moe_gather(idx_ref, h2_ref, dst, blk):
    s0 = (blk + 1) * MOE_RB
    for r in range(MOE_RB):
        tok = idx_ref[0, 0, s0 + r] & 0xFFFF
        dst[r:r + 1, :] = h2_ref[pl.ds(tok, 1), :]


def _moe_scatter(idx_ref, gw_ref, o_ref, ysrc, blk):
    s0 = (blk + 1) * MOE_RB
    for q in range(MOE_RB // MOE_UNROLL):
        toks, vals = [], []
        for j in range(MOE_UNROLL):
            r = q * MOE_UNROLL + j
            tok = idx_ref[0, 0, s0 + r] >> 16
            w = gw_ref[0, 0, s0 + r]
            toks.append(tok)
            vals.append(o_ref[0, pl.ds(tok, 1), :] + w * ysrc[r:r + 1, :])
        for tok, val in zip(toks, vals):
            o_ref[0, pl.ds(tok, 1), :] = val


def _moe_ffn(lhs, wgu_ref, wd_ref, bgu_ref, bd_ref, ydst):
    pk = lhs[...]
    lo = lax.bitcast_convert_type(pk << jnp.uint32(16), F32)
    hi = lax.bitcast_convert_type(pk & jnp.uint32(0xFFFF0000), F32)
    xb = jnp.concatenate([hi, lo], axis=1).astype(BF16)
    gu = jnp.dot(xb, wgu_ref[0], preferred_element_type=F32) + bgu_ref[0]
    g = jnp.minimum(gu[:, 0:D_FF], SWIGLU_LIMIT)
    u = jnp.clip(gu[:, D_FF:2 * D_FF], -SWIGLU_LIMIT, SWIGLU_LIMIT)
    hdn = (u + 1.0) * g * _sigmoid(SWIGLU_ALPHA * g)
    ydst[...] = jnp.dot(hdn.astype(BF16), wd_ref[0], preferred_element_type=F32) + bd_ref[0]


def _moe_kernel(bexp_ref, nblk_ref, idx_ref, gw_ref, h2_ref, wgu_ref, wd_ref, bgu_ref, bd_ref,
                o_ref, la_s, lb_s, ya_s, yb_s):
    i = pl.program_id(0)
    s = pl.program_id(1)
    b0 = 2 * s

    @pl.when(s == 0)
    def _():
        o_ref[...] = jnp.zeros_like(o_ref)
        yb_s[...] = jnp.zeros_like(yb_s)
        _moe_gather(idx_ref, h2_ref, la_s, b0)

    @pl.when(b0 <= nblk_ref[i])
    def _():
        _moe_scatter(idx_ref, gw_ref, o_ref, yb_s, b0 - 1)
        _moe_gather(idx_ref, h2_ref, lb_s, b0 + 1)
        _moe_ffn(la_s, wgu_ref, wd_ref, bgu_ref, bd_ref, ya_s)
        _moe_scatter(idx_ref, gw_ref, o_ref, ya_s, b0)
        _moe_gather(idx_ref, h2_ref, la_s, b0 + 2)
        _moe_ffn(lb_s, wgu_ref, wd_ref, bgu_ref, bd_ref, yb_s)


MOE_RUN = 2 * MOE_RB


def _moe_dims(ts):
    nbmax = ts * TOP_K // MOE_RB + 2 * N_EXP
    nsteps = nbmax // 2 + 1
    nba = nbmax + 4
    return nbmax, nsteps, nba


def _moe_call(bexp, nblk, idx, wp, h2p, prm, ts):
    R = h2p.shape[0]
    nst = R // ts
    nbmax, nsteps, nba = _moe_dims(ts)
    one = pl.Buffered(1)
    wspec = lambda shape: pl.BlockSpec(shape, lambda i, s, bexp, nblk: (bexp[i * nsteps + s], 0, 0))
    gs = pltpu.PrefetchScalarGridSpec(
        num_scalar_prefetch=2, grid=(nst, nsteps),
        in_specs=[pl.BlockSpec((1, 1, nba * MOE_RB), lambda i, s, *_: (i, 0, 0), memory_space=pltpu.SMEM),
                  pl.BlockSpec((1, 1, nba * MOE_RB), lambda i, s, *_: (i, 0, 0), memory_space=pltpu.SMEM),
                  pl.BlockSpec((ts, D // 2), lambda i, s, *_: (i, 0), pipeline_mode=one),
                  wspec((1, D, 2 * D_FF)), wspec((1, D_FF, D)), wspec((1, 1, 2 * D_FF)), wspec((1, 1, D))],
        out_specs=pl.BlockSpec((1, ts + MOE_UNROLL, D), lambda i, s, *_: (i, 0, 0), pipeline_mode=one),
        scratch_shapes=[pltpu.VMEM((MOE_RB, D // 2), jnp.uint32), pltpu.VMEM((MOE_RB, D // 2), jnp.uint32),
                        pltpu.VMEM((MOE_RB, D), F32), pltpu.VMEM((MOE_RB, D), F32)],
    )
    return pl.pallas_call(
        _moe_kernel, grid_spec=gs,
        out_shape=jax.ShapeDtypeStruct((nst, ts + MOE_UNROLL, D), F32),
        compiler_params=_cparams(("arbitrary", "arbitrary")),
    )(bexp, nblk, idx, wp, h2p, prm['moe_wgu'], prm['moe_wd'], prm['moe_bgu'], prm['moe_bd'])


def _moe_plan(topi, topw, ts):
    R = topi.shape[0]
    nst = R // ts
    na = ts * TOP_K
    nbmax, nsteps, nba = _moe_dims(ts)
    e_flat = topi[:, :TOP_K].reshape(nst, na)
    w_flat = topw[:, :TOP_K].reshape(nst, na)
    ex = jnp.arange(N_EXP, dtype=jnp.int32)
    counts = jnp.sum(e_flat[:, :, None] == ex[None, None, :], axis=1).astype(jnp.int32)
    padded = (counts + MOE_RUN - 1) // MOE_RUN * MOE_RUN
    pad_end = jnp.cumsum(padded, axis=1)
    total = pad_end[:, -1:]
    nblk = (total[:, 0] // MOE_RB).astype(jnp.int32)
    blk0 = jnp.minimum(jnp.arange(nsteps, dtype=jnp.int32)[None, :] * MOE_RUN, total - MOE_RUN)
    bexp = jnp.minimum(jnp.sum(blk0[:, :, None] >= pad_end[:, None, :], axis=2), N_EXP - 1).astype(jnp.int32)
    m = na + MOE_RUN
    unused = N_EXP * m
    key_real = e_flat * m + jnp.arange(na, dtype=jnp.int32)[None, :]
    d = jnp.arange(MOE_RUN - 1, dtype=jnp.int32)[None, None, :]
    key_pad = jnp.where(d < (padded - counts)[:, :, None], ex[None, :, None] * m + na + d, unused)
    keys = jnp.concatenate([key_real, key_pad.reshape(nst, -1)], axis=1)
    vals = jnp.concatenate([w_flat, jnp.zeros((nst, N_EXP * (MOE_RUN - 1)), F32)], axis=1)
    keys, vals = lax.sort((keys, vals), dimension=1, num_keys=1)
    n_tail = nba * MOE_RB - MOE_RB - keys.shape[1]
    assert n_tail >= 0
    keys = jnp.concatenate([jnp.full((nst, MOE_RB), unused, jnp.int32), keys,
                            jnp.full((nst, n_tail), unused, jnp.int32)], axis=1)
    wp = jnp.concatenate([jnp.zeros((nst, MOE_RB), F32), vals, jnp.zeros((nst, n_tail), F32)], axis=1)
    j = keys % m
    real = (keys < unused) & (j < na)
    tok = j // TOP_K
    spare = ts + (jnp.arange(nba * MOE_RB, dtype=jnp.int32)[None, :] & (MOE_UNROLL - 1))
    idx = jnp.where(real, tok | (tok << 16), spare << 16).astype(jnp.int32)
    wp = jnp.where(real, wp, 0.0)
    return bexp.reshape(-1), nblk, idx.reshape(nst, 1, -1), wp.reshape(nst, 1, -1)


def _wperm_kernel(w_ref, p_ref, o_ref):
    o_ref[0] = jnp.dot(w_ref[0].astype(BF16), p_ref[...], preferred_element_type=F32).astype(BF16)


def _wperm_call(wgu):
    rows = lax.broadcasted_iota(jnp.int32, (2 * D_FF, 2 * D_FF), 0)
    cols = lax.broadcasted_iota(jnp.int32, (2 * D_FF, 2 * D_FF), 1)
    perm = (rows == 2 * (cols % D_FF) + cols // D_FF).astype(BF16)
    return pl.pallas_call(
        _wperm_kernel,
        grid=(N_EXP, 2),
        in_specs=[pl.BlockSpec((1, D, 2 * D_FF), lambda e, h: (e, 0, 0)),
                  pl.BlockSpec((2 * D_FF, D_FF), lambda e, h: (0, h))],
        out_specs=pl.BlockSpec((1, D, D_FF), lambda e, h: (e, 0, h)),
        out_shape=jax.ShapeDtypeStruct((N_EXP, D, 2 * D_FF), BF16),
        compiler_params=_cparams(("arbitrary", "arbitrary")),
    )(wgu, perm)


def _ln2_kernel(gid_ref, x1_ref, y_ref, g2_ref, lg_ref, lb_ref, o_ref):
    v = ALPHA_DN * x1_ref[...] + _per_slab(y_ref[0], g2_ref[0])
    o_ref[...] = _layer_norm(v, lg_ref[...], lb_ref[...])


def _ln2_call(gid, x1, y, mod3, prm):
    R = x1.shape[0]
    per = (y.shape[1] - MOE_UNROLL) // ROWS
    rspec = pl.BlockSpec((ROWS, D), lambda i, gid: (i, 0))
    yspec = pl.BlockSpec((1, ROWS, D), lambda i, gid: (i // per, i % per, 0))
    cspec = pl.BlockSpec((1, D), lambda i, gid: (0, 0))
    gs = pltpu.PrefetchScalarGridSpec(
        num_scalar_prefetch=1, grid=(R // ROWS,),
        in_specs=[rspec, yspec, pl.BlockSpec((1, NB, D), lambda i, gid: (gid[i], 0, 5)), cspec, cspec],
        out_specs=rspec,
    )
    return pl.pallas_call(
        _ln2_kernel, grid_spec=gs, out_shape=jax.ShapeDtypeStruct((R, D), F32),
        compiler_params=_cparams(("arbitrary",)),
    )(gid, x1, y, mod3, prm['ln2_g'], prm['ln2_b'])


def _block_diag(blocks):
    n, r, c = blocks.shape
    eye = jnp.eye(n, dtype=blocks.dtype)
    return jnp.einsum('nrc,nm->nrmc', blocks, eye).reshape(n * r, n * c)


def _prep_layer(p, l):
    prm = {}
    w = p['w_in'][l]
    pieces = [w[:, 0:512], w[:, 512:1024], w[:, 1280:1792], w[:, 1808:2320], w[:, 2320:2832], w[:, 3344:3856],
              w[:, 3856:4368], w[:, 2832:3088], w[:, 3088:3344], w[:, 1024:1152], w[:, 1152:1280],
              w[:, 1792:1808], w[:, 4368:4384], jnp.zeros((D, 96), w.dtype)]
    prm['w_in'] = jnp.concatenate(pieces, axis=1).astype(BF16)
    b_re, b_im = p['s5_b_re'][l], p['s5_b_im'][l]
    bt = []
    for j in range(4):
        sl = slice(8 * j, 8 * j + 8)
        bre = _block_diag(jnp.swapaxes(b_re[sl], 1, 2))
        bim = _block_diag(jnp.swapaxes(b_im[sl], 1, 2))
        bt.append(jnp.concatenate([bre, bim], axis=1))
    prm['s5_bt'] = jnp.stack(bt).astype(BF16)
    lam = lax.complex(p['s5_lam_re'][l], p['s5_lam_im'][l])
    dt = jnp.exp(p['s5_log_dt'][l])[:, :, None]
    lam_bar = jnp.exp(lam * dt)
    f = (lam_bar - 1.0) / lam
    prm['s5_lr'] = jnp.real(lam_bar).reshape(2, 4, 1, WG)
    prm['s5_li'] = jnp.imag(lam_bar).reshape(2, 4, 1, WG)
    c_c = lax.complex(p['s5_c_re'][l], p['s5_c_im'][l])
    cms = []
    for d in range(2):
        e = c_c * f[d][:, None, :]
        er = jnp.swapaxes(jnp.real(e), 1, 2)
        ei = jnp.swapaxes(jnp.imag(e), 1, 2)
        cm = []
        for j in range(4):
            sl = slice(8 * j, 8 * j + 8)
            cm.append(jnp.concatenate([_block_diag(er[sl]), -_block_diag(ei[sl])], axis=0))
        cms.append(jnp.stack(cm))
    prm['s5_cm'] = jnp.stack(cms).astype(BF16)
    prm['s5_d'] = p['s5_d'][l].reshape(1, WG)
    prm['s5_glu_w'] = p['s5_glu_w'][l].astype(BF16)
    prm['s5_glu_b'] = p['s5_glu_b'][l].reshape(1, WG)
    prm['ssd_conv_w'] = p['ssd_conv_w'][l]
    prm['ssd_conv_b'] = p['ssd_conv_b'][l].reshape(1, -1)
    pad = lambda v: jnp.concatenate([v.reshape(-1), jnp.zeros((128 - v.size,), F32)]).reshape(1, 128)
    prm['ssd_dtb'] = pad(p['ssd_dt_bias'][l])
    prm['ssd_arow'] = pad(-jnp.exp(p['ssd_a_log'][l]))
    prm['ssd_drow'] = jnp.repeat(p['ssd_d'][l], SSD_P).reshape(1, WG)
    prm['ssd_norm_g'] = p['ssd_norm_g'][l].reshape(1, WG)
    prm['lru_conv_w'] = p['lru_conv_w'][l]
    prm['lru_conv_b'] = p['lru_conv_b'][l].reshape(1, WG)
    prm['lru_wa'] = jnp.stack([_block_diag(p['lru_wa'][l][d]) for d in range(2)]).astype(BF16)
    prm['lru_wx'] = jnp.stack([_block_diag(p['lru_wx'][l][d]) for d in range(2)]).astype(BF16)
    prm['lru_ba'] = p['lru_ba'][l].reshape(2, 1, WG)
    prm['lru_bx'] = p['lru_bx'][l].reshape(2, 1, WG)
    prm['lru_sp'] = jax.nn.softplus(-p['lru_lam'][l]).reshape(2, 1, WG)
    prm['ml_gb'] = jnp.concatenate([jnp.zeros((SM_GATE,), F32), p['ml_gate_b'][l].reshape(-1),
                                    jnp.zeros((128 - SM_GATE - 4 * ML_H,), F32)]).reshape(1, 128)
    prm['ml_norm_g'] = p['ml_norm_g'][l].reshape(1, WG)
    prm['w_out'] = p['w_out'][l].astype(BF16)
    for k in ('ln1_g', 'ln1_b', 'ln2_g', 'ln2_b'):
        prm[k] = p[k][l].reshape(1, D)
    prm['router_w'] = p['router_w'][l]
    prm['router_b'] = p['router_b'][l].reshape(1, N_EXP)
    prm['moe_wgu'] = _wperm_call(p['moe_w_gate_up'][l])
    bgu = p['moe_b_gate_up'][l]
    prm['moe_bgu'] = jnp.concatenate([bgu[:, 0::2], bgu[:, 1::2]], axis=1).reshape(N_EXP, 1, 2 * D_FF)
    prm['moe_wd'] = p['moe_w_down'][l].astype(BF16)
    prm['moe_bd'] = p['moe_b_down'][l].reshape(N_EXP, 1, D)
    prm['ada_w'] = p['ada_w'][l].astype(BF16)
    prm['ada_b'] = p['ada_b'][l].reshape(1, 6 * D)
    return prm


def _mixers(flags, proj, prm):
    ya = _s5_call(flags, proj, None, prm, 0, reverse=False, final=False)
    ya = _s5_call(flags, proj, ya, prm, 1, reverse=True, final=True)
    yb = _ssd_call(flags, proj, None, prm, reverse=False, final=False)
    yb = _ssd_call(flags, proj, yb, prm, reverse=True, final=True)
    yc = _lru_call(flags, proj, None, prm, 0, reverse=False, final=False)
    yc = _lru_call(flags, proj, yc, prm, 1, reverse=True, final=True)
    yd = _mlstm_call(flags, proj, None, prm, reverse=False, final=False)
    yd = _mlstm_call(flags, proj, yd, prm, reverse=True, final=True)
    return ya, yb, yc, yd


def _to_rows(x):
    b, L, _ = x.shape
    return x.reshape(b // NB, NB, L, D).transpose(0, 2, 1, 3).reshape(b * L, D)


def _from_rows(r, b, L):
    return r.reshape(b // NB, L, NB, D).transpose(0, 2, 1, 3).reshape(b, L, D)


def _trunk(xs, cs, p, depth, moe_ts):
    first, last, gid = [], [], []
    g = 0
    for x in xs:
        b, L, _ = x.shape
        assert b % NB == 0 and L % CH == 0
        n = L // CH
        for _ in range(b // NB):
            first += [1] + [0] * (n - 1)
            last += [0] * (n - 1) + [1]
            gid += [g] * n
            g += 1
    first = jnp.asarray(first, jnp.int32)
    last = jnp.asarray(last, jnp.int32)
    gid = jnp.asarray(gid, jnp.int32)
    flags = (first, last)
    x = jnp.concatenate([_to_rows(x.astype(F32)) for x in xs], axis=0)
    c_all = jnp.concatenate([c.astype(F32) for c in cs], axis=0)
    R = x.shape[0]
    ts = min(moe_ts, R)
    assert R % ts == 0
    for l in range(depth):
        prm = _prep_layer(p, l)
        mod3 = _mod_call(c_all, prm['ada_w'], prm['ada_b']).reshape(g, NB, 6 * D)
        proj = _inproj_call(gid, x, mod3, prm['w_in'])
        ys = _mixers(flags, proj, prm)
        x1, h2, topi, topw = _outproj_call(gid, ys, x, mod3, prm)
        bexp, nblk, idx, wp = _moe_plan(topi, topw, ts)
        y = _moe_call(bexp, nblk, idx, wp, h2, prm, ts)
        x = _ln2_call(gid, x1, y, mod3, prm)
    outs = []
    r0 = 0
    for xin in xs:
        b, L, _ = xin.shape
        outs.append(_from_rows(x[r0:r0 + b * L], b, L).astype(xin.dtype))
        r0 += b * L
    return outs


def kernel(x_prompt, x_sample, c_prompt, c_sample, ada_w, ada_b, w_in, s5_lam_re, s5_lam_im, s5_log_dt, s5_b_re, s5_b_im, s5_c_re, s5_c_im, s5_d, s5_glu_w, s5_glu_b, ssd_conv_w, ssd_conv_b, ssd_a_log, ssd_dt_bias, ssd_d, ssd_norm_g, lru_conv_w, lru_conv_b, lru_wa, lru_ba, lru_wx, lru_bx, lru_lam, ml_gate_b, ml_norm_g, w_out, ln1_g, ln1_b, router_w, router_b, moe_w_gate_up, moe_b_gate_up, moe_w_down, moe_b_down, ln2_g, ln2_b):
    p = {
        'ada_w': ada_w, 'ada_b': ada_b, 'w_in': w_in,
        's5_lam_re': s5_lam_re, 's5_lam_im': s5_lam_im, 's5_log_dt': s5_log_dt,
        's5_b_re': s5_b_re, 's5_b_im': s5_b_im, 's5_c_re': s5_c_re, 's5_c_im': s5_c_im,
        's5_d': s5_d, 's5_glu_w': s5_glu_w, 's5_glu_b': s5_glu_b,
        'ssd_conv_w': ssd_conv_w, 'ssd_conv_b': ssd_conv_b, 'ssd_a_log': ssd_a_log,
        'ssd_dt_bias': ssd_dt_bias, 'ssd_d': ssd_d, 'ssd_norm_g': ssd_norm_g,
        'lru_conv_w': lru_conv_w, 'lru_conv_b': lru_conv_b, 'lru_wa': lru_wa, 'lru_ba': lru_ba,
        'lru_wx': lru_wx, 'lru_bx': lru_bx, 'lru_lam': lru_lam,
        'ml_gate_b': ml_gate_b, 'ml_norm_g': ml_norm_g, 'w_out': w_out,
        'ln1_g': ln1_g, 'ln1_b': ln1_b, 'router_w': router_w, 'router_b': router_b,
        'moe_w_gate_up': moe_w_gate_up, 'moe_b_gate_up': moe_b_gate_up,
        'moe_w_down': moe_w_down, 'moe_b_down': moe_b_down, 'ln2_g': ln2_g, 'ln2_b': ln2_b,
    }
    y_prompt, y_sample = _trunk([x_prompt, x_sample], [c_prompt, c_sample], p, DEPTH, MOE_TS)
    return (y_prompt, y_sample)
```

```python
import functools
import math

import numpy as np
import jax
import jax.numpy as jnp
from jax import lax
from jax.experimental import pallas as pl
from jax.experimental.pallas import tpu as pltpu

F32 = jnp.float32
BF16 = jnp.bfloat16
HIGHEST = lax.Precision.HIGHEST

D = 1024
DEPTH = 4
WG = 512
S5_G, S5_CH, S5_N = 32, 16, 64
SSD_H, SSD_P, SSD_NG, SSD_N = 8, 64, 2, 64
LRU_NB, LRU_BD, LRU_C = 8, 64, 8.0
ML_H, ML_DQK, ML_DV = 4, 64, 128
N_EXP, TOP_K, D_FF = 32, 4, 1024
SWIGLU_LIMIT, SWIGLU_ALPHA = 7.0, 1.702
ALPHA_DN = (2.0 * DEPTH) ** 0.25
LN_EPS, RMS_EPS = 1e-5, 1e-6

LANE = 128
NB = 8
CH = 128
ROWS = CH * NB
PW = 4480
K1_TN = 640
K3_TM = 512
MOE_RB = 256
MOE_UNROLL = 8
MOE_TS = 4096
VMEM_LIMIT = 56 * 1024 * 1024

COL_U, COL_XS, COL_Z, COL_XL, COL_GL, COL_V, COL_O = 0, 1, 2, 3, 4, 5, 6
COL_Q, COL_K, COL_BC = 14, 15, 16
COL_SM = 34
SM_DT, SM_GATE = 0, 16


def _sigmoid(x):
    return 1.0 / (1.0 + jnp.exp(-x))


def _silu(x):
    return x * _sigmoid(x)


def _softplus(x):
    return jnp.maximum(x, 0.0) + jnp.log(1.0 + jnp.exp(-jnp.abs(x)))


def _gelu_tanh(x):
    return 0.5 * x * (1.0 + jnp.tanh(math.sqrt(2.0 / math.pi) * (x + 0.044715 * (x * x * x))))


def _layer_norm(v, g, b):
    mu = jnp.mean(v, axis=-1, keepdims=True)
    vc = v - mu
    var = jnp.mean(vc * vc, axis=-1, keepdims=True)
    return vc * lax.rsqrt(var + LN_EPS) * g + b


def _bdot(a, b):
    return jnp.dot(a.astype(BF16), b.astype(BF16), preferred_element_type=F32)


def _per_slab(x, m):
    n = x.shape[0] // NB
    return (x.reshape(n, NB, x.shape[1]) * m[None]).reshape(x.shape)


def _per_slab_add(x, m):
    n = x.shape[0] // NB
    return (x.reshape(n, NB, x.shape[1]) + m[None]).reshape(x.shape)


def _cparams(sem):
    return pltpu.CompilerParams(dimension_semantics=sem, vmem_limit_bytes=VMEM_LIMIT)


def _mod_kernel(c_ref, w_ref, b_ref, o_ref):
    o_ref[...] = _bdot(_silu(c_ref[...]), w_ref[...]) + b_ref[...]


def _mod_call(c_all, w, b):
    n = c_all.shape[0]
    return pl.pallas_call(
        _mod_kernel,
        grid=(6,),
        in_specs=[pl.BlockSpec((n, D), lambda j: (0, 0)),
                  pl.BlockSpec((D, D), lambda j: (0, j)),
                  pl.BlockSpec((1, D), lambda j: (0, j))],
        out_specs=pl.BlockSpec((n, D), lambda j: (0, j)),
        out_shape=jax.ShapeDtypeStruct((n, 6 * D), F32),
        compiler_params=_cparams(("arbitrary",)),
    )(c_all, w, b)


def _inproj_kernel(gid_ref, x_ref, sc_ref, sh_ref, w_ref, o_ref, h_s):
    @pl.when(pl.program_id(1) == 0)
    def _():
        h = _per_slab_add(_per_slab(x_ref[...], 1.0 + sc_ref[0]), sh_ref[0])
        h_s[...] = h.astype(BF16)

    res = jnp.dot(h_s[...], w_ref[...], preferred_element_type=F32)
    for k in range(K1_TN // LANE):
        o_ref[k] = res[:, LANE * k:LANE * (k + 1)]


def _inproj_call(gid, x, mod3, w_in_p):
    R = x.shape[0]
    nch = R // ROWS
    gs = pltpu.PrefetchScalarGridSpec(
        num_scalar_prefetch=1,
        grid=(nch, PW // K1_TN),
        in_specs=[pl.BlockSpec((ROWS, D), lambda i, j, gid: (i, 0)),
                  pl.BlockSpec((1, NB, D), lambda i, j, gid: (gid[i], 0, 1)),
                  pl.BlockSpec((1, NB, D), lambda i, j, gid: (gid[i], 0, 0)),
                  pl.BlockSpec((D, K1_TN), lambda i, j, gid: (0, j))],
        out_specs=pl.BlockSpec((K1_TN // LANE, ROWS, LANE), lambda i, j, gid: (j, i, 0)),
        scratch_shapes=[pltpu.VMEM((ROWS, D), BF16)],
    )
    return pl.pallas_call(
        _inproj_kernel, grid_spec=gs,
        out_shape=jax.ShapeDtypeStruct((PW // LANE, R, LANE), F32),
        compiler_params=_cparams(("arbitrary", "arbitrary")),
    )(gid, x, mod3, mod3, w_in_p)


def _chunk_idx(nch, reverse):
    return (lambda c: nch - 1 - c) if reverse else (lambda c: c)


def _main_spec(width, col, nch, reverse):
    ci = _chunk_idx(nch, reverse)
    return pl.BlockSpec((width // LANE, ROWS, LANE), lambda c, *_: (col, ci(c), 0))


def _prev_spec(width, col, nch, reverse):
    ci = _chunk_idx(nch, reverse)
    per = ROWS // (2 * NB)
    return pl.BlockSpec((width // LANE, 2 * NB, LANE), lambda c, *_: (col, jnp.maximum(ci(c) * per - 1, 0), 0))


def _next_spec(width, col, nch, reverse):
    ci = _chunk_idx(nch, reverse)
    per = ROWS // NB
    return pl.BlockSpec((width // LANE, NB, LANE),
                        lambda c, *_: (col, jnp.minimum((ci(c) + 1) * per, nch * per - 1), 0))


def _const_spec(shape):
    nd = len(shape)
    return pl.BlockSpec(shape, lambda c, *_: (0,) * nd)


def _cat(ref):
    return jnp.concatenate([ref[k] for k in range(ref.shape[0])], axis=1)


def _put(ref, val):
    for k in range(ref.shape[0]):
        ref[k] = val[:, LANE * k:LANE * (k + 1)]


def _fill_ext(ext_s, k0, x_ref, xp_ref, xn_ref, is_first, is_last):
    for k in range(x_ref.shape[0]):
        ext_s[k0 + k, 0:2 * NB, :] = jnp.where(is_first, 0.0, xp_ref[k])
        ext_s[k0 + k, 2 * NB:2 * NB + ROWS, :] = x_ref[k]
        ext_s[k0 + k, 2 * NB + ROWS:3 * NB + ROWS, :] = jnp.where(is_last, 0.0, xn_ref[k])


def _dwconv(ext_s, k, w_ref, b_ref):
    sl = slice(LANE * k, LANE * (k + 1))
    out = b_ref[:, sl] + w_ref[0:1, sl] * ext_s[k, 0:ROWS, :]
    for j in range(1, 4):
        out = out + w_ref[j:j + 1, sl] * ext_s[k, j * NB:j * NB + ROWS, :]
    return out


def _tri(reverse):
    r = lax.broadcasted_iota(jnp.int32, (CH, CH), 0)
    c = lax.broadcasted_iota(jnp.int32, (CH, CH), 1)
    return (c >= r) if reverse else (c <= r)


def _s5_kernel(*refs, nch, reverse, final):
    first_ref, last_ref = refs[0], refs[1]
    if final:
        (u_ref, yp_ref, bt_ref, lr_ref, li_ref, cm_ref, d_ref, gw_ref, gb_ref,
         o_ref, bu_s, st_s, carry_s) = refs[2:]
    else:
        u_ref, bt_ref, lr_ref, li_ref, cm_ref, o_ref, bu_s, st_s, carry_s = refs[2:]
    c = pl.program_id(0)
    cc = nch - 1 - c if reverse else c
    start = last_ref[cc] if reverse else first_ref[cc]

    @pl.when(start == 1)
    def _():
        carry_s[...] = jnp.zeros_like(carry_s)

    half = WG
    for j in range(4):
        bu_s[...] = jnp.dot(u_ref[j].astype(BF16), bt_ref[j], preferred_element_type=F32)
        lr = jnp.broadcast_to(lr_ref[j], (NB, half))
        li = jnp.broadcast_to(li_ref[j], (NB, half))

        def step(i, carry, lr=lr, li=li):
            sr, si = carry
            t = CH - 1 - i if reverse else i
            r0 = pl.multiple_of(t * NB, NB)
            br = bu_s[pl.ds(r0, NB), 0:half]
            bi = bu_s[pl.ds(r0, NB), half:2 * half]
            nr = lr * sr - li * si + br
            ni = lr * si + li * sr + bi
            st_s[pl.ds(r0, NB), 0:half] = nr
            st_s[pl.ds(r0, NB), half:2 * half] = ni
            return nr, ni

        lax.fori_loop(0, CH, step, (carry_s[j, :, 0:half], carry_s[j, :, half:2 * half]), unroll=8)
        e0 = (CH - 1) * NB if reverse else 0
        carry_s[j] = st_s[e0:e0 + NB, :]
        o_ref[j] = jnp.dot(st_s[...].astype(BF16), cm_ref[j], preferred_element_type=F32)

    if final:
        y = _cat(o_ref) + _cat(yp_ref) + d_ref[...] * _cat(u_ref)
        g = _gelu_tanh(y)
        _put(o_ref, g * _sigmoid(_bdot(g, gw_ref[...]) + gb_ref[...]))


def _s5_call(flags, proj, yprev, prm, d, *, reverse, final):
    first, last = flags
    R = proj.shape[1]
    nch = R // ROWS
    in_specs = [_main_spec(WG, COL_U, nch, reverse)]
    args = [proj]
    if final:
        in_specs.append(_main_spec(WG, 0, nch, reverse))
        args.append(yprev)
    in_specs += [_const_spec((4, 128, 2 * WG)), _const_spec((4, 1, WG)), _const_spec((4, 1, WG)),
                 _const_spec((4, 2 * WG, 128))]
    args += [prm['s5_bt'], prm['s5_lr'][d], prm['s5_li'][d], prm['s5_cm'][d]]
    if final:
        in_specs += [_const_spec((1, WG)), _const_spec((WG, WG)), _const_spec((1, WG))]
        args += [prm['s5_d'], prm['s5_glu_w'], prm['s5_glu_b']]
    gs = pltpu.PrefetchScalarGridSpec(
        num_scalar_prefetch=2, grid=(nch,), in_specs=in_specs,
        out_specs=_main_spec(WG, 0, nch, reverse),
        scratch_shapes=[pltpu.VMEM((ROWS, 2 * WG), F32), pltpu.VMEM((ROWS, 2 * WG), F32),
                        pltpu.VMEM((4, NB, 2 * WG), F32)],
    )
    return pl.pallas_call(
        functools.partial(_s5_kernel, nch=nch, reverse=reverse, final=final), grid_spec=gs,
        out_shape=jax.ShapeDtypeStruct((WG // LANE, R, LANE), F32),
        compiler_params=_cparams(("arbitrary",)),
    )(first, last, *args)


def _lru_kernel(*refs, nch, reverse, final):
    first_ref, last_ref = refs[0], refs[1]
    if final:
        (x_ref, xp_ref, xn_ref, gate_ref, hp_ref, cw_ref, cb_ref, wa_ref, ba_ref, wx_ref, bx_ref, sp_ref,
         o_ref, ext_s, a_s, inp_s, carry_s) = refs[2:]
    else:
        (x_ref, xp_ref, xn_ref, cw_ref, cb_ref, wa_ref, ba_ref, wx_ref, bx_ref, sp_ref,
         o_ref, ext_s, a_s, inp_s, carry_s) = refs[2:]
    c = pl.program_id(0)
    cc = nch - 1 - c if reverse else c
    is_first = first_ref[cc] == 1
    is_last = last_ref[cc] == 1
    start = is_last if reverse else is_first

    @pl.when(start)
    def _():
        carry_s[...] = jnp.zeros_like(carry_s)

    _fill_ext(ext_s, 0, x_ref, xp_ref, xn_ref, is_first, is_last)
    xc = jnp.concatenate([_dwconv(ext_s, k, cw_ref, cb_ref) for k in range(WG // LANE)], axis=1)
    xb = xc.astype(BF16)
    r = _sigmoid(jnp.dot(xb, wa_ref[...], preferred_element_type=F32) + ba_ref[...])
    i = _sigmoid(jnp.dot(xb, wx_ref[...], preferred_element_type=F32) + bx_ref[...])
    log_a = -LRU_C * r * sp_ref[...]
    a_s[...] = jnp.exp(log_a)
    inp_s[...] = jnp.sqrt(1.0 - jnp.exp(2.0 * log_a)) * (i * xc)

    def step(k, h):
        t = CH - 1 - k if reverse else k
        r0 = pl.multiple_of(t * NB, NB)
        h = a_s[pl.ds(r0, NB), :] * h + inp_s[pl.ds(r0, NB), :]
        for k in range(WG // LANE):
            o_ref[k, pl.ds(r0, NB), :] = h[:, LANE * k:LANE * (k + 1)]
        return h

    carry_s[...] = lax.fori_loop(0, CH, step, carry_s[...], unroll=8)
    if final:
        for k in range(WG // LANE):
            o_ref[k] = (o_ref[k] + hp_ref[k]) * _gelu_tanh(gate_ref[k])


def _lru_call(flags, proj, hprev, prm, d, *, reverse, final):
    first, last = flags
    R = proj.shape[1]
    nch = R // ROWS
    in_specs = [_main_spec(WG, COL_XL, nch, reverse), _prev_spec(WG, COL_XL, nch, reverse),
                _next_spec(WG, COL_XL, nch, reverse)]
    args = [proj, proj, proj]
    if final:
        in_specs += [_main_spec(WG, COL_GL, nch, reverse), _main_spec(WG, 0, nch, reverse)]
        args += [proj, hprev]
    in_specs += [_const_spec((4, WG)), _const_spec((1, WG)), _const_spec((WG, WG)), _const_spec((1, WG)),
                 _const_spec((WG, WG)), _const_spec((1, WG)), _const_spec((1, WG))]
    args += [prm['lru_conv_w'], prm['lru_conv_b'], prm['lru_wa'][d], prm['lru_ba'][d], prm['lru_wx'][d],
             prm['lru_bx'][d], prm['lru_sp'][d]]
    gs = pltpu.PrefetchScalarGridSpec(
        num_scalar_prefetch=2, grid=(nch,), in_specs=in_specs,
        out_specs=_main_spec(WG, 0, nch, reverse),
        scratch_shapes=[pltpu.VMEM((WG // LANE, ROWS + 3 * NB, LANE), F32), pltpu.VMEM((ROWS, WG), F32),
                        pltpu.VMEM((ROWS, WG), F32), pltpu.VMEM((NB, WG), F32)],
    )
    return pl.pallas_call(
        functools.partial(_lru_kernel, nch=nch, reverse=reverse, final=final), grid_spec=gs,
        out_shape=jax.ShapeDtypeStruct((WG // LANE, R, LANE), F32),
        compiler_params=_cparams(("arbitrary",)),
    )(first, last, *args)


def _ssd_kernel(*refs, nch, reverse, final):
    first_ref, last_ref = refs[0], refs[1]
    if final:
        (x_ref, xp_ref, xn_ref, bc_ref, bcp_ref, bcn_ref, sm_ref, z_ref, yp_ref,
         cw_ref, cb_ref, dtb_ref, arow_ref, drow_ref, ng_ref,
         o_ref, ext_s, xd_s, smd_s, yd_s, st_s) = refs[2:]
    else:
        (x_ref, xp_ref, xn_ref, bc_ref, bcp_ref, bcn_ref, sm_ref,
         cw_ref, cb_ref, dtb_ref, arow_ref,
         o_ref, ext_s, xd_s, smd_s, yd_s, st_s) = refs[2:]
    c = pl.program_id(0)
    cc = nch - 1 - c if reverse else c
    is_first = first_ref[cc] == 1
    is_last = last_ref[cc] == 1
    start = is_last if reverse else is_first

    @pl.when(start)
    def _():
        st_s[...] = jnp.zeros_like(st_s)

    CW = WG + 2 * SSD_NG * SSD_N
    _fill_ext(ext_s, 0, x_ref, xp_ref, xn_ref, is_first, is_last)
    _fill_ext(ext_s, WG // LANE, bc_ref, bcp_ref, bcn_ref, is_first, is_last)
    for k in range(CW // LANE):
        ext_s[k, 0:ROWS, :] = _silu(_dwconv(ext_s, k, cw_ref, cb_ref))
    for b in range(NB):
        for k in range(CW // LANE):
            xd_s[b, :, LANE * k:LANE * (k + 1)] = ext_s[k, pl.ds(b, CH, stride=NB), :]
        smd_s[b] = sm_ref[0, pl.ds(b, CH, stride=NB), :]

    mask = _tri(reverse)
    tri = jnp.where(mask, 1.0, 0.0)
    doff = SM_DT + (SSD_H if reverse else 0)
    edge = 0 if reverse else CH - 1
    NS = SSD_N

    def per_b(b, _):
        xb = xd_s[b]
        xs = xb[:, 0:WG]
        bm = xb[:, WG:WG + SSD_NG * NS]
        cm = xb[:, WG + SSD_NG * NS:CW]
        dtf = _softplus(smd_s[b] + dtb_ref[...])
        af = dtf * arow_ref[...]
        cs = jnp.dot(tri, af, precision=HIGHEST, preferred_element_type=F32)
        cs_t = cs.T
        dt_t = dtf.T
        bm_t = bm.T
        gmat = [_bdot(cm[:, NS * g:NS * (g + 1)], bm_t[NS * g:NS * (g + 1), :]) for g in range(SSD_NG)]
        for h in range(SSD_H):
            ci = doff + h
            g = h // (SSD_H // SSD_NG)
            col = cs[:, ci:ci + 1]
            row = cs_t[ci:ci + 1, :]
            lmat = jnp.where(mask, jnp.exp(jnp.minimum(col - row, 0.0)), 0.0)
            m = gmat[g] * lmat * dt_t[ci:ci + 1, :]
            xh = xs[:, SSD_P * h:SSD_P * (h + 1)]
            s_prev = st_s[b, h]
            y = _bdot(m, xh) + jnp.exp(col) * _bdot(cm[:, NS * g:NS * (g + 1)], s_prev)
            tot = cs[edge:edge + 1, ci:ci + 1]
            wcol = jnp.exp(tot - col) * dtf[:, ci:ci + 1]
            st_s[b, h] = jnp.exp(tot) * s_prev + _bdot(bm_t[NS * g:NS * (g + 1), :], xh * wcol)
            yd_s[b, :, SSD_P * h:SSD_P * (h + 1)] = y
        return 0

    lax.fori_loop(0, NB, per_b, 0)
    for b in range(NB):
        for k in range(WG // LANE):
            o_ref[k, pl.ds(b, CH, stride=NB), :] = yd_s[b, :, LANE * k:LANE * (k + 1)]
    if final:
        xs_all = jnp.concatenate([ext_s[k, 0:ROWS, :] for k in range(WG // LANE)], axis=1)
        y = (_cat(o_ref) + _cat(yp_ref) + drow_ref[...] * xs_all) * _silu(_cat(z_ref))
        ms = jnp.mean(y * y, axis=-1, keepdims=True)
        _put(o_ref, y * lax.rsqrt(ms + RMS_EPS) * ng_ref[...])


def _ssd_call(flags, proj, yprev, prm, *, reverse, final):
    first, last = flags
    R = proj.shape[1]
    nch = R // ROWS
    CW = WG + 2 * SSD_NG * SSD_N
    in_specs = [_main_spec(WG, COL_XS, nch, reverse), _prev_spec(WG, COL_XS, nch, reverse),
                _next_spec(WG, COL_XS, nch, reverse),
                _main_spec(256, COL_BC, nch, reverse), _prev_spec(256, COL_BC, nch, reverse),
                _next_spec(256, COL_BC, nch, reverse),
                _main_spec(128, COL_SM, nch, reverse)]
    args = [proj] * 7
    if final:
        in_specs += [_main_spec(WG, COL_Z, nch, reverse), _main_spec(WG, 0, nch, reverse)]
        args += [proj, yprev]
    in_specs += [_const_spec((4, CW)), _const_spec((1, CW)), _const_spec((1, 128)), _const_spec((1, 128))]
    args += [prm['ssd_conv_w'], prm['ssd_conv_b'], prm['ssd_dtb'], prm['ssd_arow']]
    if final:
        in_specs += [_const_spec((1, WG)), _const_spec((1, WG))]
        args += [prm['ssd_drow'], prm['ssd_norm_g']]
    gs = pltpu.PrefetchScalarGridSpec(
        num_scalar_prefetch=2, grid=(nch,), in_specs=in_specs,
        out_specs=_main_spec(WG, 0, nch, reverse),
        scratch_shapes=[pltpu.VMEM((CW // LANE, ROWS + 3 * NB, LANE), F32), pltpu.VMEM((NB, CH, CW), F32),
                        pltpu.VMEM((NB, CH, 128), F32), pltpu.VMEM((NB, CH, WG), F32),
                        pltpu.VMEM((NB, SSD_H, SSD_N, SSD_P), F32)],
    )
    return pl.pallas_call(
        functools.partial(_ssd_kernel, nch=nch, reverse=reverse, final=final), grid_spec=gs,
        out_shape=jax.ShapeDtypeStruct((WG // LANE, R, LANE), F32),
        compiler_params=_cparams(("arbitrary",)),
    )(first, last, *args)


NEG = -1e30


def _mlstm_kernel(*refs, nch, reverse, final):
    first_ref, last_ref = refs[0], refs[1]
    if final:
        (q_ref, k_ref, v_ref, sm_ref, og_ref, hp_ref, gb_ref, ng_ref,
         o_ref, qd_s, kd_s, vd_s, smd_s, hd_s, c_s, n_s, m_s) = refs[2:]
    else:
        (q_ref, k_ref, v_ref, sm_ref, gb_ref,
         o_ref, qd_s, kd_s, vd_s, smd_s, hd_s, c_s, n_s, m_s) = refs[2:]
    c = pl.program_id(0)
    cc = nch - 1 - c if reverse else c
    start = (last_ref[cc] if reverse else first_ref[cc]) == 1

    @pl.when(start)
    def _():
        c_s[...] = jnp.zeros_like(c_s)
        n_s[...] = jnp.zeros_like(n_s)
        m_s[...] = jnp.zeros_like(m_s)

    for b in range(NB):
        for src, dst in ((q_ref, qd_s), (k_ref, kd_s), (v_ref, vd_s)):
            for k in range(src.shape[0]):
                dst[b, :, LANE * k:LANE * (k + 1)] = src[k, pl.ds(b, CH, stride=NB), :]
        smd_s[b] = sm_ref[0, pl.ds(b, CH, stride=NB), :]

    mask = _tri(reverse)
    tri = jnp.where(mask, 1.0, 0.0)
    goff = SM_GATE + (2 * ML_H if reverse else 0)
    edge = 0 if reverse else CH - 1

    def per_b(b, _):
        sm = smd_s[b] + gb_ref[...]
        logf = -_softplus(-sm)
        fc_all = jnp.dot(tri, logf, precision=HIGHEST, preferred_element_type=F32)
        fr_all = fc_all.T
        sm_t = sm.T
        qb = qd_s[b] * (ML_DQK ** -0.5)
        kb = kd_s[b]
        kb_t = kb.T
        vb = vd_s[b]
        for h in range(ML_H):
            ci = goff + h
            cf = goff + ML_H + h
            fc = fc_all[:, cf:cf + 1]
            fr = fr_all[cf:cf + 1, :]
            ir = sm_t[ci:ci + 1, :]
            ic = sm[:, ci:ci + 1]
            m_prev = m_s[b, h]
            dlog = jnp.where(mask, fc - fr + ir, NEG)
            inter = fc + m_prev
            m_t = jnp.maximum(jnp.max(dlog, axis=1, keepdims=True), inter)
            wts = jnp.exp(dlog - m_t)
            w_inter = jnp.exp(inter - m_t)
            qh = qb[:, ML_DQK * h:ML_DQK * (h + 1)]
            kh = kb[:, ML_DQK * h:ML_DQK * (h + 1)]
            kh_t = kb_t[ML_DQK * h:ML_DQK * (h + 1), :]
            vh = vb[:, ML_DV * h:ML_DV * (h + 1)]
            ct_prev = c_s[b, h]
            n_prev = n_s[b, h]
            s = _bdot(qh, kh_t) * wts
            num = _bdot(s, vh) + w_inter * _bdot(qh, ct_prev)
            den = jnp.sum(s, axis=1, keepdims=True) + w_inter * jnp.sum(qh * n_prev, axis=1, keepdims=True)
            hd_s[b, :, ML_DV * h:ML_DV * (h + 1)] = num / jnp.maximum(jnp.abs(den), jnp.exp(-m_t))
            f_end = fc_all[edge:edge + 1, cf:cf + 1]
            w_log = f_end - fc + ic
            m_new = jnp.maximum(f_end + m_prev, jnp.max(w_log, axis=0, keepdims=True))
            w_s = jnp.exp(w_log - m_new)
            scale = jnp.exp(f_end + m_prev - m_new)
            c_s[b, h] = scale * ct_prev + _bdot(kh_t, vh * w_s)
            n_s[b, h] = scale * n_prev + jnp.sum(kh * w_s, axis=0, keepdims=True)
            m_s[b, h] = m_new
        return 0

    lax.fori_loop(0, NB, per_b, 0)
    for b in range(NB):
        for k in range(WG // LANE):
            o_ref[k, pl.ds(b, CH, stride=NB), :] = hd_s[b, :, LANE * k:LANE * (k + 1)]
    if final:
        assert ML_DV == LANE
        for h in range(ML_H):
            hh = o_ref[h] + hp_ref[h]
            ms = jnp.mean(hh * hh, axis=-1, keepdims=True)
            o_ref[h] = hh * lax.rsqrt(ms + RMS_EPS) * ng_ref[:, LANE * h:LANE * (h + 1)] * _sigmoid(og_ref[h])


def _mlstm_call(flags, proj, hprev, prm, *, reverse, final):
    first, last = flags
    R = proj.shape[1]
    nch = R // ROWS
    in_specs = [_main_spec(256, COL_Q, nch, reverse), _main_spec(256, COL_K, nch, reverse),
                _main_spec(WG, COL_V, nch, reverse), _main_spec(128, COL_SM, nch, reverse)]
    args = [proj] * 4
    if final:
        in_specs += [_main_spec(WG, COL_O, nch, reverse), _main_spec(WG, 0, nch, reverse)]
        args += [proj, hprev]
    in_specs += [_const_spec((1, 128))]
    args += [prm['ml_gb']]
    if final:
        in_specs += [_const_spec((1, WG))]
        args += [prm['ml_norm_g']]
    gs = pltpu.PrefetchScalarGridSpec(
        num_scalar_prefetch=2, grid=(nch,), in_specs=in_specs,
        out_specs=_main_spec(WG, 0, nch, reverse),
        scratch_shapes=[pltpu.VMEM((NB, CH, 256), F32), pltpu.VMEM((NB, CH, 256), F32),
                        pltpu.VMEM((NB, CH, WG), F32), pltpu.VMEM((NB, CH, 128), F32),
                        pltpu.VMEM((NB, CH, WG), F32),
                        pltpu.VMEM((NB, ML_H, ML_DQK, ML_DV), F32), pltpu.VMEM((NB, ML_H, 1, ML_DQK), F32),
                        pltpu.VMEM((NB, ML_H, 1, 1), F32)],
    )
    return pl.pallas_call(
        functools.partial(_mlstm_kernel, nch=nch, reverse=reverse, final=final), grid_spec=gs,
        out_shape=jax.ShapeDtypeStruct((WG // LANE, R, LANE), F32),
        compiler_params=_cparams(("arbitrary",)),
    )(first, last, *args)


def _outproj_kernel(gid_ref, ya_ref, yb_ref, yc_ref, yd_ref, x_ref, g1_ref, sc_ref, sh_ref, wo_ref,
                    lg_ref, lb_ref, rw_ref, rb_ref, x1_ref, h2_ref, ti_ref, tw_ref):
    o = _bdot(_cat(ya_ref), wo_ref[0:WG, :])
    o = o + _bdot(_cat(yb_ref), wo_ref[WG:2 * WG, :])
    o = o + _bdot(_cat(yc_ref), wo_ref[2 * WG:3 * WG, :])
    o = o + _bdot(_cat(yd_ref), wo_ref[3 * WG:4 * WG, :])
    v = ALPHA_DN * x_ref[...] + _per_slab(o, g1_ref[0])
    x1 = _layer_norm(v, lg_ref[...], lb_ref[...])
    x1_ref[...] = x1
    h2 = _per_slab_add(_per_slab(x1, 1.0 + sc_ref[0]), sh_ref[0])
    bits = lax.bitcast_convert_type(h2.astype(BF16).astype(F32), jnp.uint32)
    h2_ref[...] = (bits[:, 0:D // 2] & jnp.uint32(0xFFFF0000)) | (bits[:, D // 2:D] >> jnp.uint32(16))
    logit = _bdot(h2, rw_ref[...]) + rb_ref[...]
    tm = logit.shape[0]
    lane = lax.broadcasted_iota(jnp.int32, (tm, N_EXP), 1)
    lane8 = lax.broadcasted_iota(jnp.int32, (tm, 8), 1)
    idx8 = jnp.zeros((tm, 8), jnp.int32)
    val8 = jnp.zeros((tm, 8), F32)
    top0 = None
    den = None
    for k in range(TOP_K):
        mx = jnp.max(logit, axis=1, keepdims=True)
        sel = jnp.min(jnp.where(logit == mx, lane, N_EXP), axis=1, keepdims=True)
        if k == 0:
            top0 = mx
        ek = jnp.exp(mx - top0)
        den = ek if k == 0 else den + ek
        idx8 = jnp.where(lane8 == k, sel, idx8)
        val8 = jnp.where(lane8 == k, ek, val8)
        logit = jnp.where(lane == sel, -jnp.inf, logit)
    ti_ref[...] = idx8
    tw_ref[...] = val8 / den


def _outproj_call(gid3, ys, x, mod3, prm):
    R = x.shape[0]
    per = ROWS // K3_TM
    rspec = lambda w: pl.BlockSpec((K3_TM, w), lambda i, gid: (i, 0))
    yspec = pl.BlockSpec((WG // LANE, K3_TM, LANE), lambda i, gid: (0, i, 0))
    mspec = lambda k: pl.BlockSpec((1, NB, D), lambda i, gid: (gid[i // per], 0, k))
    cspec = lambda shape: pl.BlockSpec(shape, lambda i, gid: (0,) * len(shape))
    gs = pltpu.PrefetchScalarGridSpec(
        num_scalar_prefetch=1, grid=(R // K3_TM,),
        in_specs=[yspec, yspec, yspec, yspec, rspec(D), mspec(2), mspec(4), mspec(3),
                  cspec((4 * WG, D)), cspec((1, D)), cspec((1, D)), cspec((D, N_EXP)), cspec((1, N_EXP))],
        out_specs=[rspec(D), rspec(D // 2), rspec(8), rspec(8)],
    )
    return pl.pallas_call(
        _outproj_kernel, grid_spec=gs,
        out_shape=[jax.ShapeDtypeStruct((R, D), F32), jax.ShapeDtypeStruct((R, D // 2), jnp.uint32),
                   jax.ShapeDtypeStruct((R, 8), jnp.int32), jax.ShapeDtypeStruct((R, 8), F32)],
        compiler_params=_cparams(("arbitrary",)),
    )(gid3, *ys, x, mod3, mod3, mod3, prm['w_out'], prm['ln1_g'], prm['ln1_b'], prm['router_w'], prm['router_b'])


def _moe_gather(idx_ref, h2_ref, dst, blk):
    s0 = (blk + 1) * MOE_RB
    for r in range(MOE_RB):
        tok = idx_ref[0, 0, s0 + r] & 0xFFFF
        dst[r:r + 1, :] = h2_ref[pl.ds(tok, 1), :]


def _moe_scatter(idx_ref, gw_ref, o_ref, ysrc, blk):
    s0 = (blk + 1) * MOE_RB
    for q in range(MOE_RB // MOE_UNROLL):
        toks, vals = [], []
        for j in range(MOE_UNROLL):
            r = q * MOE_UNROLL + j
            tok = idx_ref[0, 0, s0 + r] >> 16
            w = gw_ref[0, 0, s0 + r]
            toks.append(tok)
            vals.append(o_ref[0, pl.ds(tok, 1), :] + w * ysrc[r:r + 1, :])
        for tok, val in zip(toks, vals):
            o_ref[0, pl.ds(tok, 1), :] = val


def _moe_ffn(lhs, wgu_ref, wd_ref, bgu_ref, bd_ref, ydst):
    pk = lhs[...]
    lo = lax.bitcast_convert_type(pk << jnp.uint32(16), F32)
    hi = lax.bitcast_convert_type(pk & jnp.uint32(0xFFFF0000), F32)
    xb = jnp.concatenate([hi, lo], axis=1).astype(BF16)
    gu = jnp.dot(xb, wgu_ref[0], preferred_element_type=F32) + bgu_ref[0]
    g = jnp.minimum(gu[:, 0:D_FF], SWIGLU_LIMIT)
    u = jnp.clip(gu[:, D_FF:2 * D_FF], -SWIGLU_LIMIT, SWIGLU_LIMIT)
    hdn = (u + 1.0) * g * _sigmoid(SWIGLU_ALPHA * g)
    ydst[...] = jnp.dot(hdn.astype(BF16), wd_ref[0], preferred_element_type=F32) + bd_ref[0]


def _moe_kernel(bexp_ref, nblk_ref, idx_ref, gw_ref, h2_ref, wgu_ref, wd_ref, bgu_ref, bd_ref,
                o_ref, la_s, lb_s, ya_s, yb_s):
    i = pl.program_id(0)
    s = pl.program_id(1)

    @pl.when(s == 0)
    def _():
        o_ref[...] = jnp.zeros_like(o_ref)
        yb_s[...] = jnp.zeros_like(yb_s)
        _moe_gather(idx_ref, h2_ref, la_s, s)

    def step(cur_l, nxt_l, cur_y, prv_y):
        _moe_scatter(idx_ref, gw_ref, o_ref, prv_y, s - 1)
        _moe_gather(idx_ref, h2_ref, nxt_l, s + 1)
        _moe_ffn(cur_l, wgu_ref, wd_ref, bgu_ref, bd_ref, cur_y)

    active = s <= nblk_ref[i]

    @pl.when(active & (s % 2 == 0))
    def _():
        step(la_s, lb_s, ya_s, yb_s)

    @pl.when(active & (s % 2 == 1))
    def _():
        step(lb_s, la_s, yb_s, ya_s)


MOE_RUN = MOE_RB


def _moe_dims(ts):
    nbmax = ts * TOP_K // MOE_RB + N_EXP
    nsteps = nbmax + 1
    nba = nbmax + 3
    return nbmax, nsteps, nba


def _moe_call(bexp, nblk, idx, wp, h2p, prm, ts):
    R = h2p.shape[0]
    nst = R // ts
    nbmax, nsteps, nba = _moe_dims(ts)
    one = pl.Buffered(1)
    wspec = lambda shape: pl.BlockSpec(shape, lambda i, s, bexp, nblk: (bexp[i * nsteps + s], 0, 0))
    gs = pltpu.PrefetchScalarGridSpec(
        num_scalar_prefetch=2, grid=(nst, nsteps),
        in_specs=[pl.BlockSpec((1, 1, nba * MOE_RB), lambda i, s, *_: (i, 0, 0), memory_space=pltpu.SMEM),
                  pl.BlockSpec((1, 1, nba * MOE_RB), lambda i, s, *_: (i, 0, 0), memory_space=pltpu.SMEM),
                  pl.BlockSpec((ts, D // 2), lambda i, s, *_: (i, 0), pipeline_mode=one),
                  wspec((1, D, 2 * D_FF)), wspec((1, D_FF, D)), wspec((1, 1, 2 * D_FF)), wspec((1, 1, D))],
        out_specs=pl.BlockSpec((1, ts + MOE_UNROLL, D), lambda i, s, *_: (i, 0, 0), pipeline_mode=one),
        scratch_shapes=[pltpu.VMEM((MOE_RB, D // 2), jnp.uint32), pltpu.VMEM((MOE_RB, D // 2), jnp.uint32),
                        pltpu.VMEM((MOE_RB, D), F32), pltpu.VMEM((MOE_RB, D), F32)],
    )
    return pl.pallas_call(
        _moe_kernel, grid_spec=gs,
        out_shape=jax.ShapeDtypeStruct((nst, ts + MOE_UNROLL, D), F32),
        compiler_params=_cparams(("arbitrary", "arbitrary")),
    )(bexp, nblk, idx, wp, h2p, prm['moe_wgu'], prm['moe_wd'], prm['moe_bgu'], prm['moe_bd'])


def _moe_plan(topi, topw, ts):
    R = topi.shape[0]
    nst = R // ts
    na = ts * TOP_K
    nbmax, nsteps, nba = _moe_dims(ts)
    e_flat = topi[:, :TOP_K].reshape(nst, na)
    w_flat = topw[:, :TOP_K].reshape(nst, na)
    ex = jnp.arange(N_EXP, dtype=jnp.int32)
    counts = jnp.sum(e_flat[:, :, None] == ex[None, None, :], axis=1).astype(jnp.int32)
    padded = (counts + MOE_RUN - 1) // MOE_RUN * MOE_RUN
    pad_end = jnp.cumsum(padded, axis=1)
    total = pad_end[:, -1:]
    nblk = (total[:, 0] // MOE_RB).astype(jnp.int32)
    blk0 = jnp.minimum(jnp.arange(nsteps, dtype=jnp.int32)[None, :] * MOE_RUN, total - MOE_RUN)
    bexp = jnp.minimum(jnp.sum(blk0[:, :, None] >= pad_end[:, None, :], axis=2), N_EXP - 1).astype(jnp.int32)
    m = na + MOE_RUN
    unused = N_EXP * m
    key_real = e_flat * m + jnp.arange(na, dtype=jnp.int32)[None, :]
    d = jnp.arange(MOE_RUN - 1, dtype=jnp.int32)[None, None, :]
    key_pad = jnp.where(d < (padded - counts)[:, :, None], ex[None, :, None] * m + na + d, unused)
    keys = jnp.concatenate([key_real, key_pad.reshape(nst, -1)], axis=1)
    vals = jnp.concatenate([w_flat, jnp.zeros((nst, N_EXP * (MOE_RUN - 1)), F32)], axis=1)
    keys, vals = lax.sort((keys, vals), dimension=1, num_keys=1)
    n_tail = nba * MOE_RB - MOE_RB - keys.shape[1]
    assert n_tail >= 0
    keys = jnp.concatenate([jnp.full((nst, MOE_RB), unused, jnp.int32), keys,
                            jnp.full((nst, n_tail), unused, jnp.int32)], axis=1)
    wp = jnp.concatenate([jnp.zeros((nst, MOE_RB), F32), vals, jnp.zeros((nst, n_tail), F32)], axis=1)
    j = keys % m
    real = (keys < unused) & (j < na)
    tok = j // TOP_K
    spare = ts + (jnp.arange(nba * MOE_RB, dtype=jnp.int32)[None, :] & (MOE_UNROLL - 1))
    idx = jnp.where(real, tok | (tok << 16), spare << 16).astype(jnp.int32)
    wp = jnp.where(real, wp, 0.0)
    return bexp.reshape(-1), nblk, idx.reshape(nst, 1, -1), wp.reshape(nst, 1, -1)


def _wperm_kernel(w_ref, p_ref, o_ref):
    o_ref[0] = jnp.dot(w_ref[0].astype(BF16), p_ref[...], preferred_element_type=F32).astype(BF16)


def _wperm_call(wgu):
    rows = lax.broadcasted_iota(jnp.int32, (2 * D_FF, 2 * D_FF), 0)
    cols = lax.broadcasted_iota(jnp.int32, (2 * D_FF, 2 * D_FF), 1)
    perm = (rows == 2 * (cols % D_FF) + cols // D_FF).astype(BF16)
    return pl.pallas_call(
        _wperm_kernel,
        grid=(N_EXP, 2),
        in_specs=[pl.BlockSpec((1, D, 2 * D_FF), lambda e, h: (e, 0, 0)),
                  pl.BlockSpec((2 * D_FF, D_FF), lambda e, h: (0, h))],
        out_specs=pl.BlockSpec((1, D, D_FF), lambda e, h: (e, 0, h)),
        out_shape=jax.ShapeDtypeStruct((N_EXP, D, 2 * D_FF), BF16),
        compiler_params=_cparams(("arbitrary", "arbitrary")),
    )(wgu, perm)


def _ln2_kernel(gid_ref, x1_ref, y_ref, g2_ref, lg_ref, lb_ref, o_ref):
    v = ALPHA_DN * x1_ref[...] + _per_slab(y_ref[0], g2_ref[0])
    o_ref[...] = _layer_norm(v, lg_ref[...], lb_ref[...])


def _ln2_call(gid, x1, y, mod3, prm):
    R = x1.shape[0]
    per = (y.shape[1] - MOE_UNROLL) // ROWS
    rspec = pl.BlockSpec((ROWS, D), lambda i, gid: (i, 0))
    yspec = pl.BlockSpec((1, ROWS, D), lambda i, gid: (i // per, i % per, 0))
    cspec = pl.BlockSpec((1, D), lambda i, gid: (0, 0))
    gs = pltpu.PrefetchScalarGridSpec(
        num_scalar_prefetch=1, grid=(R // ROWS,),
        in_specs=[rspec, yspec, pl.BlockSpec((1, NB, D), lambda i, gid: (gid[i], 0, 5)), cspec, cspec],
        out_specs=rspec,
    )
    return pl.pallas_call(
        _ln2_kernel, grid_spec=gs, out_shape=jax.ShapeDtypeStruct((R, D), F32),
        compiler_params=_cparams(("arbitrary",)),
    )(gid, x1, y, mod3, prm['ln2_g'], prm['ln2_b'])


def _block_diag(blocks):
    n, r, c = blocks.shape
    eye = jnp.eye(n, dtype=blocks.dtype)
    return jnp.einsum('nrc,nm->nrmc', blocks, eye).reshape(n * r, n * c)


def _prep_layer(p, l):
    prm = {}
    w = p['w_in'][l]
    pieces = [w[:, 0:512], w[:, 512:1024], w[:, 1280:1792], w[:, 1808:2320], w[:, 2320:2832], w[:, 3344:3856],
              w[:, 3856:4368], w[:, 2832:3088], w[:, 3088:3344], w[:, 1024:1152], w[:, 1152:1280],
              w[:, 1792:1808], w[:, 4368:4384], jnp.zeros((D, 96), w.dtype)]
    prm['w_in'] = jnp.concatenate(pieces, axis=1).astype(BF16)
    b_re, b_im = p['s5_b_re'][l], p['s5_b_im'][l]
    bt = []
    for j in range(4):
        sl = slice(8 * j, 8 * j + 8)
        bre = _block_diag(jnp.swapaxes(b_re[sl], 1, 2))
        bim = _block_diag(jnp.swapaxes(b_im[sl], 1, 2))
        bt.append(jnp.concatenate([bre, bim], axis=1))
    prm['s5_bt'] = jnp.stack(bt).astype(BF16)
    lam = lax.complex(p['s5_lam_re'][l], p['s5_lam_im'][l])
    dt = jnp.exp(p['s5_log_dt'][l])[:, :, None]
    lam_bar = jnp.exp(lam * dt)
    f = (lam_bar - 1.0) / lam
    prm['s5_lr'] = jnp.real(lam_bar).reshape(2, 4, 1, WG)
    prm['s5_li'] = jnp.imag(lam_bar).reshape(2, 4, 1, WG)
    c_c = lax.complex(p['s5_c_re'][l], p['s5_c_im'][l])
    cms = []
    for d in range(2):
        e = c_c * f[d][:, None, :]
        er = jnp.swapaxes(jnp.real(e), 1, 2)
        ei = jnp.swapaxes(jnp.imag(e), 1, 2)
        cm = []
        for j in range(4):
            sl = slice(8 * j, 8 * j + 8)
            cm.append(jnp.concatenate([_block_diag(er[sl]), -_block_diag(ei[sl])], axis=0))
        cms.append(jnp.stack(cm))
    prm['s5_cm'] = jnp.stack(cms).astype(BF16)
    prm['s5_d'] = p['s5_d'][l].reshape(1, WG)
    prm['s5_glu_w'] = p['s5_glu_w'][l].astype(BF16)
    prm['s5_glu_b'] = p['s5_glu_b'][l].reshape(1, WG)
    prm['ssd_conv_w'] = p['ssd_conv_w'][l]
    prm['ssd_conv_b'] = p['ssd_conv_b'][l].reshape(1, -1)
    pad = lambda v: jnp.concatenate([v.reshape(-1), jnp.zeros((128 - v.size,), F32)]).reshape(1, 128)
    prm['ssd_dtb'] = pad(p['ssd_dt_bias'][l])
    prm['ssd_arow'] = pad(-jnp.exp(p['ssd_a_log'][l]))
    prm['ssd_drow'] = jnp.repeat(p['ssd_d'][l], SSD_P).reshape(1, WG)
    prm['ssd_norm_g'] = p['ssd_norm_g'][l].reshape(1, WG)
    prm['lru_conv_w'] = p['lru_conv_w'][l]
    prm['lru_conv_b'] = p['lru_conv_b'][l].reshape(1, WG)
    prm['lru_wa'] = jnp.stack([_block_diag(p['lru_wa'][l][d]) for d in range(2)]).astype(BF16)
    prm['lru_wx'] = jnp.stack([_block_diag(p['lru_wx'][l][d]) for d in range(2)]).astype(BF16)
    prm['lru_ba'] = p['lru_ba'][l].reshape(2, 1, WG)
    prm['lru_bx'] = p['lru_bx'][l].reshape(2, 1, WG)
    prm['lru_sp'] = jax.nn.softplus(-p['lru_lam'][l]).reshape(2, 1, WG)
    prm['ml_gb'] = jnp.concatenate([jnp.zeros((SM_GATE,), F32), p['ml_gate_b'][l].reshape(-1),
                                    jnp.zeros((128 - SM_GATE - 4 * ML_H,), F32)]).reshape(1, 128)
    prm['ml_norm_g'] = p['ml_norm_g'][l].reshape(1, WG)
    prm['w_out'] = p['w_out'][l].astype(BF16)
    for k in ('ln1_g', 'ln1_b', 'ln2_g', 'ln2_b'):
        prm[k] = p[k][l].reshape(1, D)
    prm['router_w'] = p['router_w'][l]
    prm['router_b'] = p['router_b'][l].reshape(1, N_EXP)
    prm['moe_wgu'] = _wperm_call(p['moe_w_gate_up'][l])
    bgu = p['moe_b_gate_up'][l]
    prm['moe_bgu'] = jnp.concatenate([bgu[:, 0::2], bgu[:, 1::2]], axis=1).reshape(N_EXP, 1, 2 * D_FF)
    prm['moe_wd'] = p['moe_w_down'][l].astype(BF16)
    prm['moe_bd'] = p['moe_b_down'][l].reshape(N_EXP, 1, D)
    prm['ada_w'] = p['ada_w'][l].astype(BF16)
    prm['ada_b'] = p['ada_b'][l].reshape(1, 6 * D)
    return prm


def _mixers(flags, proj, prm):
    ya = _s5_call(flags, proj, None, prm, 0, reverse=False, final=False)
    ya = _s5_call(flags, proj, ya, prm, 1, reverse=True, final=True)
    yb = _ssd_call(flags, proj, None, prm, reverse=False, final=False)
    yb = _ssd_call(flags, proj, yb, prm, reverse=True, final=True)
    yc = _lru_call(flags, proj, None, prm, 0, reverse=False, final=False)
    yc = _lru_call(flags, proj, yc, prm, 1, reverse=True, final=True)
    yd = _mlstm_call(flags, proj, None, prm, reverse=False, final=False)
    yd = _mlstm_call(flags, proj, yd, prm, reverse=True, final=True)
    return ya, yb, yc, yd


def _to_rows(x):
    b, L, _ = x.shape
    return x.reshape(b // NB, NB, L, D).transpose(0, 2, 1, 3).reshape(b * L, D)


def _from_rows(r, b, L):
    return r.reshape(b // NB, L, NB, D).transpose(0, 2, 1, 3).reshape(b, L, D)


def _trunk(xs, cs, p, depth, moe_ts):
    first, last, gid = [], [], []
    g = 0
    for x in xs:
        b, L, _ = x.shape
        assert b % NB == 0 and L % CH == 0
        n = L // CH
        for _ in range(b // NB):
            first += [1] + [0] * (n - 1)
            last += [0] * (n - 1) + [1]
            gid += [g] * n
            g += 1
    first = jnp.asarray(first, jnp.int32)
    last = jnp.asarray(last, jnp.int32)
    gid = jnp.asarray(gid, jnp.int32)
    flags = (first, last)
    x = jnp.concatenate([_to_rows(x.astype(F32)) for x in xs], axis=0)
    c_all = jnp.concatenate([c.astype(F32) for c in cs], axis=0)
    R = x.shape[0]
    ts = min(moe_ts, R)
    assert R % ts == 0
    for l in range(depth):
        prm = _prep_layer(p, l)
        mod3 = _mod_call(c_all, prm['ada_w'], prm['ada_b']).reshape(g, NB, 6 * D)
        proj = _inproj_call(gid, x, mod3, prm['w_in'])
        ys = _mixers(flags, proj, prm)
        x1, h2, topi, topw = _outproj_call(gid, ys, x, mod3, prm)
        bexp, nblk, idx, wp = _moe_plan(topi, topw, ts)
        y = _moe_call(bexp, nblk, idx, wp, h2, prm, ts)
        x = _ln2_call(gid, x1, y, mod3, prm)
    outs = []
    r0 = 0
    for xin in xs:
        b, L, _ = xin.shape
        outs.append(_from_rows(x[r0:r0 + b * L], b, L).astype(xin.dtype))
        r0 += b * L
    return outs


def kernel(x_prompt, x_sample, c_prompt, c_sample, ada_w, ada_b, w_in, s5_lam_re, s5_lam_im, s5_log_dt, s5_b_re, s5_b_im, s5_c_re, s5_c_im, s5_d, s5_glu_w, s5_glu_b, ssd_conv_w, ssd_conv_b, ssd_a_log, ssd_dt_bias, ssd_d, ssd_norm_g, lru_conv_w, lru_conv_b, lru_wa, lru_ba, lru_wx, lru_bx, lru_lam, ml_gate_b, ml_norm_g, w_out, ln1_g, ln1_b, router_w, router_b, moe_w_gate_up, moe_b_gate_up, moe_w_down, moe_b_down, ln2_g, ln2_b):
    p = {
        'ada_w': ada_w, 'ada_b': ada_b, 'w_in': w_in,
        's5_lam_re': s5_lam_re, 's5_lam_im': s5_lam_im, 's5_log_dt': s5_log_dt,
        's5_b_re': s5_b_re, 's5_b_im': s5_b_im, 's5_c_re': s5_c_re, 's5_c_im': s5_c_im,
        's5_d': s5_d, 's5_glu_w': s5_glu_w, 's5_glu_b': s5_glu_b,
        'ssd_conv_w': ssd_conv_w, 'ssd_conv_b': ssd_conv_b, 'ssd_a_log': ssd_a_log,
        'ssd_dt_bias': ssd_dt_bias, 'ssd_d': ssd_d, 'ssd_norm_g': ssd_norm_g,
        'lru_conv_w': lru_conv_w, 'lru_conv_b': lru_conv_b, 'lru_wa': lru_wa, 'lru_ba': lru_ba,
        'lru_wx': lru_wx, 'lru_bx': lru_bx, 'lru_lam': lru_lam,
        'ml_gate_b': ml_gate_b, 'ml_norm_g': ml_norm_g, 'w_out': w_out,
        'ln1_g': ln1_g, 'ln1_b': ln1_b, 'router_w': router_w, 'router_b': router_b,
        'moe_w_gate_up': moe_w_gate_up, 'moe_b_gate_up': moe_b_gate_up,
        'moe_w_down': moe_w_down, 'moe_b_down': moe_b_down, 'ln2_g': ln2_g, 'ln2_b': ln2_b,
    }
    y_prompt, y_sample = _trunk([x_prompt, x_sample], [c_prompt, c_sample], p, DEPTH, MOE_TS)
    return (y_prompt, y_sample)
```

```python
import functools
import math

import numpy as np
import jax
import jax.numpy as jnp
from jax import lax
from jax.experimental import pallas as pl
from jax.experimental.pallas import tpu as pltpu

F32 = jnp.float32
BF16 = jnp.bfloat16
HIGHEST = lax.Precision.HIGHEST

D = 1024
DEPTH = 4
WG = 512
S5_G, S5_CH, S5_N = 32, 16, 64
SSD_H, SSD_P, SSD_NG, SSD_N = 8, 64, 2, 64
LRU_NB, LRU_BD, LRU_C = 8, 64, 8.0
ML_H, ML_DQK, ML_DV = 4, 64, 128
N_EXP, TOP_K, D_FF = 32, 4, 1024
SWIGLU_LIMIT, SWIGLU_ALPHA = 7.0, 1.702
ALPHA_DN = (2.0 * DEPTH) ** 0.25
LN_EPS, RMS_EPS = 1e-5, 1e-6

LANE = 128
NB = 8
CH = 128
ROWS = CH * NB
PW = 4480
K1_TN = 640
K3_TM = 512
MOE_RB = 256
MOE_UNROLL = 8
MOE_TS = 4096
VMEM_LIMIT = 56 * 1024 * 1024

COL_U, COL_XS, COL_Z, COL_XL, COL_GL, COL_V, COL_O = 0, 1, 2, 3, 4, 5, 6
COL_Q, COL_K, COL_BC = 14, 15, 16
COL_SM = 34
SM_DT, SM_GATE = 0, 16


def _sigmoid(x):
    return 1.0 / (1.0 + jnp.exp(-x))


def _silu(x):
    return x * _sigmoid(x)


def _softplus(x):
    return jnp.maximum(x, 0.0) + jnp.log(1.0 + jnp.exp(-jnp.abs(x)))


def _gelu_tanh(x):
    return 0.5 * x * (1.0 + jnp.tanh(math.sqrt(2.0 / math.pi) * (x + 0.044715 * (x * x * x))))


def _layer_norm(v, g, b):
    mu = jnp.mean(v, axis=-1, keepdims=True)
    vc = v - mu
    var = jnp.mean(vc * vc, axis=-1, keepdims=True)
    return vc * lax.rsqrt(var + LN_EPS) * g + b


def _bdot(a, b):
    return jnp.dot(a.astype(BF16), b.astype(BF16), preferred_element_type=F32)


def _per_slab(x, m):
    n = x.shape[0] // NB
    return (x.reshape(n, NB, x.shape[1]) * m[None]).reshape(x.shape)


def _per_slab_add(x, m):
    n = x.shape[0] // NB
    return (x.reshape(n, NB, x.shape[1]) + m[None]).reshape(x.shape)


def _cparams(sem):
    return pltpu.CompilerParams(dimension_semantics=sem, vmem_limit_bytes=VMEM_LIMIT)


def _mod_kernel(c_ref, w_ref, b_ref, o_ref):
    o_ref[...] = _bdot(_silu(c_ref[...]), w_ref[...]) + b_ref[...]


def _mod_call(c_all, w, b):
    n = c_all.shape[0]
    return pl.pallas_call(
        _mod_kernel,
        grid=(6,),
        in_specs=[pl.BlockSpec((n, D), lambda j: (0, 0)),
                  pl.BlockSpec((D, D), lambda j: (0, j)),
                  pl.BlockSpec((1, D), lambda j: (0, j))],
        out_specs=pl.BlockSpec((n, D), lambda j: (0, j)),
        out_shape=jax.ShapeDtypeStruct((n, 6 * D), F32),
        compiler_params=_cparams(("arbitrary",)),
    )(c_all, w, b)


def _inproj_kernel(gid_ref, x_ref, sc_ref, sh_ref, w_ref, o_ref, h_s):
    @pl.when(pl.program_id(1) == 0)
    def _():
        h = _per_slab_add(_per_slab(x_ref[...], 1.0 + sc_ref[0]), sh_ref[0])
        h_s[...] = h.astype(BF16)

    res = jnp.dot(h_s[...], w_ref[...], preferred_element_type=F32)
    for k in range(K1_TN // LANE):
        o_ref[k] = res[:, LANE * k:LANE * (k + 1)]


def _inproj_call(gid, x, mod3, w_in_p):
    R = x.shape[0]
    nch = R // ROWS
    gs = pltpu.PrefetchScalarGridSpec(
        num_scalar_prefetch=1,
        grid=(nch, PW // K1_TN),
        in_specs=[pl.BlockSpec((ROWS, D), lambda i, j, gid: (i, 0)),
                  pl.BlockSpec((1, NB, D), lambda i, j, gid: (gid[i], 0, 1)),
                  pl.BlockSpec((1, NB, D), lambda i, j, gid: (gid[i], 0, 0)),
                  pl.BlockSpec((D, K1_TN), lambda i, j, gid: (0, j))],
        out_specs=pl.BlockSpec((K1_TN // LANE, ROWS, LANE), lambda i, j, gid: (j, i, 0)),
        scratch_shapes=[pltpu.VMEM((ROWS, D), BF16)],
    )
    return pl.pallas_call(
        _inproj_kernel, grid_spec=gs,
        out_shape=jax.ShapeDtypeStruct((PW // LANE, R, LANE), F32),
        compiler_params=_cparams(("arbitrary", "arbitrary")),
    )(gid, x, mod3, mod3, w_in_p)


def _chunk_idx(nch, reverse):
    return (lambda c: nch - 1 - c) if reverse else (lambda c: c)


def _main_spec(width, col, nch, reverse):
    ci = _chunk_idx(nch, reverse)
    return pl.BlockSpec((width // LANE, ROWS, LANE), lambda c, *_: (col, ci(c), 0))


def _prev_spec(width, col, nch, reverse):
    ci = _chunk_idx(nch, reverse)
    per = ROWS // (2 * NB)
    return pl.BlockSpec((width // LANE, 2 * NB, LANE), lambda c, *_: (col, jnp.maximum(ci(c) * per - 1, 0), 0))


def _next_spec(width, col, nch, reverse):
    ci = _chunk_idx(nch, reverse)
    per = ROWS // NB
    return pl.BlockSpec((width // LANE, NB, LANE),
                        lambda c, *_: (col, jnp.minimum((ci(c) + 1) * per, nch * per - 1), 0))


def _const_spec(shape):
    nd = len(shape)
    return pl.BlockSpec(shape, lambda c, *_: (0,) * nd)


def _cat(ref):
    return jnp.concatenate([ref[k] for k in range(ref.shape[0])], axis=1)


def _put(ref, val):
    for k in range(ref.shape[0]):
        ref[k] = val[:, LANE * k:LANE * (k + 1)]


def _fill_ext(ext_s, k0, x_ref, xp_ref, xn_ref, is_first, is_last):
    for k in range(x_ref.shape[0]):
        ext_s[k0 + k, 0:2 * NB, :] = jnp.where(is_first, 0.0, xp_ref[k])
        ext_s[k0 + k, 2 * NB:2 * NB + ROWS, :] = x_ref[k]
        ext_s[k0 + k, 2 * NB + ROWS:3 * NB + ROWS, :] = jnp.where(is_last, 0.0, xn_ref[k])


def _dwconv(ext_s, k, w_ref, b_ref):
    sl = slice(LANE * k, LANE * (k + 1))
    out = b_ref[:, sl] + w_ref[0:1, sl] * ext_s[k, 0:ROWS, :]
    for j in range(1, 4):
        out = out + w_ref[j:j + 1, sl] * ext_s[k, j * NB:j * NB + ROWS, :]
    return out


def _tri(reverse):
    r = lax.broadcasted_iota(jnp.int32, (CH, CH), 0)
    c = lax.broadcasted_iota(jnp.int32, (CH, CH), 1)
    return (c >= r) if reverse else (c <= r)


def _s5_kernel(*refs, nch, reverse, final):
    first_ref, last_ref = refs[0], refs[1]
    if final:
        (u_ref, yp_ref, bt_ref, lr_ref, li_ref, cm_ref, d_ref, gw_ref, gb_ref,
         o_ref, bu_s, st_s, carry_s) = refs[2:]
    else:
        u_ref, bt_ref, lr_ref, li_ref, cm_ref, o_ref, bu_s, st_s, carry_s = refs[2:]
    c = pl.program_id(0)
    cc = nch - 1 - c if reverse else c
    start = last_ref[cc] if reverse else first_ref[cc]

    @pl.when(start == 1)
    def _():
        carry_s[...] = jnp.zeros_like(carry_s)

    half = WG
    for j in range(4):
        bu_s[...] = jnp.dot(u_ref[j].astype(BF16), bt_ref[j], preferred_element_type=F32)
        lr = jnp.broadcast_to(lr_ref[j], (NB, half))
        li = jnp.broadcast_to(li_ref[j], (NB, half))

        def step(i, carry, lr=lr, li=li):
            sr, si = carry
            t = CH - 1 - i if reverse else i
            r0 = pl.multiple_of(t * NB, NB)
            br = bu_s[pl.ds(r0, NB), 0:half]
            bi = bu_s[pl.ds(r0, NB), half:2 * half]
            nr = lr * sr - li * si + br
            ni = lr * si + li * sr + bi
            st_s[pl.ds(r0, NB), 0:half] = nr
            st_s[pl.ds(r0, NB), half:2 * half] = ni
            return nr, ni

        lax.fori_loop(0, CH, step, (carry_s[j, :, 0:half], carry_s[j, :, half:2 * half]), unroll=8)
        e0 = (CH - 1) * NB if reverse else 0
        carry_s[j] = st_s[e0:e0 + NB, :]
        o_ref[j] = jnp.dot(st_s[...].astype(BF16), cm_ref[j], preferred_element_type=F32)

    if final:
        y = _cat(o_ref) + _cat(yp_ref) + d_ref[...] * _cat(u_ref)
        g = _gelu_tanh(y)
        _put(o_ref, g * _sigmoid(_bdot(g, gw_ref[...]) + gb_ref[...]))


def _s5_call(flags, proj, yprev, prm, d, *, reverse, final):
    first, last = flags
    R = proj.shape[1]
    nch = R // ROWS
    in_specs = [_main_spec(WG, COL_U, nch, reverse)]
    args = [proj]
    if final:
        in_specs.append(_main_spec(WG, 0, nch, reverse))
        args.append(yprev)
    in_specs += [_const_spec((4, 128, 2 * WG)), _const_spec((4, 1, WG)), _const_spec((4, 1, WG)),
                 _const_spec((4, 2 * WG, 128))]
    args += [prm['s5_bt'], prm['s5_lr'][d], prm['s5_li'][d], prm['s5_cm'][d]]
    if final:
        in_specs += [_const_spec((1, WG)), _const_spec((WG, WG)), _const_spec((1, WG))]
        args += [prm['s5_d'], prm['s5_glu_w'], prm['s5_glu_b']]
    gs = pltpu.PrefetchScalarGridSpec(
        num_scalar_prefetch=2, grid=(nch,), in_specs=in_specs,
        out_specs=_main_spec(WG, 0, nch, reverse),
        scratch_shapes=[pltpu.VMEM((ROWS, 2 * WG), F32), pltpu.VMEM((ROWS, 2 * WG), F32),
                        pltpu.VMEM((4, NB, 2 * WG), F32)],
    )
    return pl.pallas_call(
        functools.partial(_s5_kernel, nch=nch, reverse=reverse, final=final), grid_spec=gs,
        out_shape=jax.ShapeDtypeStruct((WG // LANE, R, LANE), F32),
        compiler_params=_cparams(("arbitrary",)),
    )(first, last, *args)


def _lru_kernel(*refs, nch, reverse, final):
    first_ref, last_ref = refs[0], refs[1]
    if final:
        (x_ref, xp_ref, xn_ref, gate_ref, hp_ref, cw_ref, cb_ref, wa_ref, ba_ref, wx_ref, bx_ref, sp_ref,
         o_ref, ext_s, a_s, inp_s, carry_s) = refs[2:]
    else:
        (x_ref, xp_ref, xn_ref, cw_ref, cb_ref, wa_ref, ba_ref, wx_ref, bx_ref, sp_ref,
         o_ref, ext_s, a_s, inp_s, carry_s) = refs[2:]
    c = pl.program_id(0)
    cc = nch - 1 - c if reverse else c
    is_first = first_ref[cc] == 1
    is_last = last_ref[cc] == 1
    start = is_last if reverse else is_first

    @pl.when(start)
    def _():
        carry_s[...] = jnp.zeros_like(carry_s)

    _fill_ext(ext_s, 0, x_ref, xp_ref, xn_ref, is_first, is_last)
    xc = jnp.concatenate([_dwconv(ext_s, k, cw_ref, cb_ref) for k in range(WG // LANE)], axis=1)
    xb = xc.astype(BF16)
    r = _sigmoid(jnp.dot(xb, wa_ref[...], preferred_element_type=F32) + ba_ref[...])
    i = _sigmoid(jnp.dot(xb, wx_ref[...], preferred_element_type=F32) + bx_ref[...])
    log_a = -LRU_C * r * sp_ref[...]
    a_s[...] = jnp.exp(log_a)
    inp_s[...] = jnp.sqrt(1.0 - jnp.exp(2.0 * log_a)) * (i * xc)

    def step(k, h):
        t = CH - 1 - k if reverse else k
        r0 = pl.multiple_of(t * NB, NB)
        h = a_s[pl.ds(r0, NB), :] * h + inp_s[pl.ds(r0, NB), :]
        for k in range(WG // LANE):
            o_ref[k, pl.ds(r0, NB), :] = h[:, LANE * k:LANE * (k + 1)]
        return h

    carry_s[...] = lax.fori_loop(0, CH, step, carry_s[...], unroll=8)
    if final:
        for k in range(WG // LANE):
            o_ref[k] = (o_ref[k] + hp_ref[k]) * _gelu_tanh(gate_ref[k])


def _lru_call(flags, proj, hprev, prm, d, *, reverse, final):
    first, last = flags
    R = proj.shape[1]
    nch = R // ROWS
    in_specs = [_main_spec(WG, COL_XL, nch, reverse), _prev_spec(WG, COL_XL, nch, reverse),
                _next_spec(WG, COL_XL, nch, reverse)]
    args = [proj, proj, proj]
    if final:
        in_specs += [_main_spec(WG, COL_GL, nch, reverse), _main_spec(WG, 0, nch, reverse)]
        args += [proj, hprev]
    in_specs += [_const_spec((4, WG)), _const_spec((1, WG)), _const_spec((WG, WG)), _const_spec((1, WG)),
                 _const_spec((WG, WG)), _const_spec((1, WG)), _const_spec((1, WG))]
    args += [prm['lru_conv_w'], prm['lru_conv_b'], prm['lru_wa'][d], prm['lru_ba'][d], prm['lru_wx'][d],
             prm['lru_bx'][d], prm['lru_sp'][d]]
    gs = pltpu.PrefetchScalarGridSpec(
        num_scalar_prefetch=2, grid=(nch,), in_specs=in_specs,
        out_specs=_main_spec(WG, 0, nch, reverse),
        scratch_shapes=[pltpu.VMEM((WG // LANE, ROWS + 3 * NB, LANE), F32), pltpu.VMEM((ROWS, WG), F32),
                        pltpu.VMEM((ROWS, WG), F32), pltpu.VMEM((NB, WG), F32)],
    )
    return pl.pallas_call(
        functools.partial(_lru_kernel, nch=nch, reverse=reverse, final=final), grid_spec=gs,
        out_shape=jax.ShapeDtypeStruct((WG // LANE, R, LANE), F32),
        compiler_params=_cparams(("arbitrary",)),
    )(first, last, *args)


def _ssd_kernel(*refs, nch, reverse, final):
    first_ref, last_ref = refs[0], refs[1]
    if final:
        (x_ref, xp_ref, xn_ref, bc_ref, bcp_ref, bcn_ref, sm_ref, z_ref, yp_ref,
         cw_ref, cb_ref, dtb_ref, arow_ref, drow_ref, ng_ref,
         o_ref, ext_s, xd_s, smd_s, yd_s, st_s) = refs[2:]
    else:
        (x_ref, xp_ref, xn_ref, bc_ref, bcp_ref, bcn_ref, sm_ref,
         cw_ref, cb_ref, dtb_ref, arow_ref,
         o_ref, ext_s, xd_s, smd_s, yd_s, st_s) = refs[2:]
    c = pl.program_id(0)
    cc = nch - 1 - c if reverse else c
    is_first = first_ref[cc] == 1
    is_last = last_ref[cc] == 1
    start = is_last if reverse else is_first

    @pl.when(start)
    def _():
        st_s[...] = jnp.zeros_like(st_s)

    CW = WG + 2 * SSD_NG * SSD_N
    _fill_ext(ext_s, 0, x_ref, xp_ref, xn_ref, is_first, is_last)
    _fill_ext(ext_s, WG // LANE, bc_ref, bcp_ref, bcn_ref, is_first, is_last)
    for k in range(CW // LANE):
        ext_s[k, 0:ROWS, :] = _silu(_dwconv(ext_s, k, cw_ref, cb_ref))
    for b in range(NB):
        for k in range(CW // LANE):
            xd_s[b, :, LANE * k:LANE * (k + 1)] = ext_s[k, pl.ds(b, CH, stride=NB), :]
        smd_s[b] = sm_ref[0, pl.ds(b, CH, stride=NB), :]

    mask = _tri(reverse)
    tri = jnp.where(mask, 1.0, 0.0)
    doff = SM_DT + (SSD_H if reverse else 0)
    edge = 0 if reverse else CH - 1
    NS = SSD_N
    lane_c = lax.broadcasted_iota(jnp.int32, (CH, CH), 1)

    def per_b(b, _):
        xb = xd_s[b]
        xs = xb[:, 0:WG]
        bm = xb[:, WG:WG + SSD_NG * NS]
        cm = xb[:, WG + SSD_NG * NS:CW]
        dtf = _softplus(smd_s[b] + dtb_ref[...])
        af = dtf * arow_ref[...]
        cs = jnp.dot(tri, af, precision=HIGHEST, preferred_element_type=F32)
        cs_t = cs.T
        dt_t = dtf.T
        bm_t = bm.T
        cmask = [jnp.where((lane_c // NS) == g, cm, 0.0).astype(BF16) for g in range(SSD_NG)]
        bm_tb = bm_t.astype(BF16)
        gmat = [jnp.dot(cmask[g], bm_tb, preferred_element_type=F32) for g in range(SSD_NG)]
        for hp in range(SSD_H // 2):
            g = (2 * hp) // (SSD_H // SSD_NG)
            xpair = xs[:, LANE * hp:LANE * (hp + 1)]
            s_prev = st_s[b, hp]
            y = None
            upd = None
            ecol = None
            etot = None
            for k in range(2):
                ci = doff + 2 * hp + k
                mine = (lane_c // SSD_P) == k
                col_b = jnp.broadcast_to(cs[:, ci:ci + 1], (CH, CH))
                row = cs_t[ci:ci + 1, :]
                lmat = jnp.where(mask, jnp.exp(jnp.minimum(col_b - row, 0.0)), 0.0)
                m = gmat[g] * lmat * dt_t[ci:ci + 1, :]
                xk = jnp.where(mine, xpair, 0.0).astype(BF16)
                yk = jnp.dot(m.astype(BF16), xk, preferred_element_type=F32)
                tot = cs[edge:edge + 1, ci:ci + 1]
                wrow = jnp.exp(tot - row) * dt_t[ci:ci + 1, :]
                uk = jnp.dot((bm_t[NS * g:NS * (g + 1), :] * wrow).astype(BF16), xk, preferred_element_type=F32)
                y = yk if k == 0 else y + yk
                upd = uk if k == 0 else upd + uk
                ecol = jnp.exp(col_b) if k == 0 else jnp.where(mine, jnp.exp(col_b), ecol)
                etot = jnp.exp(tot) if k == 0 else jnp.where(mine[0:1, :], jnp.exp(tot), etot)
            s2 = jnp.concatenate([s_prev, s_prev], axis=0).astype(BF16)
            y = y + ecol * jnp.dot(cmask[g], s2, preferred_element_type=F32)
            st_s[b, hp] = etot * s_prev + upd
            yd_s[b, :, LANE * hp:LANE * (hp + 1)] = y
        return 0

    lax.fori_loop(0, NB, per_b, 0)
    for b in range(NB):
        for k in range(WG // LANE):
            o_ref[k, pl.ds(b, CH, stride=NB), :] = yd_s[b, :, LANE * k:LANE * (k + 1)]
    if final:
        xs_all = jnp.concatenate([ext_s[k, 0:ROWS, :] for k in range(WG // LANE)], axis=1)
        y = (_cat(o_ref) + _cat(yp_ref) + drow_ref[...] * xs_all) * _silu(_cat(z_ref))
        ms = jnp.mean(y * y, axis=-1, keepdims=True)
        _put(o_ref, y * lax.rsqrt(ms + RMS_EPS) * ng_ref[...])


def _ssd_call(flags, proj, yprev, prm, *, reverse, final):
    first, last = flags
    R = proj.shape[1]
    nch = R // ROWS
    CW = WG + 2 * SSD_NG * SSD_N
    in_specs = [_main_spec(WG, COL_XS, nch, reverse), _prev_spec(WG, COL_XS, nch, reverse),
                _next_spec(WG, COL_XS, nch, reverse),
                _main_spec(256, COL_BC, nch, reverse), _prev_spec(256, COL_BC, nch, reverse),
                _next_spec(256, COL_BC, nch, reverse),
                _main_spec(128, COL_SM, nch, reverse)]
    args = [proj] * 7
    if final:
        in_specs += [_main_spec(WG, COL_Z, nch, reverse), _main_spec(WG, 0, nch, reverse)]
        args += [proj, yprev]
    in_specs += [_const_spec((4, CW)), _const_spec((1, CW)), _const_spec((1, 128)), _const_spec((1, 128))]
    args += [prm['ssd_conv_w'], prm['ssd_conv_b'], prm['ssd_dtb'], prm['ssd_arow']]
    if final:
        in_specs += [_const_spec((1, WG)), _const_spec((1, WG))]
        args += [prm['ssd_drow'], prm['ssd_norm_g']]
    gs = pltpu.PrefetchScalarGridSpec(
        num_scalar_prefetch=2, grid=(nch,), in_specs=in_specs,
        out_specs=_main_spec(WG, 0, nch, reverse),
        scratch_shapes=[pltpu.VMEM((CW // LANE, ROWS + 3 * NB, LANE), F32), pltpu.VMEM((NB, CH, CW), F32),
                        pltpu.VMEM((NB, CH, 128), F32), pltpu.VMEM((NB, CH, WG), F32),
                        pltpu.VMEM((NB, SSD_H // 2, SSD_N, 2 * SSD_P), F32)],
    )
    return pl.pallas_call(
        functools.partial(_ssd_kernel, nch=nch, reverse=reverse, final=final), grid_spec=gs,
        out_shape=jax.ShapeDtypeStruct((WG // LANE, R, LANE), F32),
        compiler_params=_cparams(("arbitrary",)),
    )(first, last, *args)


NEG = -1e30


def _mlstm_kernel(*refs, nch, reverse, final):
    first_ref, last_ref = refs[0], refs[1]
    if final:
        (q_ref, k_ref, v_ref, sm_ref, og_ref, hp_ref, gb_ref, ng_ref,
         o_ref, qd_s, kd_s, vd_s, smd_s, hd_s, c_s, n_s, m_s) = refs[2:]
    else:
        (q_ref, k_ref, v_ref, sm_ref, gb_ref,
         o_ref, qd_s, kd_s, vd_s, smd_s, hd_s, c_s, n_s, m_s) = refs[2:]
    c = pl.program_id(0)
    cc = nch - 1 - c if reverse else c
    start = (last_ref[cc] if reverse else first_ref[cc]) == 1

    @pl.when(start)
    def _():
        c_s[...] = jnp.zeros_like(c_s)
        n_s[...] = jnp.zeros_like(n_s)
        m_s[...] = jnp.zeros_like(m_s)

    for b in range(NB):
        for src, dst in ((q_ref, qd_s), (k_ref, kd_s), (v_ref, vd_s)):
            for k in range(src.shape[0]):
                dst[b, :, LANE * k:LANE * (k + 1)] = src[k, pl.ds(b, CH, stride=NB), :]
        smd_s[b] = sm_ref[0, pl.ds(b, CH, stride=NB), :]

    mask = _tri(reverse)
    tri = jnp.where(mask, 1.0, 0.0)
    goff = SM_GATE + (2 * ML_H if reverse else 0)
    edge = 0 if reverse else CH - 1
    ones_b = jnp.ones((CH, CH), BF16)
    lane_c = lax.broadcasted_iota(jnp.int32, (CH, CH), 1)

    def per_b(b, _):
        sm = smd_s[b] + gb_ref[...]
        logf = -_softplus(-sm)
        fc_all = jnp.dot(tri, logf, precision=HIGHEST, preferred_element_type=F32)
        fr_all = fc_all.T
        sm_t = sm.T
        qb = qd_s[b] * (ML_DQK ** -0.5)
        kb_t = kd_s[b].T
        vb = vd_s[b]
        for h in range(ML_H):
            ci = goff + h
            cf = goff + ML_H + h
            fc = fc_all[:, cf:cf + 1]
            ic = sm[:, ci:ci + 1]
            fr = fr_all[cf:cf + 1, :]
            ir = sm_t[ci:ci + 1, :]
            m_prev = m_s[b, h]
            u = jnp.where(mask, ir - fr, NEG)
            g = jnp.maximum(jnp.max(u, axis=1, keepdims=True), m_prev)
            g_b = jnp.broadcast_to(g, (CH, CH))
            wts = jnp.exp(u - g_b)
            w_inter = jnp.exp(m_prev - g_b)
            hp, k = h // 2, h % 2
            qh = jnp.where((lane_c // ML_DQK) == k, qb[:, LANE * hp:LANE * (hp + 1)], 0.0).astype(BF16)
            kh_t = kb_t[ML_DQK * h:ML_DQK * (h + 1), :]
            vh = vb[:, ML_DV * h:ML_DV * (h + 1)]
            c_pair = c_s[b, hp]
            n_pair = n_s[b, hp]
            ct_prev = c_pair[ML_DQK * k:ML_DQK * (k + 1), :]
            n_prev = n_pair[ML_DQK * k:ML_DQK * (k + 1), :]
            sb = (jnp.dot(qh, kb_t[LANE * hp:LANE * (hp + 1), :].astype(BF16), preferred_element_type=F32)
                  * wts).astype(BF16)
            num = _bdot(sb, vh) + w_inter * _bdot(qh, c_pair)
            den = jnp.dot(sb, ones_b, preferred_element_type=F32) + w_inter * _bdot(qh, n_pair)
            hd_s[b, :, ML_DV * h:ML_DV * (h + 1)] = num / jnp.maximum(jnp.abs(den), jnp.exp(-(fc + g)))
            f_end = fc_all[edge:edge + 1, cf:cf + 1]
            w_log = f_end - fc + ic
            m_new = jnp.maximum(f_end + m_prev, jnp.max(w_log, axis=0, keepdims=True))
            w_s = jnp.broadcast_to(jnp.exp(w_log - m_new), (CH, CH))
            scale = jnp.exp(f_end + m_prev - m_new)
            c_s[b, hp, ML_DQK * k:ML_DQK * (k + 1), :] = scale * ct_prev + _bdot(kh_t, vh * w_s)
            n_s[b, hp, ML_DQK * k:ML_DQK * (k + 1), :] = scale * n_prev + _bdot(kh_t, w_s)
            m_s[b, h] = m_new
        return 0

    lax.fori_loop(0, NB, per_b, 0)
    for b in range(NB):
        for k in range(WG // LANE):
            o_ref[k, pl.ds(b, CH, stride=NB), :] = hd_s[b, :, LANE * k:LANE * (k + 1)]
    if final:
        assert ML_DV == LANE
        for h in range(ML_H):
            hh = o_ref[h] + hp_ref[h]
            ms = jnp.mean(hh * hh, axis=-1, keepdims=True)
            o_ref[h] = hh * lax.rsqrt(ms + RMS_EPS) * ng_ref[:, LANE * h:LANE * (h + 1)] * _sigmoid(og_ref[h])


def _mlstm_call(flags, proj, hprev, prm, *, reverse, final):
    first, last = flags
    R = proj.shape[1]
    nch = R // ROWS
    in_specs = [_main_spec(256, COL_Q, nch, reverse), _main_spec(256, COL_K, nch, reverse),
                _main_spec(WG, COL_V, nch, reverse), _main_spec(128, COL_SM, nch, reverse)]
    args = [proj] * 4
    if final:
        in_specs += [_main_spec(WG, COL_O, nch, reverse), _main_spec(WG, 0, nch, reverse)]
        args += [proj, hprev]
    in_specs += [_const_spec((1, 128))]
    args += [prm['ml_gb']]
    if final:
        in_specs += [_const_spec((1, WG))]
        args += [prm['ml_norm_g']]
    gs = pltpu.PrefetchScalarGridSpec(
        num_scalar_prefetch=2, grid=(nch,), in_specs=in_specs,
        out_specs=_main_spec(WG, 0, nch, reverse),
        scratch_shapes=[pltpu.VMEM((NB, CH, 256), F32), pltpu.VMEM((NB, CH, 256), F32),
                        pltpu.VMEM((NB, CH, WG), F32), pltpu.VMEM((NB, CH, 128), F32),
                        pltpu.VMEM((NB, CH, WG), F32),
                        pltpu.VMEM((NB, ML_H // 2, 2 * ML_DQK, ML_DV), F32),
                        pltpu.VMEM((NB, ML_H // 2, 2 * ML_DQK, CH), F32),
                        pltpu.VMEM((NB, ML_H, 1, 1), F32)],
    )
    return pl.pallas_call(
        functools.partial(_mlstm_kernel, nch=nch, reverse=reverse, final=final), grid_spec=gs,
        out_shape=jax.ShapeDtypeStruct((WG // LANE, R, LANE), F32),
        compiler_params=_cparams(("arbitrary",)),
    )(first, last, *args)


def _outproj_kernel(gid_ref, ya_ref, yb_ref, yc_ref, yd_ref, x_ref, g1_ref, sc_ref, sh_ref, wo_ref,
                    lg_ref, lb_ref, rw_ref, rb_ref, x1_ref, h2_ref, ti_ref, tw_ref):
    o = _bdot(_cat(ya_ref), wo_ref[0:WG, :])
    o = o + _bdot(_cat(yb_ref), wo_ref[WG:2 * WG, :])
    o = o + _bdot(_cat(yc_ref), wo_ref[2 * WG:3 * WG, :])
    o = o + _bdot(_cat(yd_ref), wo_ref[3 * WG:4 * WG, :])
    v = ALPHA_DN * x_ref[...] + _per_slab(o, g1_ref[0])
    x1 = _layer_norm(v, lg_ref[...], lb_ref[...])
    x1_ref[...] = x1
    h2 = _per_slab_add(_per_slab(x1, 1.0 + sc_ref[0]), sh_ref[0])
    bits = lax.bitcast_convert_type(h2.astype(BF16).astype(F32), jnp.uint32)
    h2_ref[...] = (bits[:, 0:D // 2] & jnp.uint32(0xFFFF0000)) | (bits[:, D // 2:D] >> jnp.uint32(16))
    logit = _bdot(h2, rw_ref[...]) + rb_ref[...]
    tm = logit.shape[0]
    lane = lax.broadcasted_iota(jnp.int32, (tm, N_EXP), 1)
    lane8 = lax.broadcasted_iota(jnp.int32, (tm, 8), 1)
    idx8 = jnp.zeros((tm, 8), jnp.int32)
    val8 = jnp.zeros((tm, 8), F32)
    top0 = None
    den = None
    for k in range(TOP_K):
        mx = jnp.max(logit, axis=1, keepdims=True)
        sel = jnp.min(jnp.where(logit == mx, lane, N_EXP), axis=1, keepdims=True)
        if k == 0:
            top0 = mx
        ek = jnp.exp(mx - top0)
        den = ek if k == 0 else den + ek
        idx8 = jnp.where(lane8 == k, sel, idx8)
        val8 = jnp.where(lane8 == k, ek, val8)
        logit = jnp.where(lane == sel, -jnp.inf, logit)
    ti_ref[...] = idx8
    tw_ref[...] = val8 / den


def _outproj_call(gid3, ys, x, mod3, prm):
    R = x.shape[0]
    per = ROWS // K3_TM
    rspec = lambda w: pl.BlockSpec((K3_TM, w), lambda i, gid: (i, 0))
    yspec = pl.BlockSpec((WG // LANE, K3_TM, LANE), lambda i, gid: (0, i, 0))
    mspec = lambda k: pl.BlockSpec((1, NB, D), lambda i, gid: (gid[i // per], 0, k))
    cspec = lambda shape: pl.BlockSpec(shape, lambda i, gid: (0,) * len(shape))
    gs = pltpu.PrefetchScalarGridSpec(
        num_scalar_prefetch=1, grid=(R // K3_TM,),
        in_specs=[yspec, yspec, yspec, yspec, rspec(D), mspec(2), mspec(4), mspec(3),
                  cspec((4 * WG, D)), cspec((1, D)), cspec((1, D)), cspec((D, N_EXP)), cspec((1, N_EXP))],
        out_specs=[rspec(D), rspec(D // 2), rspec(8), rspec(8)],
    )
    return pl.pallas_call(
        _outproj_kernel, grid_spec=gs,
        out_shape=[jax.ShapeDtypeStruct((R, D), F32), jax.ShapeDtypeStruct((R, D // 2), jnp.uint32),
                   jax.ShapeDtypeStruct((R, 8), jnp.int32), jax.ShapeDtypeStruct((R, 8), F32)],
        compiler_params=_cparams(("arbitrary",)),
    )(gid3, *ys, x, mod3, mod3, mod3, prm['w_out'], prm['ln1_g'], prm['ln1_b'], prm['router_w'], prm['router_b'])


def _moe_gather(idx_ref, h2_ref, dst, blk):
    s0 = (blk + 1) * MOE_RB
    for r in range(MOE_RB):
        tok = idx_ref[0, 0, s0 + r] & 0xFFFF
        dst[r:r + 1, :] = h2_ref[pl.ds(tok, 1), :]


def _moe_scatter(idx_ref, gw_ref, o_ref, ysrc, blk):
    s0 = (blk + 1) * MOE_RB
    for q in range(MOE_RB // MOE_UNROLL):
        toks, vals = [], []
        for j in range(MOE_UNROLL):
            r = q * MOE_UNROLL + j
            tok = idx_ref[0, 0, s0 + r] >> 16
            w = gw_ref[0, 0, s0 + r]
            toks.append(tok)
            vals.append(o_ref[0, pl.ds(tok, 1), :] + w * ysrc[r:r + 1, :])
        for tok, val in zip(toks, vals):
            o_ref[0, pl.ds(tok, 1), :] = val


def _moe_ffn(lhs, wgu_ref, wd_ref, bgu_ref, bd_ref, ydst):
    pk = lhs[...]
    lo = lax.bitcast_convert_type(pk << jnp.uint32(16), F32)
    hi = lax.bitcast_convert_type(pk & jnp.uint32(0xFFFF0000), F32)
    xb = jnp.concatenate([hi, lo], axis=1).astype(BF16)
    gu = jnp.dot(xb, wgu_ref[0], preferred_element_type=F32) + bgu_ref[0]
    g = jnp.minimum(gu[:, 0:D_FF], SWIGLU_LIMIT)
    u = jnp.clip(gu[:, D_FF:2 * D_FF], -SWIGLU_LIMIT, SWIGLU_LIMIT)
    hdn = (u + 1.0) * g * _sigmoid(SWIGLU_ALPHA * g)
    ydst[...] = jnp.dot(hdn.astype(BF16), wd_ref[0], preferred_element_type=F32) + bd_ref[0]


def _moe_kernel(bexp_ref, nblk_ref, idx_ref, gw_ref, h2_ref, wgu_ref, wd_ref, bgu_ref, bd_ref,
                o_ref, la_s, lb_s, ya_s, yb_s):
    i = pl.program_id(0)
    s = pl.program_id(1)

    @pl.when(s == 0)
    def _():
        o_ref[...] = jnp.zeros_like(o_ref)
        yb_s[...] = jnp.zeros_like(yb_s)
        _moe_gather(idx_ref, h2_ref, la_s, s)

    def step(cur_l, nxt_l, cur_y, prv_y):
        _moe_scatter(idx_ref, gw_ref, o_ref, prv_y, s - 1)
        _moe_gather(idx_ref, h2_ref, nxt_l, s + 1)
        _moe_ffn(cur_l, wgu_ref, wd_ref, bgu_ref, bd_ref, cur_y)

    active = s <= nblk_ref[i]

    @pl.when(active & (s % 2 == 0))
    def _():
        step(la_s, lb_s, ya_s, yb_s)

    @pl.when(active & (s % 2 == 1))
    def _():
        step(lb_s, la_s, yb_s, ya_s)


MOE_RUN = MOE_RB


def _moe_dims(ts):
    nbmax = ts * TOP_K // MOE_RB + N_EXP
    nsteps = nbmax + 1
    nba = nbmax + 3
    return nbmax, nsteps, nba


def _moe_call(bexp, nblk, idx, wp, h2p, prm, ts):
    R = h2p.shape[0]
    nst = R // ts
    nbmax, nsteps, nba = _moe_dims(ts)
    one = pl.Buffered(1)
    wspec = lambda shape: pl.BlockSpec(shape, lambda i, s, bexp, nblk: (bexp[i * nsteps + s], 0, 0))
    gs = pltpu.PrefetchScalarGridSpec(
        num_scalar_prefetch=2, grid=(nst, nsteps),
        in_specs=[pl.BlockSpec((1, 1, nba * MOE_RB), lambda i, s, *_: (i, 0, 0), memory_space=pltpu.SMEM),
                  pl.BlockSpec((1, 1, nba * MOE_RB), lambda i, s, *_: (i, 0, 0), memory_space=pltpu.SMEM),
                  pl.BlockSpec((ts, D // 2), lambda i, s, *_: (i, 0), pipeline_mode=one),
                  wspec((1, D, 2 * D_FF)), wspec((1, D_FF, D)), wspec((1, 1, 2 * D_FF)), wspec((1, 1, D))],
        out_specs=pl.BlockSpec((1, ts + MOE_UNROLL, D), lambda i, s, *_: (i, 0, 0), pipeline_mode=one),
        scratch_shapes=[pltpu.VMEM((MOE_RB, D // 2), jnp.uint32), pltpu.VMEM((MOE_RB, D // 2), jnp.uint32),
                        pltpu.VMEM((MOE_RB, D), F32), pltpu.VMEM((MOE_RB, D), F32)],
    )
    return pl.pallas_call(
        _moe_kernel, grid_spec=gs,
        out_shape=jax.ShapeDtypeStruct((nst, ts + MOE_UNROLL, D), F32),
        compiler_params=_cparams(("arbitrary", "arbitrary")),
    )(bexp, nblk, idx, wp, h2p, prm['moe_wgu'], prm['moe_wd'], prm['moe_bgu'], prm['moe_bd'])


def _moe_plan(topi, topw, ts):
    R = topi.shape[0]
    nst = R // ts
    na = ts * TOP_K
    nbmax, nsteps, nba = _moe_dims(ts)
    e_flat = topi[:, :TOP_K].reshape(nst, na)
    w_flat = topw[:, :TOP_K].reshape(nst, na)
    ex = jnp.arange(N_EXP, dtype=jnp.int32)
    counts = jnp.sum(e_flat[:, :, None] == ex[None, None, :], axis=1).astype(jnp.int32)
    padded = (counts + MOE_RUN - 1) // MOE_RUN * MOE_RUN
    pad_end = jnp.cumsum(padded, axis=1)
    total = pad_end[:, -1:]
    nblk = (total[:, 0] // MOE_RB).astype(jnp.int32)
    blk0 = jnp.minimum(jnp.arange(nsteps, dtype=jnp.int32)[None, :] * MOE_RUN, total - MOE_RUN)
    bexp = jnp.minimum(jnp.sum(blk0[:, :, None] >= pad_end[:, None, :], axis=2), N_EXP - 1).astype(jnp.int32)
    m = na + MOE_RUN
    unused = N_EXP * m
    key_real = e_flat * m + jnp.arange(na, dtype=jnp.int32)[None, :]
    d = jnp.arange(MOE_RUN - 1, dtype=jnp.int32)[None, None, :]
    key_pad = jnp.where(d < (padded - counts)[:, :, None], ex[None, :, None] * m + na + d, unused)
    keys = jnp.concatenate([key_real, key_pad.reshape(nst, -1)], axis=1)
    vals = jnp.concatenate([w_flat, jnp.zeros((nst, N_EXP * (MOE_RUN - 1)), F32)], axis=1)
    keys, vals = lax.sort((keys, vals), dimension=1, num_keys=1)
    n_tail = nba * MOE_RB - MOE_RB - keys.shape[1]
    assert n_tail >= 0
    keys = jnp.concatenate([jnp.full((nst, MOE_RB), unused, jnp.int32), keys,
                            jnp.full((nst, n_tail), unused, jnp.int32)], axis=1)
    wp = jnp.concatenate([jnp.zeros((nst, MOE_RB), F32), vals, jnp.zeros((nst, n_tail), F32)], axis=1)
    j = keys % m
    real = (keys < unused) & (j < na)
    tok = j // TOP_K
    spare = ts + (jnp.arange(nba * MOE_RB, dtype=jnp.int32)[None, :] & (MOE_UNROLL - 1))
    idx = jnp.where(real, tok | (tok << 16), spare << 16).astype(jnp.int32)
    wp = jnp.where(real, wp, 0.0)
    return bexp.reshape(-1), nblk, idx.reshape(nst, 1, -1), wp.reshape(nst, 1, -1)


def _wperm_kernel(w_ref, p_ref, o_ref):
    o_ref[0] = jnp.dot(w_ref[0].astype(BF16), p_ref[...], preferred_element_type=F32).astype(BF16)


def _wperm_call(wgu):
    rows = lax.broadcasted_iota(jnp.int32, (2 * D_FF, 2 * D_FF), 0)
    cols = lax.broadcasted_iota(jnp.int32, (2 * D_FF, 2 * D_FF), 1)
    perm = (rows == 2 * (cols % D_FF) + cols // D_FF).astype(BF16)
    return pl.pallas_call(
        _wperm_kernel,
        grid=(N_EXP, 2),
        in_specs=[pl.BlockSpec((1, D, 2 * D_FF), lambda e, h: (e, 0, 0)),
                  pl.BlockSpec((2 * D_FF, D_FF), lambda e, h: (0, h))],
        out_specs=pl.BlockSpec((1, D, D_FF), lambda e, h: (e, 0, h)),
        out_shape=jax.ShapeDtypeStruct((N_EXP, D, 2 * D_FF), BF16),
        compiler_params=_cparams(("arbitrary", "arbitrary")),
    )(wgu, perm)


def _ln2_kernel(gid_ref, x1_ref, y_ref, g2_ref, lg_ref, lb_ref, o_ref):
    v = ALPHA_DN * x1_ref[...] + _per_slab(y_ref[0], g2_ref[0])
    o_ref[...] = _layer_norm(v, lg_ref[...], lb_ref[...])


def _ln2_call(gid, x1, y, mod3, prm):
    R = x1.shape[0]
    per = (y.shape[1] - MOE_UNROLL) // ROWS
    rspec = pl.BlockSpec((ROWS, D), lambda i, gid: (i, 0))
    yspec = pl.BlockSpec((1, ROWS, D), lambda i, gid: (i // per, i % per, 0))
    cspec = pl.BlockSpec((1, D), lambda i, gid: (0, 0))
    gs = pltpu.PrefetchScalarGridSpec(
        num_scalar_prefetch=1, grid=(R // ROWS,),
        in_specs=[rspec, yspec, pl.BlockSpec((1, NB, D), lambda i, gid: (gid[i], 0, 5)), cspec, cspec],
        out_specs=rspec,
    )
    return pl.pallas_call(
        _ln2_kernel, grid_spec=gs, out_shape=jax.ShapeDtypeStruct((R, D), F32),
        compiler_params=_cparams(("arbitrary",)),
    )(gid, x1, y, mod3, prm['ln2_g'], prm['ln2_b'])


def _block_diag(blocks):
    n, r, c = blocks.shape
    eye = jnp.eye(n, dtype=blocks.dtype)
    return jnp.einsum('nrc,nm->nrmc', blocks, eye).reshape(n * r, n * c)


def _prep_layer(p, l):
    prm = {}
    w = p['w_in'][l]
    pieces = [w[:, 0:512], w[:, 512:1024], w[:, 1280:1792], w[:, 1808:2320], w[:, 2320:2832], w[:, 3344:3856],
              w[:, 3856:4368], w[:, 2832:3088], w[:, 3088:3344], w[:, 1024:1152], w[:, 1152:1280],
              w[:, 1792:1808], w[:, 4368:4384], jnp.zeros((D, 96), w.dtype)]
    prm['w_in'] = jnp.concatenate(pieces, axis=1).astype(BF16)
    b_re, b_im = p['s5_b_re'][l], p['s5_b_im'][l]
    bt = []
    for j in range(4):
        sl = slice(8 * j, 8 * j + 8)
        bre = _block_diag(jnp.swapaxes(b_re[sl], 1, 2))
        bim = _block_diag(jnp.swapaxes(b_im[sl], 1, 2))
        bt.append(jnp.concatenate([bre, bim], axis=1))
    prm['s5_bt'] = jnp.stack(bt).astype(BF16)
    lam = lax.complex(p['s5_lam_re'][l], p['s5_lam_im'][l])
    dt = jnp.exp(p['s5_log_dt'][l])[:, :, None]
    lam_bar = jnp.exp(lam * dt)
    f = (lam_bar - 1.0) / lam
    prm['s5_lr'] = jnp.real(lam_bar).reshape(2, 4, 1, WG)
    prm['s5_li'] = jnp.imag(lam_bar).reshape(2, 4, 1, WG)
    c_c = lax.complex(p['s5_c_re'][l], p['s5_c_im'][l])
    cms = []
    for d in range(2):
        e = c_c * f[d][:, None, :]
        er = jnp.swapaxes(jnp.real(e), 1, 2)
        ei = jnp.swapaxes(jnp.imag(e), 1, 2)
        cm = []
        for j in range(4):
            sl = slice(8 * j, 8 * j + 8)
            cm.append(jnp.concatenate([_block_diag(er[sl]), -_block_diag(ei[sl])], axis=0))
        cms.append(jnp.stack(cm))
    prm['s5_cm'] = jnp.stack(cms).astype(BF16)
    prm['s5_d'] = p['s5_d'][l].reshape(1, WG)
    prm['s5_glu_w'] = p['s5_glu_w'][l].astype(BF16)
    prm['s5_glu_b'] = p['s5_glu_b'][l].reshape(1, WG)
    prm['ssd_conv_w'] = p['ssd_conv_w'][l]
    prm['ssd_conv_b'] = p['ssd_conv_b'][l].reshape(1, -1)
    pad = lambda v: jnp.concatenate([v.reshape(-1), jnp.zeros((128 - v.size,), F32)]).reshape(1, 128)
    prm['ssd_dtb'] = pad(p['ssd_dt_bias'][l])
    prm['ssd_arow'] = pad(-jnp.exp(p['ssd_a_log'][l]))
    prm['ssd_drow'] = jnp.repeat(p['ssd_d'][l], SSD_P).reshape(1, WG)
    prm['ssd_norm_g'] = p['ssd_norm_g'][l].reshape(1, WG)
    prm['lru_conv_w'] = p['lru_conv_w'][l]
    prm['lru_conv_b'] = p['lru_conv_b'][l].reshape(1, WG)
    prm['lru_wa'] = jnp.stack([_block_diag(p['lru_wa'][l][d]) for d in range(2)]).astype(BF16)
    prm['lru_wx'] = jnp.stack([_block_diag(p['lru_wx'][l][d]) for d in range(2)]).astype(BF16)
    prm['lru_ba'] = p['lru_ba'][l].reshape(2, 1, WG)
    prm['lru_bx'] = p['lru_bx'][l].reshape(2, 1, WG)
    prm['lru_sp'] = jax.nn.softplus(-p['lru_lam'][l]).reshape(2, 1, WG)
    prm['ml_gb'] = jnp.concatenate([jnp.zeros((SM_GATE,), F32), p['ml_gate_b'][l].reshape(-1),
                                    jnp.zeros((128 - SM_GATE - 4 * ML_H,), F32)]).reshape(1, 128)
    prm['ml_norm_g'] = p['ml_norm_g'][l].reshape(1, WG)
    prm['w_out'] = p['w_out'][l].astype(BF16)
    for k in ('ln1_g', 'ln1_b', 'ln2_g', 'ln2_b'):
        prm[k] = p[k][l].reshape(1, D)
    prm['router_w'] = p['router_w'][l]
    prm['router_b'] = p['router_b'][l].reshape(1, N_EXP)
    prm['moe_wgu'] = _wperm_call(p['moe_w_gate_up'][l])
    bgu = p['moe_b_gate_up'][l]
    prm['moe_bgu'] = jnp.concatenate([bgu[:, 0::2], bgu[:, 1::2]], axis=1).reshape(N_EXP, 1, 2 * D_FF)
    prm['moe_wd'] = p['moe_w_down'][l].astype(BF16)
    prm['moe_bd'] = p['moe_b_down'][l].reshape(N_EXP, 1, D)
    prm['ada_w'] = p['ada_w'][l].astype(BF16)
    prm['ada_b'] = p['ada_b'][l].reshape(1, 6 * D)
    return prm


def _mixers(flags, proj, prm):
    ya = _s5_call(flags, proj, None, prm, 0, reverse=False, final=False)
    ya = _s5_call(flags, proj, ya, prm, 1, reverse=True, final=True)
    yb = _ssd_call(flags, proj, None, prm, reverse=False, final=False)
    yb = _ssd_call(flags, proj, yb, prm, reverse=True, final=True)
    yc = _lru_call(flags, proj, None, prm, 0, reverse=False, final=False)
    yc = _lru_call(flags, proj, yc, prm, 1, reverse=True, final=True)
    yd = _mlstm_call(flags, proj, None, prm, reverse=False, final=False)
    yd = _mlstm_call(flags, proj, yd, prm, reverse=True, final=True)
    return ya, yb, yc, yd


def _to_rows(x):
    b, L, _ = x.shape
    return x.reshape(b // NB, NB, L, D).transpose(0, 2, 1, 3).reshape(b * L, D)


def _from_rows(r, b, L):
    return r.reshape(b // NB, L, NB, D).transpose(0, 2, 1, 3).reshape(b, L, D)


def _trunk(xs, cs, p, depth, moe_ts):
    first, last, gid = [], [], []
    g = 0
    for x in xs:
        b, L, _ = x.shape
        assert b % NB == 0 and L % CH == 0
        n = L // CH
        for _ in range(b // NB):
            first += [1] + [0] * (n - 1)
            last += [0] * (n - 1) + [1]
            gid += [g] * n
            g += 1
    first = jnp.asarray(first, jnp.int32)
    last = jnp.asarray(last, jnp.int32)
    gid = jnp.asarray(gid, jnp.int32)
    flags = (first, last)
    x = jnp.concatenate([_to_rows(x.astype(F32)) for x in xs], axis=0)
    c_all = jnp.concatenate([c.astype(F32) for c in cs], axis=0)
    R = x.shape[0]
    ts = min(moe_ts, R)
    assert R % ts == 0
    for l in range(depth):
        prm = _prep_layer(p, l)
        mod3 = _mod_call(c_all, prm['ada_w'], prm['ada_b']).reshape(g, NB, 6 * D)
        proj = _inproj_call(gid, x, mod3, prm['w_in'])
        ys = _mixers(flags, proj, prm)
        x1, h2, topi, topw = _outproj_call(gid, ys, x, mod3, prm)
        bexp, nblk, idx, wp = _moe_plan(topi, topw, ts)
        y = _moe_call(bexp, nblk, idx, wp, h2, prm, ts)
        x = _ln2_call(gid, x1, y, mod3, prm)
    outs = []
    r0 = 0
    for xin in xs:
        b, L, _ = xin.shape
        outs.append(_from_rows(x[r0:r0 + b * L], b, L).astype(xin.dtype))
        r0 += b * L
    return outs


def kernel(x_prompt, x_sample, c_prompt, c_sample, ada_w, ada_b, w_in, s5_lam_re, s5_lam_im, s5_log_dt, s5_b_re, s5_b_im, s5_c_re, s5_c_im, s5_d, s5_glu_w, s5_glu_b, ssd_conv_w, ssd_conv_b, ssd_a_log, ssd_dt_bias, ssd_d, ssd_norm_g, lru_conv_w, lru_conv_b, lru_wa, lru_ba, lru_wx, lru_bx, lru_lam, ml_gate_b, ml_norm_g, w_out, ln1_g, ln1_b, router_w, router_b, moe_w_gate_up, moe_b_gate_up, moe_w_down, moe_b_down, ln2_g, ln2_b):
    p = {
        'ada_w': ada_w, 'ada_b': ada_b, 'w_in': w_in,
        's5_lam_re': s5_lam_re, 's5_lam_im': s5_lam_im, 's5_log_dt': s5_log_dt,
        's5_b_re': s5_b_re, 's5_b_im': s5_b_im, 's5_c_re': s5_c_re, 's5_c_im': s5_c_im,
        's5_d': s5_d, 's5_glu_w': s5_glu_w, 's5_glu_b': s5_glu_b,
        'ssd_conv_w': ssd_conv_w, 'ssd_conv_b': ssd_conv_b, 'ssd_a_log': ssd_a_log,
        'ssd_dt_bias': ssd_dt_bias, 'ssd_d': ssd_d, 'ssd_norm_g': ssd_norm_g,
        'lru_conv_w': lru_conv_w, 'lru_conv_b': lru_conv_b, 'lru_wa': lru_wa, 'lru_ba': lru_ba,
        'lru_wx': lru_wx, 'lru_bx': lru_bx, 'lru_lam': lru_lam,
        'ml_gate_b': ml_gate_b, 'ml_norm_g': ml_norm_g, 'w_out': w_out,
        'ln1_g': ln1_g, 'ln1_b': ln1_b, 'router_w': router_w, 'router_b': router_b,
        'moe_w_gate_up': moe_w_gate_up, 'moe_b_gate_up': moe_b_gate_up,
        'moe_w_down': moe_w_down, 'moe_b_down': moe_b_down, 'ln2_g': ln2_g, 'ln2_b': ln2_b,
    }
    y_prompt, y_sample = _trunk([x_prompt, x_sample], [c_prompt, c_sample], p, DEPTH, MOE_TS)
    return (y_prompt, y_sample)
```

```python
import functools
import math

import numpy as np
import jax
import jax.numpy as jnp
from jax import lax
from jax.experimental import pallas as pl
from jax.experimental.pallas import tpu as pltpu

F32 = jnp.float32
BF16 = jnp.bfloat16
HIGHEST = lax.Precision.HIGHEST

D = 1024
DEPTH = 4
WG = 512
S5_G, S5_CH, S5_N = 32, 16, 64
SSD_H, SSD_P, SSD_NG, SSD_N = 8, 64, 2, 64
LRU_NB, LRU_BD, LRU_C = 8, 64, 8.0
ML_H, ML_DQK, ML_DV = 4, 64, 128
N_EXP, TOP_K, D_FF = 32, 4, 1024
SWIGLU_LIMIT, SWIGLU_ALPHA = 7.0, 1.702
ALPHA_DN = (2.0 * DEPTH) ** 0.25
LN_EPS, RMS_EPS = 1e-5, 1e-6

LANE = 128
NB = 8
CH = 128
ROWS = CH * NB
PW = 4480
K1_TN = 640
K3_TM = 512
MOE_RB = 256
MOE_FC = 256
MOE_UNROLL = 8
MOE_TS = 4096
VMEM_LIMIT = 56 * 1024 * 1024

COL_U, COL_XS, COL_Z, COL_XL, COL_GL, COL_V, COL_O = 0, 1, 2, 3, 4, 5, 6
COL_Q, COL_K, COL_BC = 14, 15, 16
COL_SM = 34
SM_DT, SM_GATE = 0, 16


def _sigmoid(x):
    return 1.0 / (1.0 + jnp.exp(-x))


def _silu(x):
    return x * _sigmoid(x)


def _softplus(x):
    return jnp.maximum(x, 0.0) + jnp.log(1.0 + jnp.exp(-jnp.abs(x)))


def _gelu_tanh(x):
    return 0.5 * x * (1.0 + jnp.tanh(math.sqrt(2.0 / math.pi) * (x + 0.044715 * (x * x * x))))


def _layer_norm(v, g, b):
    mu = jnp.mean(v, axis=-1, keepdims=True)
    vc = v - mu
    var = jnp.mean(vc * vc, axis=-1, keepdims=True)
    return vc * lax.rsqrt(var + LN_EPS) * g + b


def _bdot(a, b):
    return jnp.dot(a.astype(BF16), b.astype(BF16), preferred_element_type=F32)


def _per_slab(x, m):
    n = x.shape[0] // NB
    return (x.reshape(n, NB, x.shape[1]) * m[None]).reshape(x.shape)


def _per_slab_add(x, m):
    n = x.shape[0] // NB
    return (x.reshape(n, NB, x.shape[1]) + m[None]).reshape(x.shape)


def _cparams(sem):
    return pltpu.CompilerParams(dimension_semantics=sem, vmem_limit_bytes=VMEM_LIMIT)


def _mod_kernel(c_ref, w_ref, b_ref, o_ref):
    o_ref[...] = _bdot(_silu(c_ref[...]), w_ref[...]) + b_ref[...]


def _mod_call(c_all, w, b):
    n = c_all.shape[0]
    return pl.pallas_call(
        _mod_kernel,
        grid=(6,),
        in_specs=[pl.BlockSpec((n, D), lambda j: (0, 0)),
                  pl.BlockSpec((D, D), lambda j: (0, j)),
                  pl.BlockSpec((1, D), lambda j: (0, j))],
        out_specs=pl.BlockSpec((n, D), lambda j: (0, j)),
        out_shape=jax.ShapeDtypeStruct((n, 6 * D), F32),
        compiler_params=_cparams(("arbitrary",)),
    )(c_all, w, b)


def _inproj_kernel(gid_ref, x_ref, sc_ref, sh_ref, w_ref, o_ref, h_s):
    @pl.when(pl.program_id(1) == 0)
    def _():
        h = _per_slab_add(_per_slab(x_ref[...], 1.0 + sc_ref[0]), sh_ref[0])
        h_s[...] = h.astype(BF16)

    res = jnp.dot(h_s[...], w_ref[...], preferred_element_type=F32)
    for k in range(K1_TN // LANE):
        o_ref[k] = res[:, LANE * k:LANE * (k + 1)]


def _inproj_call(gid, x, mod3, w_in_p):
    R = x.shape[0]
    nch = R // ROWS
    gs = pltpu.PrefetchScalarGridSpec(
        num_scalar_prefetch=1,
        grid=(nch, PW // K1_TN),
        in_specs=[pl.BlockSpec((ROWS, D), lambda i, j, gid: (i, 0)),
                  pl.BlockSpec((1, NB, D), lambda i, j, gid: (gid[i], 0, 1)),
                  pl.BlockSpec((1, NB, D), lambda i, j, gid: (gid[i], 0, 0)),
                  pl.BlockSpec((D, K1_TN), lambda i, j, gid: (0, j))],
        out_specs=pl.BlockSpec((K1_TN // LANE, ROWS, LANE), lambda i, j, gid: (j, i, 0)),
        scratch_shapes=[pltpu.VMEM((ROWS, D), BF16)],
    )
    return pl.pallas_call(
        _inproj_kernel, grid_spec=gs,
        out_shape=jax.ShapeDtypeStruct((PW // LANE, R, LANE), F32),
        compiler_params=_cparams(("arbitrary", "arbitrary")),
    )(gid, x, mod3, mod3, w_in_p)


def _chunk_idx(nch, reverse):
    return (lambda c: nch - 1 - c) if reverse else (lambda c: c)


def _main_spec(width, col, nch, reverse):
    ci = _chunk_idx(nch, reverse)
    return pl.BlockSpec((width // LANE, ROWS, LANE), lambda c, *_: (col, ci(c), 0))


def _prev_spec(width, col, nch, reverse):
    ci = _chunk_idx(nch, reverse)
    per = ROWS // (2 * NB)
    return pl.BlockSpec((width // LANE, 2 * NB, LANE), lambda c, *_: (col, jnp.maximum(ci(c) * per - 1, 0), 0))


def _next_spec(width, col, nch, reverse):
    ci = _chunk_idx(nch, reverse)
    per = ROWS // NB
    return pl.BlockSpec((width // LANE, NB, LANE),
                        lambda c, *_: (col, jnp.minimum((ci(c) + 1) * per, nch * per - 1), 0))


def _const_spec(shape):
    nd = len(shape)
    return pl.BlockSpec(shape, lambda c, *_: (0,) * nd)


def _cat(ref):
    return jnp.concatenate([ref[k] for k in range(ref.shape[0])], axis=1)


def _put(ref, val):
    for k in range(ref.shape[0]):
        ref[k] = val[:, LANE * k:LANE * (k + 1)]


def _fill_ext(ext_s, k0, x_ref, xp_ref, xn_ref, is_first, is_last):
    for k in range(x_ref.shape[0]):
        ext_s[k0 + k, 0:2 * NB, :] = jnp.where(is_first, 0.0, xp_ref[k])
        ext_s[k0 + k, 2 * NB:2 * NB + ROWS, :] = x_ref[k]
        ext_s[k0 + k, 2 * NB + ROWS:3 * NB + ROWS, :] = jnp.where(is_last, 0.0, xn_ref[k])


def _dwconv(ext_s, k, w_ref, b_ref):
    sl = slice(LANE * k, LANE * (k + 1))
    out = b_ref[:, sl] + w_ref[0:1, sl] * ext_s[k, 0:ROWS, :]
    for j in range(1, 4):
        out = out + w_ref[j:j + 1, sl] * ext_s[k, j * NB:j * NB + ROWS, :]
    return out


def _tri(reverse):
    r = lax.broadcasted_iota(jnp.int32, (CH, CH), 0)
    c = lax.broadcasted_iota(jnp.int32, (CH, CH), 1)
    return (c >= r) if reverse else (c <= r)


def _s5_kernel(*refs, nch, reverse, final):
    first_ref, last_ref = refs[0], refs[1]
    if final:
        (u_ref, yp_ref, bt_ref, lr_ref, li_ref, cm_ref, d_ref, gw_ref, gb_ref,
         o_ref, bu_s, st_s, carry_s) = refs[2:]
    else:
        u_ref, bt_ref, lr_ref, li_ref, cm_ref, o_ref, bu_s, st_s, carry_s = refs[2:]
    c = pl.program_id(0)
    cc = nch - 1 - c if reverse else c
    start = last_ref[cc] if reverse else first_ref[cc]

    @pl.when(start == 1)
    def _():
        carry_s[...] = jnp.zeros_like(carry_s)

    half = WG
    for j in range(4):
        bu_s[...] = jnp.dot(u_ref[j].astype(BF16), bt_ref[j], preferred_element_type=F32)
        lr = jnp.broadcast_to(lr_ref[j], (NB, half))
        li = jnp.broadcast_to(li_ref[j], (NB, half))

        def step(i, carry, lr=lr, li=li):
            sr, si = carry
            t = CH - 1 - i if reverse else i
            r0 = pl.multiple_of(t * NB, NB)
            br = bu_s[pl.ds(r0, NB), 0:half]
            bi = bu_s[pl.ds(r0, NB), half:2 * half]
            nr = lr * sr - li * si + br
            ni = lr * si + li * sr + bi
            st_s[pl.ds(r0, NB), 0:half] = nr
            st_s[pl.ds(r0, NB), half:2 * half] = ni
            return nr, ni

        lax.fori_loop(0, CH, step, (carry_s[j, :, 0:half], carry_s[j, :, half:2 * half]), unroll=8)
        e0 = (CH - 1) * NB if reverse else 0
        carry_s[j] = st_s[e0:e0 + NB, :]
        o_ref[j] = jnp.dot(st_s[...].astype(BF16), cm_ref[j], preferred_element_type=F32)

    if final:
        y = _cat(o_ref) + _cat(yp_ref) + d_ref[...] * _cat(u_ref)
        g = _gelu_tanh(y)
        _put(o_ref, g * _sigmoid(_bdot(g, gw_ref[...]) + gb_ref[...]))


def _s5_call(flags, proj, yprev, prm, d, *, reverse, final):
    first, last = flags
    R = proj.shape[1]
    nch = R // ROWS
    in_specs = [_main_spec(WG, COL_U, nch, reverse)]
    args = [proj]
    if final:
        in_specs.append(_main_spec(WG, 0, nch, reverse))
        args.append(yprev)
    in_specs += [_const_spec((4, 128, 2 * WG)), _const_spec((4, 1, WG)), _const_spec((4, 1, WG)),
                 _const_spec((4, 2 * WG, 128))]
    args += [prm['s5_bt'], prm['s5_lr'][d], prm['s5_li'][d], prm['s5_cm'][d]]
    if final:
        in_specs += [_const_spec((1, WG)), _const_spec((WG, WG)), _const_spec((1, WG))]
        args += [prm['s5_d'], prm['s5_glu_w'], prm['s5_glu_b']]
    gs = pltpu.PrefetchScalarGridSpec(
        num_scalar_prefetch=2, grid=(nch,), in_specs=in_specs,
        out_specs=_main_spec(WG, 0, nch, reverse),
        scratch_shapes=[pltpu.VMEM((ROWS, 2 * WG), F32), pltpu.VMEM((ROWS, 2 * WG), F32),
                        pltpu.VMEM((4, NB, 2 * WG), F32)],
    )
    return pl.pallas_call(
        functools.partial(_s5_kernel, nch=nch, reverse=reverse, final=final), grid_spec=gs,
        out_shape=jax.ShapeDtypeStruct((WG // LANE, R, LANE), F32),
        compiler_params=_cparams(("arbitrary",)),
    )(first, last, *args)


def _lru_kernel(*refs, nch, reverse, final):
    first_ref, last_ref = refs[0], refs[1]
    if final:
        (x_ref, xp_ref, xn_ref, gate_ref, hp_ref, cw_ref, cb_ref, wa_ref, ba_ref, wx_ref, bx_ref, sp_ref,
         o_ref, ext_s, a_s, inp_s, carry_s) = refs[2:]
    else:
        (x_ref, xp_ref, xn_ref, cw_ref, cb_ref, wa_ref, ba_ref, wx_ref, bx_ref, sp_ref,
         o_ref, ext_s, a_s, inp_s, carry_s) = refs[2:]
    c = pl.program_id(0)
    cc = nch - 1 - c if reverse else c
    is_first = first_ref[cc] == 1
    is_last = last_ref[cc] == 1
    start = is_last if reverse else is_first

    @pl.when(start)
    def _():
        carry_s[...] = jnp.zeros_like(carry_s)

    _fill_ext(ext_s, 0, x_ref, xp_ref, xn_ref, is_first, is_last)
    xc = jnp.concatenate([_dwconv(ext_s, k, cw_ref, cb_ref) for k in range(WG // LANE)], axis=1)
    xb = xc.astype(BF16)
    r = _sigmoid(jnp.dot(xb, wa_ref[...], preferred_element_type=F32) + ba_ref[...])
    i = _sigmoid(jnp.dot(xb, wx_ref[...], preferred_element_type=F32) + bx_ref[...])
    log_a = -LRU_C * r * sp_ref[...]
    a_s[...] = jnp.exp(log_a)
    inp_s[...] = jnp.sqrt(1.0 - jnp.exp(2.0 * log_a)) * (i * xc)

    def step(k, h):
        t = CH - 1 - k if reverse else k
        r0 = pl.multiple_of(t * NB, NB)
        h = a_s[pl.ds(r0, NB), :] * h + inp_s[pl.ds(r0, NB), :]
        for k in range(WG // LANE):
            o_ref[k, pl.ds(r0, NB), :] = h[:, LANE * k:LANE * (k + 1)]
        return h

    carry_s[...] = lax.fori_loop(0, CH, step, carry_s[...], unroll=8)
    if final:
        for k in range(WG // LANE):
            o_ref[k] = (o_ref[k] + hp_ref[k]) * _gelu_tanh(gate_ref[k])


def _lru_call(flags, proj, hprev, prm, d, *, reverse, final):
    first, last = flags
    R = proj.shape[1]
    nch = R // ROWS
    in_specs = [_main_spec(WG, COL_XL, nch, reverse), _prev_spec(WG, COL_XL, nch, reverse),
                _next_spec(WG, COL_XL, nch, reverse)]
    args = [proj, proj, proj]
    if final:
        in_specs += [_main_spec(WG, COL_GL, nch, reverse), _main_spec(WG, 0, nch, reverse)]
        args += [proj, hprev]
    in_specs += [_const_spec((4, WG)), _const_spec((1, WG)), _const_spec((WG, WG)), _const_spec((1, WG)),
                 _const_spec((WG, WG)), _const_spec((1, WG)), _const_spec((1, WG))]
    args += [prm['lru_conv_w'], prm['lru_conv_b'], prm['lru_wa'][d], prm['lru_ba'][d], prm['lru_wx'][d],
             prm['lru_bx'][d], prm['lru_sp'][d]]
    gs = pltpu.PrefetchScalarGridSpec(
        num_scalar_prefetch=2, grid=(nch,), in_specs=in_specs,
        out_specs=_main_spec(WG, 0, nch, reverse),
        scratch_shapes=[pltpu.VMEM((WG // LANE, ROWS + 3 * NB, LANE), F32), pltpu.VMEM((ROWS, WG), F32),
                        pltpu.VMEM((ROWS, WG), F32), pltpu.VMEM((NB, WG), F32)],
    )
    return pl.pallas_call(
        functools.partial(_lru_kernel, nch=nch, reverse=reverse, final=final), grid_spec=gs,
        out_shape=jax.ShapeDtypeStruct((WG // LANE, R, LANE), F32),
        compiler_params=_cparams(("arbitrary",)),
    )(first, last, *args)


def _ssd_kernel(*refs, nch, reverse, final):
    first_ref, last_ref = refs[0], refs[1]
    if final:
        (x_ref, xp_ref, xn_ref, bc_ref, bcp_ref, bcn_ref, sm_ref, z_ref, yp_ref,
         cw_ref, cb_ref, dtb_ref, arow_ref, drow_ref, ng_ref,
         o_ref, ext_s, xd_s, smd_s, yd_s, st_s) = refs[2:]
    else:
        (x_ref, xp_ref, xn_ref, bc_ref, bcp_ref, bcn_ref, sm_ref,
         cw_ref, cb_ref, dtb_ref, arow_ref,
         o_ref, ext_s, xd_s, smd_s, yd_s, st_s) = refs[2:]
    c = pl.program_id(0)
    cc = nch - 1 - c if reverse else c
    is_first = first_ref[cc] == 1
    is_last = last_ref[cc] == 1
    start = is_last if reverse else is_first

    @pl.when(start)
    def _():
        st_s[...] = jnp.zeros_like(st_s)

    CW = WG + 2 * SSD_NG * SSD_N
    _fill_ext(ext_s, 0, x_ref, xp_ref, xn_ref, is_first, is_last)
    _fill_ext(ext_s, WG // LANE, bc_ref, bcp_ref, bcn_ref, is_first, is_last)
    for k in range(CW // LANE):
        ext_s[k, 0:ROWS, :] = _silu(_dwconv(ext_s, k, cw_ref, cb_ref))
    for b in range(NB):
        for k in range(CW // LANE):
            xd_s[b, :, LANE * k:LANE * (k + 1)] = ext_s[k, pl.ds(b, CH, stride=NB), :]
        smd_s[b] = sm_ref[0, pl.ds(b, CH, stride=NB), :]

    mask = _tri(reverse)
    tri = jnp.where(mask, 1.0, 0.0)
    doff = SM_DT + (SSD_H if reverse else 0)
    edge = 0 if reverse else CH - 1
    NS = SSD_N
    lane_c = lax.broadcasted_iota(jnp.int32, (CH, CH), 1)

    def per_b(b, _):
        xb = xd_s[b]
        xs = xb[:, 0:WG]
        bm = xb[:, WG:WG + SSD_NG * NS]
        cm = xb[:, WG + SSD_NG * NS:CW]
        dtf = _softplus(smd_s[b] + dtb_ref[...])
        af = dtf * arow_ref[...]
        cs = jnp.dot(tri, af, precision=HIGHEST, preferred_element_type=F32)
        cs_t = cs.T
        dt_t = dtf.T
        bm_t = bm.T
        cmask = [jnp.where((lane_c // NS) == g, cm, 0.0).astype(BF16) for g in range(SSD_NG)]
        bm_tb = bm_t.astype(BF16)
        gmat = [jnp.dot(cmask[g], bm_tb, preferred_element_type=F32) for g in range(SSD_NG)]
        for hp in range(SSD_H // 2):
            g = (2 * hp) // (SSD_H // SSD_NG)
            xpair = xs[:, LANE * hp:LANE * (hp + 1)]
            s_prev = st_s[b, hp]
            y = None
            upd = None
            ecol = None
            etot = None
            for k in range(2):
                ci = doff + 2 * hp + k
                mine = (lane_c // SSD_P) == k
                col_b = jnp.broadcast_to(cs[:, ci:ci + 1], (CH, CH))
                row = cs_t[ci:ci + 1, :]
                lmat = jnp.where(mask, jnp.exp(jnp.minimum(col_b - row, 0.0)), 0.0)
                m = gmat[g] * lmat * dt_t[ci:ci + 1, :]
                xk = jnp.where(mine, xpair, 0.0).astype(BF16)
                yk = jnp.dot(m.astype(BF16), xk, preferred_element_type=F32)
                tot = cs[edge:edge + 1, ci:ci + 1]
                wrow = jnp.exp(tot - row) * dt_t[ci:ci + 1, :]
                uk = jnp.dot((bm_t[NS * g:NS * (g + 1), :] * wrow).astype(BF16), xk, preferred_element_type=F32)
                y = yk if k == 0 else y + yk
                upd = uk if k == 0 else upd + uk
                ecol = jnp.exp(col_b) if k == 0 else jnp.where(mine, jnp.exp(col_b), ecol)
                etot = jnp.exp(tot) if k == 0 else jnp.where(mine[0:1, :], jnp.exp(tot), etot)
            s2 = jnp.concatenate([s_prev, s_prev], axis=0).astype(BF16)
            y = y + ecol * jnp.dot(cmask[g], s2, preferred_element_type=F32)
            st_s[b, hp] = etot * s_prev + upd
            yd_s[b, :, LANE * hp:LANE * (hp + 1)] = y
        return 0

    lax.fori_loop(0, NB, per_b, 0)
    for b in range(NB):
        for k in range(WG // LANE):
            o_ref[k, pl.ds(b, CH, stride=NB), :] = yd_s[b, :, LANE * k:LANE * (k + 1)]
    if final:
        xs_all = jnp.concatenate([ext_s[k, 0:ROWS, :] for k in range(WG // LANE)], axis=1)
        y = (_cat(o_ref) + _cat(yp_ref) + drow_ref[...] * xs_all) * _silu(_cat(z_ref))
        ms = jnp.mean(y * y, axis=-1, keepdims=True)
        _put(o_ref, y * lax.rsqrt(ms + RMS_EPS) * ng_ref[...])


def _ssd_call(flags, proj, yprev, prm, *, reverse, final):
    first, last = flags
    R = proj.shape[1]
    nch = R // ROWS
    CW = WG + 2 * SSD_NG * SSD_N
    in_specs = [_main_spec(WG, COL_XS, nch, reverse), _prev_spec(WG, COL_XS, nch, reverse),
                _next_spec(WG, COL_XS, nch, reverse),
                _main_spec(256, COL_BC, nch, reverse), _prev_spec(256, COL_BC, nch, reverse),
                _next_spec(256, COL_BC, nch, reverse),
                _main_spec(128, COL_SM, nch, reverse)]
    args = [proj] * 7
    if final:
        in_specs += [_main_spec(WG, COL_Z, nch, reverse), _main_spec(WG, 0, nch, reverse)]
        args += [proj, yprev]
    in_specs += [_const_spec((4, CW)), _const_spec((1, CW)), _const_spec((1, 128)), _const_spec((1, 128))]
    args += [prm['ssd_conv_w'], prm['ssd_conv_b'], prm['ssd_dtb'], prm['ssd_arow']]
    if final:
        in_specs += [_const_spec((1, WG)), _const_spec((1, WG))]
        args += [prm['ssd_drow'], prm['ssd_norm_g']]
    gs = pltpu.PrefetchScalarGridSpec(
        num_scalar_prefetch=2, grid=(nch,), in_specs=in_specs,
        out_specs=_main_spec(WG, 0, nch, reverse),
        scratch_shapes=[pltpu.VMEM((CW // LANE, ROWS + 3 * NB, LANE), F32), pltpu.VMEM((NB, CH, CW), F32),
                        pltpu.VMEM((NB, CH, 128), F32), pltpu.VMEM((NB, CH, WG), F32),
                        pltpu.VMEM((NB, SSD_H // 2, SSD_N, 2 * SSD_P), F32)],
    )
    return pl.pallas_call(
        functools.partial(_ssd_kernel, nch=nch, reverse=reverse, final=final), grid_spec=gs,
        out_shape=jax.ShapeDtypeStruct((WG // LANE, R, LANE), F32),
        compiler_params=_cparams(("arbitrary",)),
    )(first, last, *args)


NEG = -1e30


def _mlstm_kernel(*refs, nch, reverse, final):
    first_ref, last_ref = refs[0], refs[1]
    if final:
        (q_ref, k_ref, v_ref, sm_ref, og_ref, hp_ref, gb_ref, ng_ref,
         o_ref, qd_s, kd_s, vd_s, smd_s, hd_s, c_s, n_s, m_s) = refs[2:]
    else:
        (q_ref, k_ref, v_ref, sm_ref, gb_ref,
         o_ref, qd_s, kd_s, vd_s, smd_s, hd_s, c_s, n_s, m_s) = refs[2:]
    c = pl.program_id(0)
    cc = nch - 1 - c if reverse else c
    start = (last_ref[cc] if reverse else first_ref[cc]) == 1

    @pl.when(start)
    def _():
        c_s[...] = jnp.zeros_like(c_s)
        n_s[...] = jnp.zeros_like(n_s)
        m_s[...] = jnp.zeros_like(m_s)

    for b in range(NB):
        for src, dst in ((q_ref, qd_s), (k_ref, kd_s), (v_ref, vd_s)):
            for k in range(src.shape[0]):
                dst[b, :, LANE * k:LANE * (k + 1)] = src[k, pl.ds(b, CH, stride=NB), :]
        smd_s[b] = sm_ref[0, pl.ds(b, CH, stride=NB), :]

    mask = _tri(reverse)
    tri = jnp.where(mask, 1.0, 0.0)
    goff = SM_GATE + (2 * ML_H if reverse else 0)
    edge = 0 if reverse else CH - 1
    ones_b = jnp.ones((CH, CH), BF16)
    lane_c = lax.broadcasted_iota(jnp.int32, (CH, CH), 1)

    def per_b(b, _):
        sm = smd_s[b] + gb_ref[...]
        logf = -_softplus(-sm)
        fc_all = jnp.dot(tri, logf, precision=HIGHEST, preferred_element_type=F32)
        fr_all = fc_all.T
        sm_t = sm.T
        qb = qd_s[b] * (ML_DQK ** -0.5)
        kb_t = kd_s[b].T
        vb = vd_s[b]
        for h in range(ML_H):
            ci = goff + h
            cf = goff + ML_H + h
            fc = fc_all[:, cf:cf + 1]
            ic = sm[:, ci:ci + 1]
            fr = fr_all[cf:cf + 1, :]
            ir = sm_t[ci:ci + 1, :]
            m_prev = m_s[b, h]
            u = jnp.where(mask, ir - fr, NEG)
            g = jnp.maximum(jnp.max(u, axis=1, keepdims=True), m_prev)
            g_b = jnp.broadcast_to(g, (CH, CH))
            wts = jnp.exp(u - g_b)
            w_inter = jnp.exp(m_prev - g_b)
            hp, k = h // 2, h % 2
            qh = jnp.where((lane_c // ML_DQK) == k, qb[:, LANE * hp:LANE * (hp + 1)], 0.0).astype(BF16)
            kh_t = kb_t[ML_DQK * h:ML_DQK * (h + 1), :]
            vh = vb[:, ML_DV * h:ML_DV * (h + 1)]
            c_pair = c_s[b, hp]
            n_pair = n_s[b, hp]
            ct_prev = c_pair[ML_DQK * k:ML_DQK * (k + 1), :]
            n_prev = n_pair[ML_DQK * k:ML_DQK * (k + 1), :]
            sb = (jnp.dot(qh, kb_t[LANE * hp:LANE * (hp + 1), :].astype(BF16), preferred_element_type=F32)
                  * wts).astype(BF16)
            num = _bdot(sb, vh) + w_inter * _bdot(qh, c_pair)
            den = jnp.dot(sb, ones_b, preferred_element_type=F32) + w_inter * _bdot(qh, n_pair)
            hd_s[b, :, ML_DV * h:ML_DV * (h + 1)] = num / jnp.maximum(jnp.abs(den), jnp.exp(-(fc + g)))
            f_end = fc_all[edge:edge + 1, cf:cf + 1]
            w_log = f_end - fc + ic
            m_new = jnp.maximum(f_end + m_prev, jnp.max(w_log, axis=0, keepdims=True))
            w_s = jnp.broadcast_to(jnp.exp(w_log - m_new), (CH, CH))
            scale = jnp.exp(f_end + m_prev - m_new)
            c_s[b, hp, ML_DQK * k:ML_DQK * (k + 1), :] = scale * ct_prev + _bdot(kh_t, vh * w_s)
            n_s[b, hp, ML_DQK * k:ML_DQK * (k + 1), :] = scale * n_prev + _bdot(kh_t, w_s)
            m_s[b, h] = m_new
        return 0

    lax.fori_loop(0, NB, per_b, 0)
    for b in range(NB):
        for k in range(WG // LANE):
            o_ref[k, pl.ds(b, CH, stride=NB), :] = hd_s[b, :, LANE * k:LANE * (k + 1)]
    if final:
        assert ML_DV == LANE
        for h in range(ML_H):
            hh = o_ref[h] + hp_ref[h]
            ms = jnp.mean(hh * hh, axis=-1, keepdims=True)
            o_ref[h] = hh * lax.rsqrt(ms + RMS_EPS) * ng_ref[:, LANE * h:LANE * (h + 1)] * _sigmoid(og_ref[h])


def _mlstm_call(flags, proj, hprev, prm, *, reverse, final):
    first, last = flags
    R = proj.shape[1]
    nch = R // ROWS
    in_specs = [_main_spec(256, COL_Q, nch, reverse), _main_spec(256, COL_K, nch, reverse),
                _main_spec(WG, COL_V, nch, reverse), _main_spec(128, COL_SM, nch, reverse)]
    args = [proj] * 4
    if final:
        in_specs += [_main_spec(WG, COL_O, nch, reverse), _main_spec(WG, 0, nch, reverse)]
        args += [proj, hprev]
    in_specs += [_const_spec((1, 128))]
    args += [prm['ml_gb']]
    if final:
        in_specs += [_const_spec((1, WG))]
        args += [prm['ml_norm_g']]
    gs = pltpu.PrefetchScalarGridSpec(
        num_scalar_prefetch=2, grid=(nch,), in_specs=in_specs,
        out_specs=_main_spec(WG, 0, nch, reverse),
        scratch_shapes=[pltpu.VMEM((NB, CH, 256), F32), pltpu.VMEM((NB, CH, 256), F32),
                        pltpu.VMEM((NB, CH, WG), F32), pltpu.VMEM((NB, CH, 128), F32),
                        pltpu.VMEM((NB, CH, WG), F32),
                        pltpu.VMEM((NB, ML_H // 2, 2 * ML_DQK, ML_DV), F32),
                        pltpu.VMEM((NB, ML_H // 2, 2 * ML_DQK, CH), F32),
                        pltpu.VMEM((NB, ML_H, 1, 1), F32)],
    )
    return pl.pallas_call(
        functools.partial(_mlstm_kernel, nch=nch, reverse=reverse, final=final), grid_spec=gs,
        out_shape=jax.ShapeDtypeStruct((WG // LANE, R, LANE), F32),
        compiler_params=_cparams(("arbitrary",)),
    )(first, last, *args)


def _outproj_kernel(gid_ref, ya_ref, yb_ref, yc_ref, yd_ref, x_ref, g1_ref, sc_ref, sh_ref, wo_ref,
                    lg_ref, lb_ref, rw_ref, rb_ref, x1_ref, h2_ref, ti_ref, tw_ref):
    o = _bdot(_cat(ya_ref), wo_ref[0:WG, :])
    o = o + _bdot(_cat(yb_ref), wo_ref[WG:2 * WG, :])
    o = o + _bdot(_cat(yc_ref), wo_ref[2 * WG:3 * WG, :])
    o = o + _bdot(_cat(yd_ref), wo_ref[3 * WG:4 * WG, :])
    v = ALPHA_DN * x_ref[...] + _per_slab(o, g1_ref[0])
    x1 = _layer_norm(v, lg_ref[...], lb_ref[...])
    x1_ref[...] = x1
    h2 = _per_slab_add(_per_slab(x1, 1.0 + sc_ref[0]), sh_ref[0])
    bits = lax.bitcast_convert_type(h2.astype(BF16).astype(F32), jnp.uint32)
    h2_ref[...] = (bits[:, 0:D // 2] & jnp.uint32(0xFFFF0000)) | (bits[:, D // 2:D] >> jnp.uint32(16))
    logit = _bdot(h2, rw_ref[...]) + rb_ref[...]
    tm = logit.shape[0]
    lane = lax.broadcasted_iota(jnp.int32, (tm, N_EXP), 1)
    lane8 = lax.broadcasted_iota(jnp.int32, (tm, 8), 1)
    idx8 = jnp.zeros((tm, 8), jnp.int32)
    val8 = jnp.zeros((tm, 8), F32)
    top0 = None
    den = None
    for k in range(TOP_K):
        mx = jnp.max(logit, axis=1, keepdims=True)
        sel = jnp.min(jnp.where(logit == mx, lane, N_EXP), axis=1, keepdims=True)
        if k == 0:
            top0 = mx
        ek = jnp.exp(mx - top0)
        den = ek if k == 0 else den + ek
        idx8 = jnp.where(lane8 == k, sel, idx8)
        val8 = jnp.where(lane8 == k, ek, val8)
        logit = jnp.where(lane == sel, -jnp.inf, logit)
    ti_ref[...] = idx8
    tw_ref[...] = val8 / den


def _outproj_call(gid3, ys, x, mod3, prm):
    R = x.shape[0]
    per = ROWS // K3_TM
    rspec = lambda w: pl.BlockSpec((K3_TM, w), lambda i, gid: (i, 0))
    yspec = pl.BlockSpec((WG // LANE, K3_TM, LANE), lambda i, gid: (0, i, 0))
    mspec = lambda k: pl.BlockSpec((1, NB, D), lambda i, gid: (gid[i // per], 0, k))
    cspec = lambda shape: pl.BlockSpec(shape, lambda i, gid: (0,) * len(shape))
    gs = pltpu.PrefetchScalarGridSpec(
        num_scalar_prefetch=1, grid=(R // K3_TM,),
        in_specs=[yspec, yspec, yspec, yspec, rspec(D), mspec(2), mspec(4), mspec(3),
                  cspec((4 * WG, D)), cspec((1, D)), cspec((1, D)), cspec((D, N_EXP)), cspec((1, N_EXP))],
        out_specs=[rspec(D), rspec(D // 2), rspec(8), rspec(8)],
    )
    return pl.pallas_call(
        _outproj_kernel, grid_spec=gs,
        out_shape=[jax.ShapeDtypeStruct((R, D), F32), jax.ShapeDtypeStruct((R, D // 2), jnp.uint32),
                   jax.ShapeDtypeStruct((R, 8), jnp.int32), jax.ShapeDtypeStruct((R, 8), F32)],
        compiler_params=_cparams(("arbitrary",)),
    )(gid3, *ys, x, mod3, mod3, mod3, prm['w_out'], prm['ln1_g'], prm['ln1_b'], prm['router_w'], prm['router_b'])


def _moe_gather(idx_ref, h2_ref, dst, blk):
    s0 = (blk + 1) * MOE_RB
    for r in range(MOE_RB):
        tok = idx_ref[0, 0, s0 + r] & 0xFFFF
        dst[r:r + 1, :] = h2_ref[pl.ds(tok, 1), :]


def _moe_scatter(idx_ref, gw_ref, o_ref, ysrc, blk):
    s0 = (blk + 1) * MOE_RB
    for q in range(MOE_RB // MOE_UNROLL):
        toks, vals = [], []
        for j in range(MOE_UNROLL):
            r = q * MOE_UNROLL + j
            tok = idx_ref[0, 0, s0 + r] >> 16
            w = gw_ref[0, 0, s0 + r]
            toks.append(tok)
            vals.append(o_ref[0, pl.ds(tok, 1), :] + w * ysrc[r:r + 1, :])
        for tok, val in zip(toks, vals):
            o_ref[0, pl.ds(tok, 1), :] = val


def _moe_ffn(lhs, wgu_ref, wd_ref, bgu_ref, bd_ref, ydst):
    pk = lhs[...]
    lo = lax.bitcast_convert_type(pk << jnp.uint32(16), F32)
    hi = lax.bitcast_convert_type(pk & jnp.uint32(0xFFFF0000), F32)
    xb = jnp.concatenate([hi, lo], axis=1).astype(BF16)
    acc = None
    for c in range(D_FF // MOE_FC):
        lo_c, hi_c = MOE_FC * c, MOE_FC * (c + 1)
        g = jnp.dot(xb, wgu_ref[0, :, lo_c:hi_c], preferred_element_type=F32) + bgu_ref[0, :, lo_c:hi_c]
        u = (jnp.dot(xb, wgu_ref[0, :, D_FF + lo_c:D_FF + hi_c], preferred_element_type=F32)
             + bgu_ref[0, :, D_FF + lo_c:D_FF + hi_c])
        g = jnp.minimum(g, SWIGLU_LIMIT)
        u = jnp.clip(u, -SWIGLU_LIMIT, SWIGLU_LIMIT)
        hdn = ((u + 1.0) * g * _sigmoid(SWIGLU_ALPHA * g)).astype(BF16)
        part = jnp.dot(hdn, wd_ref[0, lo_c:hi_c, :], preferred_element_type=F32)
        acc = part if c == 0 else acc + part
    ydst[...] = acc + bd_ref[0]


def _moe_kernel(bexp_ref, nblk_ref, idx_ref, gw_ref, h2_ref, wgu_ref, wd_ref, bgu_ref, bd_ref,
                o_ref, la_s, lb_s, ya_s, yb_s):
    i = pl.program_id(0)
    s = pl.program_id(1)

    @pl.when(s == 0)
    def _():
        o_ref[...] = jnp.zeros_like(o_ref)
        yb_s[...] = jnp.zeros_like(yb_s)
        _moe_gather(idx_ref, h2_ref, la_s, s)

    def step(cur_l, nxt_l, cur_y, prv_y):
        _moe_scatter(idx_ref, gw_ref, o_ref, prv_y, s - 1)
        _moe_gather(idx_ref, h2_ref, nxt_l, s + 1)
        _moe_ffn(cur_l, wgu_ref, wd_ref, bgu_ref, bd_ref, cur_y)

    active = s <= nblk_ref[i]

    @pl.when(active & (s % 2 == 0))
    def _():
        step(la_s, lb_s, ya_s, yb_s)

    @pl.when(active & (s % 2 == 1))
    def _():
        step(lb_s, la_s, yb_s, ya_s)


MOE_RUN = MOE_RB


def _moe_dims(ts):
    nbmax = ts * TOP_K // MOE_RB + N_EXP
    nsteps = nbmax + 1
    nba = nbmax + 3
    return nbmax, nsteps, nba


def _moe_call(bexp, nblk, idx, wp, h2p, prm, ts):
    R = h2p.shape[0]
    nst = R // ts
    nbmax, nsteps, nba = _moe_dims(ts)
    one = pl.Buffered(1)
    wspec = lambda shape: pl.BlockSpec(shape, lambda i, s, bexp, nblk: (bexp[i * nsteps + s], 0, 0))
    gs = pltpu.PrefetchScalarGridSpec(
        num_scalar_prefetch=2, grid=(nst, nsteps),
        in_specs=[pl.BlockSpec((1, 1, nba * MOE_RB), lambda i, s, *_: (i, 0, 0), memory_space=pltpu.SMEM),
                  pl.BlockSpec((1, 1, nba * MOE_RB), lambda i, s, *_: (i, 0, 0), memory_space=pltpu.SMEM),
                  pl.BlockSpec((ts, D // 2), lambda i, s, *_: (i, 0), pipeline_mode=one),
                  wspec((1, D, 2 * D_FF)), wspec((1, D_FF, D)), wspec((1, 1, 2 * D_FF)), wspec((1, 1, D))],
        out_specs=pl.BlockSpec((1, ts + MOE_UNROLL, D), lambda i, s, *_: (i, 0, 0), pipeline_mode=one),
        scratch_shapes=[pltpu.VMEM((MOE_RB, D // 2), jnp.uint32), pltpu.VMEM((MOE_RB, D // 2), jnp.uint32),
                        pltpu.VMEM((MOE_RB, D), F32), pltpu.VMEM((MOE_RB, D), F32)],
    )
    return pl.pallas_call(
        _moe_kernel, grid_spec=gs,
        out_shape=jax.ShapeDtypeStruct((nst, ts + MOE_UNROLL, D), F32),
        compiler_params=_cparams(("arbitrary", "arbitrary")),
    )(bexp, nblk, idx, wp, h2p, prm['moe_wgu'], prm['moe_wd'], prm['moe_bgu'], prm['moe_bd'])


def _moe_plan(topi, topw, ts):
    R = topi.shape[0]
    nst = R // ts
    na = ts * TOP_K
    nbmax, nsteps, nba = _moe_dims(ts)
    e_flat = topi[:, :TOP_K].reshape(nst, na)
    w_flat = topw[:, :TOP_K].reshape(nst, na)
    ex = jnp.arange(N_EXP, dtype=jnp.int32)
    counts = jnp.sum(e_flat[:, :, None] == ex[None, None, :], axis=1).astype(jnp.int32)
    padded = (counts + MOE_RUN - 1) // MOE_RUN * MOE_RUN
    pad_end = jnp.cumsum(padded, axis=1)
    total = pad_end[:, -1:]
    nblk = (total[:, 0] // MOE_RB).astype(jnp.int32)
    blk0 = jnp.minimum(jnp.arange(nsteps, dtype=jnp.int32)[None, :] * MOE_RUN, total - MOE_RUN)
    bexp = jnp.minimum(jnp.sum(blk0[:, :, None] >= pad_end[:, None, :], axis=2), N_EXP - 1).astype(jnp.int32)
    m = na + MOE_RUN
    unused = N_EXP * m
    key_real = e_flat * m + jnp.arange(na, dtype=jnp.int32)[None, :]
    d = jnp.arange(MOE_RUN - 1, dtype=jnp.int32)[None, None, :]
    key_pad = jnp.where(d < (padded - counts)[:, :, None], ex[None, :, None] * m + na + d, unused)
    keys = jnp.concatenate([key_real, key_pad.reshape(nst, -1)], axis=1)
    vals = jnp.concatenate([w_flat, jnp.zeros((nst, N_EXP * (MOE_RUN - 1)), F32)], axis=1)
    keys, vals = lax.sort((keys, vals), dimension=1, num_keys=1)
    n_tail = nba * MOE_RB - MOE_RB - keys.shape[1]
    assert n_tail >= 0
    keys = jnp.concatenate([jnp.full((nst, MOE_RB), unused, jnp.int32), keys,
                            jnp.full((nst, n_tail), unused, jnp.int32)], axis=1)
    wp = jnp.concatenate([jnp.zeros((nst, MOE_RB), F32), vals, jnp.zeros((nst, n_tail), F32)], axis=1)
    j = keys % m
    real = (keys < unused) & (j < na)
    tok = j // TOP_K
    spare = ts + (jnp.arange(nba * MOE_RB, dtype=jnp.int32)[None, :] & (MOE_UNROLL - 1))
    idx = jnp.where(real, tok | (tok << 16), spare << 16).astype(jnp.int32)
    wp = jnp.where(real, wp, 0.0)
    return bexp.reshape(-1), nblk, idx.reshape(nst, 1, -1), wp.reshape(nst, 1, -1)


def _wperm_kernel(w_ref, p_ref, o_ref):
    o_ref[0] = jnp.dot(w_ref[0].astype(BF16), p_ref[...], preferred_element_type=F32).astype(BF16)


def _wperm_call(wgu):
    rows = lax.broadcasted_iota(jnp.int32, (2 * D_FF, 2 * D_FF), 0)
    cols = lax.broadcasted_iota(jnp.int32, (2 * D_FF, 2 * D_FF), 1)
    perm = (rows == 2 * (cols % D_FF) + cols // D_FF).astype(BF16)
    return pl.pallas_call(
        _wperm_kernel,
        grid=(N_EXP, 2),
        in_specs=[pl.BlockSpec((1, D, 2 * D_FF), lambda e, h: (e, 0, 0)),
                  pl.BlockSpec((2 * D_FF, D_FF), lambda e, h: (0, h))],
        out_specs=pl.BlockSpec((1, D, D_FF), lambda e, h: (e, 0, h)),
        out_shape=jax.ShapeDtypeStruct((N_EXP, D, 2 * D_FF), BF16),
        compiler_params=_cparams(("arbitrary", "arbitrary")),
    )(wgu, perm)


def _ln2_kernel(gid_ref, x1_ref, y_ref, g2_ref, lg_ref, lb_ref, o_ref):
    v = ALPHA_DN * x1_ref[...] + _per_slab(y_ref[0], g2_ref[0])
    o_ref[...] = _layer_norm(v, lg_ref[...], lb_ref[...])


def _ln2_call(gid, x1, y, mod3, prm):
    R = x1.shape[0]
    per = (y.shape[1] - MOE_UNROLL) // ROWS
    rspec = pl.BlockSpec((ROWS, D), lambda i, gid: (i, 0))
    yspec = pl.BlockSpec((1, ROWS, D), lambda i, gid: (i // per, i % per, 0))
    cspec = pl.BlockSpec((1, D), lambda i, gid: (0, 0))
    gs = pltpu.PrefetchScalarGridSpec(
        num_scalar_prefetch=1, grid=(R // ROWS,),
        in_specs=[rspec, yspec, pl.BlockSpec((1, NB, D), lambda i, gid: (gid[i], 0, 5)), cspec, cspec],
        out_specs=rspec,
    )
    return pl.pallas_call(
        _ln2_kernel, grid_spec=gs, out_shape=jax.ShapeDtypeStruct((R, D), F32),
        compiler_params=_cparams(("arbitrary",)),
    )(gid, x1, y, mod3, prm['ln2_g'], prm['ln2_b'])


def _block_diag(blocks):
    n, r, c = blocks.shape
    eye = jnp.eye(n, dtype=blocks.dtype)
    return jnp.einsum('nrc,nm->nrmc', blocks, eye).reshape(n * r, n * c)


def _prep_layer(p, l):
    prm = {}
    w = p['w_in'][l]
    pieces = [w[:, 0:512], w[:, 512:1024], w[:, 1280:1792], w[:, 1808:2320], w[:, 2320:2832], w[:, 3344:3856],
              w[:, 3856:4368], w[:, 2832:3088], w[:, 3088:3344], w[:, 1024:1152], w[:, 1152:1280],
              w[:, 1792:1808], w[:, 4368:4384], jnp.zeros((D, 96), w.dtype)]
    prm['w_in'] = jnp.concatenate(pieces, axis=1).astype(BF16)
    b_re, b_im = p['s5_b_re'][l], p['s5_b_im'][l]
    bt = []
    for j in range(4):
        sl = slice(8 * j, 8 * j + 8)
        bre = _block_diag(jnp.swapaxes(b_re[sl], 1, 2))
        bim = _block_diag(jnp.swapaxes(b_im[sl], 1, 2))
        bt.append(jnp.concatenate([bre, bim], axis=1))
    prm['s5_bt'] = jnp.stack(bt).astype(BF16)
    lam = lax.complex(p['s5_lam_re'][l], p['s5_lam_im'][l])
    dt = jnp.exp(p['s5_log_dt'][l])[:, :, None]
    lam_bar = jnp.exp(lam * dt)
    f = (lam_bar - 1.0) / lam
    prm['s5_lr'] = jnp.real(lam_bar).reshape(2, 4, 1, WG)
    prm['s5_li'] = jnp.imag(lam_bar).reshape(2, 4, 1, WG)
    c_c = lax.complex(p['s5_c_re'][l], p['s5_c_im'][l])
    cms = []
    for d in range(2):
        e = c_c * f[d][:, None, :]
        er = jnp.swapaxes(jnp.real(e), 1, 2)
        ei = jnp.swapaxes(jnp.imag(e), 1, 2)
        cm = []
        for j in range(4):
            sl = slice(8 * j, 8 * j + 8)
            cm.append(jnp.concatenate([_block_diag(er[sl]), -_block_diag(ei[sl])], axis=0))
        cms.append(jnp.stack(cm))
    prm['s5_cm'] = jnp.stack(cms).astype(BF16)
    prm['s5_d'] = p['s5_d'][l].reshape(1, WG)
    prm['s5_glu_w'] = p['s5_glu_w'][l].astype(BF16)
    prm['s5_glu_b'] = p['s5_glu_b'][l].reshape(1, WG)
    prm['ssd_conv_w'] = p['ssd_conv_w'][l]
    prm['ssd_conv_b'] = p['ssd_conv_b'][l].reshape(1, -1)
    pad = lambda v: jnp.concatenate([v.reshape(-1), jnp.zeros((128 - v.size,), F32)]).reshape(1, 128)
    prm['ssd_dtb'] = pad(p['ssd_dt_bias'][l])
    prm['ssd_arow'] = pad(-jnp.exp(p['ssd_a_log'][l]))
    prm['ssd_drow'] = jnp.repeat(p['ssd_d'][l], SSD_P).reshape(1, WG)
    prm['ssd_norm_g'] = p['ssd_norm_g'][l].reshape(1, WG)
    prm['lru_conv_w'] = p['lru_conv_w'][l]
    prm['lru_conv_b'] = p['lru_conv_b'][l].reshape(1, WG)
    prm['lru_wa'] = jnp.stack([_block_diag(p['lru_wa'][l][d]) for d in range(2)]).astype(BF16)
    prm['lru_wx'] = jnp.stack([_block_diag(p['lru_wx'][l][d]) for d in range(2)]).astype(BF16)
    prm['lru_ba'] = p['lru_ba'][l].reshape(2, 1, WG)
    prm['lru_bx'] = p['lru_bx'][l].reshape(2, 1, WG)
    prm['lru_sp'] = jax.nn.softplus(-p['lru_lam'][l]).reshape(2, 1, WG)
    prm['ml_gb'] = jnp.concatenate([jnp.zeros((SM_GATE,), F32), p['ml_gate_b'][l].reshape(-1),
                                    jnp.zeros((128 - SM_GATE - 4 * ML_H,), F32)]).reshape(1, 128)
    prm['ml_norm_g'] = p['ml_norm_g'][l].reshape(1, WG)
    prm['w_out'] = p['w_out'][l].astype(BF16)
    for k in ('ln1_g', 'ln1_b', 'ln2_g', 'ln2_b'):
        prm[k] = p[k][l].reshape(1, D)
    prm['router_w'] = p['router_w'][l]
    prm['router_b'] = p['router_b'][l].reshape(1, N_EXP)
    prm['moe_wgu'] = _wperm_call(p['moe_w_gate_up'][l])
    bgu = p['moe_b_gate_up'][l]
    prm['moe_bgu'] = jnp.concatenate([bgu[:, 0::2], bgu[:, 1::2]], axis=1).reshape(N_EXP, 1, 2 * D_FF)
    prm['moe_wd'] = p['moe_w_down'][l].astype(BF16)
    prm['moe_bd'] = p['moe_b_down'][l].reshape(N_EXP, 1, D)
    prm['ada_w'] = p['ada_w'][l].astype(BF16)
    prm['ada_b'] = p['ada_b'][l].reshape(1, 6 * D)
    return prm


def _mixers(flags, proj, prm):
    ya = _s5_call(flags, proj, None, prm, 0, reverse=False, final=False)
    ya = _s5_call(flags, proj, ya, prm, 1, reverse=True, final=True)
    yb = _ssd_call(flags, proj, None, prm, reverse=False, final=False)
    yb = _ssd_call(flags, proj, yb, prm, reverse=True, final=True)
    yc = _lru_call(flags, proj, None, prm, 0, reverse=False, final=False)
    yc = _lru_call(flags, proj, yc, prm, 1, reverse=True, final=True)
    yd = _mlstm_call(flags, proj, None, prm, reverse=False, final=False)
    yd = _mlstm_call(flags, proj, yd, prm, reverse=True, final=True)
    return ya, yb, yc, yd


def _to_rows(x):
    b, L, _ = x.shape
    return x.reshape(b // NB, NB, L, D).transpose(0, 2, 1, 3).reshape(b * L, D)


def _from_rows(r, b, L):
    return r.reshape(b // NB, L, NB, D).transpose(0, 2, 1, 3).reshape(b, L, D)


def _trunk(xs, cs, p, depth, moe_ts):
    first, last, gid = [], [], []
    g = 0
    for x in xs:
        b, L, _ = x.shape
        assert b % NB == 0 and L % CH == 0
        n = L // CH
        for _ in range(b // NB):
            first += [1] + [0] * (n - 1)
            last += [0] * (n - 1) + [1]
            gid += [g] * n
            g += 1
    first = jnp.asarray(first, jnp.int32)
    last = jnp.asarray(last, jnp.int32)
    gid = jnp.asarray(gid, jnp.int32)
    flags = (first, last)
    x = jnp.concatenate([_to_rows(x.astype(F32)) for x in xs], axis=0)
    c_all = jnp.concatenate([c.astype(F32) for c in cs], axis=0)
    R = x.shape[0]
    ts = min(moe_ts, R)
    assert R % ts == 0
    for l in range(depth):
        prm = _prep_layer(p, l)
        mod3 = _mod_call(c_all, prm['ada_w'], prm['ada_b']).reshape(g, NB, 6 * D)
        proj = _inproj_call(gid, x, mod3, prm['w_in'])
        ys = _mixers(flags, proj, prm)
        x1, h2, topi, topw = _outproj_call(gid, ys, x, mod3, prm)
        bexp, nblk, idx, wp = _moe_plan(topi, topw, ts)
        y = _moe_call(bexp, nblk, idx, wp, h2, prm, ts)
        x = _ln2_call(gid, x1, y, mod3, prm)
    outs = []
    r0 = 0
    for xin in xs:
        b, L, _ = xin.shape
        outs.append(_from_rows(x[r0:r0 + b * L], b, L).astype(xin.dtype))
        r0 += b * L
    return outs


def kernel(x_prompt, x_sample, c_prompt, c_sample, ada_w, ada_b, w_in, s5_lam_re, s5_lam_im, s5_log_dt, s5_b_re, s5_b_im, s5_c_re, s5_c_im, s5_d, s5_glu_w, s5_glu_b, ssd_conv_w, ssd_conv_b, ssd_a_log, ssd_dt_bias, ssd_d, ssd_norm_g, lru_conv_w, lru_conv_b, lru_wa, lru_ba, lru_wx, lru_bx, lru_lam, ml_gate_b, ml_norm_g, w_out, ln1_g, ln1_b, router_w, router_b, moe_w_gate_up, moe_b_gate_up, moe_w_down, moe_b_down, ln2_g, ln2_b):
    p = {
        'ada_w': ada_w, 'ada_b': ada_b, 'w_in': w_in,
        's5_lam_re': s5_lam_re, 's5_lam_im': s5_lam_im, 's5_log_dt': s5_log_dt,
        's5_b_re': s5_b_re, 's5_b_im': s5_b_im, 's5_c_re': s5_c_re, 's5_c_im': s5_c_im,
        's5_d': s5_d, 's5_glu_w': s5_glu_w, 's5_glu_b': s5_glu_b,
        'ssd_conv_w': ssd_conv_w, 'ssd_conv_b': ssd_conv_b, 'ssd_a_log': ssd_a_log,
        'ssd_dt_bias': ssd_dt_bias, 'ssd_d': ssd_d, 'ssd_norm_g': ssd_norm_g,
        'lru_conv_w': lru_conv_w, 'lru_conv_b': lru_conv_b, 'lru_wa': lru_wa, 'lru_ba': lru_ba,
        'lru_wx': lru_wx, 'lru_bx': lru_bx, 'lru_lam': lru_lam,
        'ml_gate_b': ml_gate_b, 'ml_norm_g': ml_norm_g, 'w_out': w_out,
        'ln1_g': ln1_g, 'ln1_b': ln1_b, 'router_w': router_w, 'router_b': router_b,
        'moe_w_gate_up': moe_w_gate_up, 'moe_b_gate_up': moe_b_gate_up,
        'moe_w_down': moe_w_down, 'moe_b_down': moe_b_down, 'ln2_g': ln2_g, 'ln2_b': ln2_b,
    }
    y_prompt, y_sample = _trunk([x_prompt, x_sample], [c_prompt, c_sample], p, DEPTH, MOE_TS)
    return (y_prompt, y_sample)
```

```python
import functools
import math

import numpy as np
import jax
import jax.numpy as jnp
from jax import lax
from jax.experimental import pallas as pl
from jax.experimental.pallas import tpu as pltpu

F32 = jnp.float32
BF16 = jnp.bfloat16
HIGHEST = lax.Precision.HIGHEST

D = 1024
DEPTH = 4
WG = 512
S5_G, S5_CH, S5_N = 32, 16, 64
SSD_H, SSD_P, SSD_NG, SSD_N = 8, 64, 2, 64
LRU_NB, LRU_BD, LRU_C = 8, 64, 8.0
ML_H, ML_DQK, ML_DV = 4, 64, 128
N_EXP, TOP_K, D_FF = 32, 4, 1024
SWIGLU_LIMIT, SWIGLU_ALPHA = 7.0, 1.702
ALPHA_DN = (2.0 * DEPTH) ** 0.25
LN_EPS, RMS_EPS = 1e-5, 1e-6

LANE = 128
NB = 8
CH = 128
ROWS = CH * NB
PW = 4480
K1_TN = 640
K3_TM = 512
MOE_RB = 256
MOE_FC = 256
MOE_UNROLL = 8
MOE_TS = 4096
VMEM_LIMIT = 56 * 1024 * 1024

COL_U, COL_XS, COL_Z, COL_XL, COL_GL, COL_V, COL_O = 0, 1, 2, 3, 4, 5, 6
COL_Q, COL_K, COL_BC = 14, 15, 16
COL_SM = 34
SM_DT, SM_GATE = 0, 16


def _sigmoid(x):
    return 1.0 / (1.0 + jnp.exp(-x))


def _silu(x):
    return x * _sigmoid(x)


def _softplus(x):
    return jnp.maximum(x, 0.0) + jnp.log(1.0 + jnp.exp(-jnp.abs(x)))


def _gelu_tanh(x):
    return 0.5 * x * (1.0 + jnp.tanh(math.sqrt(2.0 / math.pi) * (x + 0.044715 * (x * x * x))))


def _layer_norm(v, g, b):
    mu = jnp.mean(v, axis=-1, keepdims=True)
    vc = v - mu
    var = jnp.mean(vc * vc, axis=-1, keepdims=True)
    return vc * lax.rsqrt(var + LN_EPS) * g + b


def _bdot(a, b):
    return jnp.dot(a.astype(BF16), b.astype(BF16), preferred_element_type=F32)


def _per_slab(x, m):
    n = x.shape[0] // NB
    return (x.reshape(n, NB, x.shape[1]) * m[None]).reshape(x.shape)


def _per_slab_add(x, m):
    n = x.shape[0] // NB
    return (x.reshape(n, NB, x.shape[1]) + m[None]).reshape(x.shape)


def _cparams(sem):
    return pltpu.CompilerParams(dimension_semantics=sem, vmem_limit_bytes=VMEM_LIMIT)


def _mod_kernel(c_ref, w_ref, b_ref, o_ref):
    o_ref[...] = _bdot(_silu(c_ref[...]), w_ref[...]) + b_ref[...]


def _mod_call(c_all, w, b):
    n = c_all.shape[0]
    return pl.pallas_call(
        _mod_kernel,
        grid=(6,),
        in_specs=[pl.BlockSpec((n, D), lambda j: (0, 0)),
                  pl.BlockSpec((D, D), lambda j: (0, j)),
                  pl.BlockSpec((1, D), lambda j: (0, j))],
        out_specs=pl.BlockSpec((n, D), lambda j: (0, j)),
        out_shape=jax.ShapeDtypeStruct((n, 6 * D), F32),
        compiler_params=_cparams(("arbitrary",)),
    )(c_all, w, b)


def _inproj_kernel(gid_ref, x_ref, sc_ref, sh_ref, w_ref, o_ref, h_s):
    @pl.when(pl.program_id(1) == 0)
    def _():
        h = _per_slab_add(_per_slab(x_ref[...], 1.0 + sc_ref[0]), sh_ref[0])
        h_s[...] = h.astype(BF16)

    res = jnp.dot(h_s[...], w_ref[...], preferred_element_type=F32)
    for k in range(K1_TN // LANE):
        o_ref[k] = res[:, LANE * k:LANE * (k + 1)]


def _inproj_ln_kernel(gid_ref, x1_ref, y_ref, g2_ref, lg_ref, lb_ref, sc_ref, sh_ref, w_ref, o_ref, xo_ref, h_s):
    @pl.when(pl.program_id(1) == 0)
    def _():
        v = ALPHA_DN * x1_ref[...] + _per_slab(y_ref[0], g2_ref[0])
        x = _layer_norm(v, lg_ref[...], lb_ref[...])
        xo_ref[...] = x
        h = _per_slab_add(_per_slab(x, 1.0 + sc_ref[0]), sh_ref[0])
        h_s[...] = h.astype(BF16)

    res = jnp.dot(h_s[...], w_ref[...], preferred_element_type=F32)
    for k in range(K1_TN // LANE):
        o_ref[k] = res[:, LANE * k:LANE * (k + 1)]


def _inproj_ln_call(gid, x1, y, mod3_prev, prm_prev, mod3, w_in_p):
    R = x1.shape[0]
    nch = R // ROWS
    per = (y.shape[1] - MOE_UNROLL) // ROWS
    cspec = pl.BlockSpec((1, D), lambda i, j, gid: (0, 0))
    gs = pltpu.PrefetchScalarGridSpec(
        num_scalar_prefetch=1,
        grid=(nch, PW // K1_TN),
        in_specs=[pl.BlockSpec((ROWS, D), lambda i, j, gid: (i, 0)),
                  pl.BlockSpec((1, ROWS, D), lambda i, j, gid: (i // per, i % per, 0)),
                  pl.BlockSpec((1, NB, D), lambda i, j, gid: (gid[i], 0, 5)),
                  cspec, cspec,
                  pl.BlockSpec((1, NB, D), lambda i, j, gid: (gid[i], 0, 1)),
                  pl.BlockSpec((1, NB, D), lambda i, j, gid: (gid[i], 0, 0)),
                  pl.BlockSpec((D, K1_TN), lambda i, j, gid: (0, j))],
        out_specs=[pl.BlockSpec((K1_TN // LANE, ROWS, LANE), lambda i, j, gid: (j, i, 0)),
                   pl.BlockSpec((ROWS, D), lambda i, j, gid: (i, 0))],
        scratch_shapes=[pltpu.VMEM((ROWS, D), BF16)],
    )
    return pl.pallas_call(
        _inproj_ln_kernel, grid_spec=gs,
        out_shape=[jax.ShapeDtypeStruct((PW // LANE, R, LANE), F32), jax.ShapeDtypeStruct((R, D), F32)],
        compiler_params=_cparams(("arbitrary", "arbitrary")),
    )(gid, x1, y, mod3_prev, prm_prev['ln2_g'], prm_prev['ln2_b'], mod3, mod3, w_in_p)


def _inproj_call(gid, x, mod3, w_in_p):
    R = x.shape[0]
    nch = R // ROWS
    gs = pltpu.PrefetchScalarGridSpec(
        num_scalar_prefetch=1,
        grid=(nch, PW // K1_TN),
        in_specs=[pl.BlockSpec((ROWS, D), lambda i, j, gid: (i, 0)),
                  pl.BlockSpec((1, NB, D), lambda i, j, gid: (gid[i], 0, 1)),
                  pl.BlockSpec((1, NB, D), lambda i, j, gid: (gid[i], 0, 0)),
                  pl.BlockSpec((D, K1_TN), lambda i, j, gid: (0, j))],
        out_specs=pl.BlockSpec((K1_TN // LANE, ROWS, LANE), lambda i, j, gid: (j, i, 0)),
        scratch_shapes=[pltpu.VMEM((ROWS, D), BF16)],
    )
    return pl.pallas_call(
        _inproj_kernel, grid_spec=gs,
        out_shape=jax.ShapeDtypeStruct((PW // LANE, R, LANE), F32),
        compiler_params=_cparams(("arbitrary", "arbitrary")),
    )(gid, x, mod3, mod3, w_in_p)


def _chunk_idx(nch, reverse):
    return (lambda c: nch - 1 - c) if reverse else (lambda c: c)


def _main_spec(width, col, nch, reverse):
    ci = _chunk_idx(nch, reverse)
    return pl.BlockSpec((width // LANE, ROWS, LANE), lambda c, *_: (col, ci(c), 0))


def _prev_spec(width, col, nch, reverse):
    ci = _chunk_idx(nch, reverse)
    per = ROWS // (2 * NB)
    return pl.BlockSpec((width // LANE, 2 * NB, LANE), lambda c, *_: (col, jnp.maximum(ci(c) * per - 1, 0), 0))


def _next_spec(width, col, nch, reverse):
    ci = _chunk_idx(nch, reverse)
    per = ROWS // NB
    return pl.BlockSpec((width // LANE, NB, LANE),
                        lambda c, *_: (col, jnp.minimum((ci(c) + 1) * per, nch * per - 1), 0))


def _const_spec(shape):
    nd = len(shape)
    return pl.BlockSpec(shape, lambda c, *_: (0,) * nd)


def _cat(ref):
    return jnp.concatenate([ref[k] for k in range(ref.shape[0])], axis=1)


def _put(ref, val):
    for k in range(ref.shape[0]):
        ref[k] = val[:, LANE * k:LANE * (k + 1)]


def _fill_ext(ext_s, k0, x_ref, xp_ref, xn_ref, is_first, is_last):
    for k in range(x_ref.shape[0]):
        ext_s[k0 + k, 0:2 * NB, :] = jnp.where(is_first, 0.0, xp_ref[k])
        ext_s[k0 + k, 2 * NB:2 * NB + ROWS, :] = x_ref[k]
        ext_s[k0 + k, 2 * NB + ROWS:3 * NB + ROWS, :] = jnp.where(is_last, 0.0, xn_ref[k])


def _dwconv(ext_s, k, w_ref, b_ref):
    sl = slice(LANE * k, LANE * (k + 1))
    out = b_ref[:, sl] + w_ref[0:1, sl] * ext_s[k, 0:ROWS, :]
    for j in range(1, 4):
        out = out + w_ref[j:j + 1, sl] * ext_s[k, j * NB:j * NB + ROWS, :]
    return out


def _tri(reverse):
    r = lax.broadcasted_iota(jnp.int32, (CH, CH), 0)
    c = lax.broadcasted_iota(jnp.int32, (CH, CH), 1)
    return (c >= r) if reverse else (c <= r)


def _s5_kernel(*refs, nch, reverse, final):
    first_ref, last_ref = refs[0], refs[1]
    if final:
        (u_ref, yp_ref, bt_ref, lr_ref, li_ref, cm_ref, d_ref, gw_ref, gb_ref,
         o_ref, bu_s, st_s, carry_s) = refs[2:]
    else:
        u_ref, bt_ref, lr_ref, li_ref, cm_ref, o_ref, bu_s, st_s, carry_s = refs[2:]
    c = pl.program_id(0)
    cc = nch - 1 - c if reverse else c
    start = last_ref[cc] if reverse else first_ref[cc]

    @pl.when(start == 1)
    def _():
        carry_s[...] = jnp.zeros_like(carry_s)

    half = WG
    for j in range(4):
        bu_s[...] = jnp.dot(u_ref[j].astype(BF16), bt_ref[j], preferred_element_type=F32)
        lr = jnp.broadcast_to(lr_ref[j], (NB, half))
        li = jnp.broadcast_to(li_ref[j], (NB, half))

        def step(i, carry, lr=lr, li=li):
            sr, si = carry
            t = CH - 1 - i if reverse else i
            r0 = pl.multiple_of(t * NB, NB)
            br = bu_s[pl.ds(r0, NB), 0:half]
            bi = bu_s[pl.ds(r0, NB), half:2 * half]
            nr = lr * sr - li * si + br
            ni = lr * si + li * sr + bi
            st_s[pl.ds(r0, NB), 0:half] = nr
            st_s[pl.ds(r0, NB), half:2 * half] = ni
            return nr, ni

        lax.fori_loop(0, CH, step, (carry_s[j, :, 0:half], carry_s[j, :, half:2 * half]), unroll=8)
        e0 = (CH - 1) * NB if reverse else 0
        carry_s[j] = st_s[e0:e0 + NB, :]
        o_ref[j] = jnp.dot(st_s[...].astype(BF16), cm_ref[j], preferred_element_type=F32)

    if final:
        y = _cat(o_ref) + _cat(yp_ref) + d_ref[...] * _cat(u_ref)
        g = _gelu_tanh(y)
        _put(o_ref, g * _sigmoid(_bdot(g, gw_ref[...]) + gb_ref[...]))


def _s5_call(flags, proj, yprev, prm, d, *, reverse, final):
    first, last = flags
    R = proj.shape[1]
    nch = R // ROWS
    in_specs = [_main_spec(WG, COL_U, nch, reverse)]
    args = [proj]
    if final:
        in_specs.append(_main_spec(WG, 0, nch, reverse))
        args.append(yprev)
    in_specs += [_const_spec((4, 128, 2 * WG)), _const_spec((4, 1, WG)), _const_spec((4, 1, WG)),
                 _const_spec((4, 2 * WG, 128))]
    args += [prm['s5_bt'], prm['s5_lr'][d], prm['s5_li'][d], prm['s5_cm'][d]]
    if final:
        in_specs += [_const_spec((1, WG)), _const_spec((WG, WG)), _const_spec((1, WG))]
        args += [prm['s5_d'], prm['s5_glu_w'], prm['s5_glu_b']]
    gs = pltpu.PrefetchScalarGridSpec(
        num_scalar_prefetch=2, grid=(nch,), in_specs=in_specs,
        out_specs=_main_spec(WG, 0, nch, reverse),
        scratch_shapes=[pltpu.VMEM((ROWS, 2 * WG), F32), pltpu.VMEM((ROWS, 2 * WG), F32),
                        pltpu.VMEM((4, NB, 2 * WG), F32)],
    )
    return pl.pallas_call(
        functools.partial(_s5_kernel, nch=nch, reverse=reverse, final=final), grid_spec=gs,
        out_shape=jax.ShapeDtypeStruct((WG // LANE, R, LANE), F32),
        compiler_params=_cparams(("arbitrary",)),
    )(first, last, *args)


def _lru_kernel(*refs, nch, reverse, final):
    first_ref, last_ref = refs[0], refs[1]
    if final:
        (x_ref, xp_ref, xn_ref, gate_ref, hp_ref, cw_ref, cb_ref, wa_ref, ba_ref, wx_ref, bx_ref, sp_ref,
         o_ref, ext_s, a_s, inp_s, carry_s) = refs[2:]
    else:
        (x_ref, xp_ref, xn_ref, cw_ref, cb_ref, wa_ref, ba_ref, wx_ref, bx_ref, sp_ref,
         o_ref, ext_s, a_s, inp_s, carry_s) = refs[2:]
    c = pl.program_id(0)
    cc = nch - 1 - c if reverse else c
    is_first = first_ref[cc] == 1
    is_last = last_ref[cc] == 1
    start = is_last if reverse else is_first

    @pl.when(start)
    def _():
        carry_s[...] = jnp.zeros_like(carry_s)

    _fill_ext(ext_s, 0, x_ref, xp_ref, xn_ref, is_first, is_last)
    xc = jnp.concatenate([_dwconv(ext_s, k, cw_ref, cb_ref) for k in range(WG // LANE)], axis=1)
    xb = xc.astype(BF16)
    r = _sigmoid(jnp.dot(xb, wa_ref[...], preferred_element_type=F32) + ba_ref[...])
    i = _sigmoid(jnp.dot(xb, wx_ref[...], preferred_element_type=F32) + bx_ref[...])
    log_a = -LRU_C * r * sp_ref[...]
    a_s[...] = jnp.exp(log_a)
    inp_s[...] = jnp.sqrt(1.0 - jnp.exp(2.0 * log_a)) * (i * xc)

    def step(k, h):
        t = CH - 1 - k if reverse else k
        r0 = pl.multiple_of(t * NB, NB)
        h = a_s[pl.ds(r0, NB), :] * h + inp_s[pl.ds(r0, NB), :]
        for k in range(WG // LANE):
            o_ref[k, pl.ds(r0, NB), :] = h[:, LANE * k:LANE * (k + 1)]
        return h

    carry_s[...] = lax.fori_loop(0, CH, step, carry_s[...], unroll=8)
    if final:
        for k in range(WG // LANE):
            o_ref[k] = (o_ref[k] + hp_ref[k]) * _gelu_tanh(gate_ref[k])


def _lru_call(flags, proj, hprev, prm, d, *, reverse, final):
    first, last = flags
    R = proj.shape[1]
    nch = R // ROWS
    in_specs = [_main_spec(WG, COL_XL, nch, reverse), _prev_spec(WG, COL_XL, nch, reverse),
                _next_spec(WG, COL_XL, nch, reverse)]
    args = [proj, proj, proj]
    if final:
        in_specs += [_main_spec(WG, COL_GL, nch, reverse), _main_spec(WG, 0, nch, reverse)]
        args += [proj, hprev]
    in_specs += [_const_spec((4, WG)), _const_spec((1, WG)), _const_spec((WG, WG)), _const_spec((1, WG)),
                 _const_spec((WG, WG)), _const_spec((1, WG)), _const_spec((1, WG))]
    args += [prm['lru_conv_w'], prm['lru_conv_b'], prm['lru_wa'][d], prm['lru_ba'][d], prm['lru_wx'][d],
             prm['lru_bx'][d], prm['lru_sp'][d]]
    gs = pltpu.PrefetchScalarGridSpec(
        num_scalar_prefetch=2, grid=(nch,), in_specs=in_specs,
        out_specs=_main_spec(WG, 0, nch, reverse),
        scratch_shapes=[pltpu.VMEM((WG // LANE, ROWS + 3 * NB, LANE), F32), pltpu.VMEM((ROWS, WG), F32),
                        pltpu.VMEM((ROWS, WG), F32), pltpu.VMEM((NB, WG), F32)],
    )
    return pl.pallas_call(
        functools.partial(_lru_kernel, nch=nch, reverse=reverse, final=final), grid_spec=gs,
        out_shape=jax.ShapeDtypeStruct((WG // LANE, R, LANE), F32),
        compiler_params=_cparams(("arbitrary",)),
    )(first, last, *args)


def _ssd_kernel(*refs, nch, reverse, final):
    first_ref, last_ref = refs[0], refs[1]
    if final:
        (x_ref, xp_ref, xn_ref, bc_ref, bcp_ref, bcn_ref, sm_ref, z_ref, yp_ref,
         cw_ref, cb_ref, dtb_ref, arow_ref, drow_ref, ng_ref,
         o_ref, ext_s, xd_s, smd_s, yd_s, st_s) = refs[2:]
    else:
        (x_ref, xp_ref, xn_ref, bc_ref, bcp_ref, bcn_ref, sm_ref,
         cw_ref, cb_ref, dtb_ref, arow_ref,
         o_ref, ext_s, xd_s, smd_s, yd_s, st_s) = refs[2:]
    c = pl.program_id(0)
    cc = nch - 1 - c if reverse else c
    is_first = first_ref[cc] == 1
    is_last = last_ref[cc] == 1
    start = is_last if reverse else is_first

    @pl.when(start)
    def _():
        st_s[...] = jnp.zeros_like(st_s)

    CW = WG + 2 * SSD_NG * SSD_N
    _fill_ext(ext_s, 0, x_ref, xp_ref, xn_ref, is_first, is_last)
    _fill_ext(ext_s, WG // LANE, bc_ref, bcp_ref, bcn_ref, is_first, is_last)
    for k in range(CW // LANE):
        ext_s[k, 0:ROWS, :] = _silu(_dwconv(ext_s, k, cw_ref, cb_ref))
    for b in range(NB):
        for k in range(CW // LANE):
            xd_s[b, :, LANE * k:LANE * (k + 1)] = ext_s[k, pl.ds(b, CH, stride=NB), :]
        smd_s[b] = sm_ref[0, pl.ds(b, CH, stride=NB), :]

    mask = _tri(reverse)
    tri = jnp.where(mask, 1.0, 0.0)
    doff = SM_DT + (SSD_H if reverse else 0)
    edge = 0 if reverse else CH - 1
    NS = SSD_N
    lane_c = lax.broadcasted_iota(jnp.int32, (CH, CH), 1)

    def per_b(b, _):
        xb = xd_s[b]
        xs = xb[:, 0:WG]
        bm = xb[:, WG:WG + SSD_NG * NS]
        cm = xb[:, WG + SSD_NG * NS:CW]
        dtf = _softplus(smd_s[b] + dtb_ref[...])
        af = dtf * arow_ref[...]
        cs = jnp.dot(tri, af, precision=HIGHEST, preferred_element_type=F32)
        cs_t = cs.T
        dt_t = dtf.T
        bm_t = bm.T
        cmask = [jnp.where((lane_c // NS) == g, cm, 0.0).astype(BF16) for g in range(SSD_NG)]
        bm_tb = bm_t.astype(BF16)
        gmat = [jnp.dot(cmask[g], bm_tb, preferred_element_type=F32) for g in range(SSD_NG)]
        for hp in range(SSD_H // 2):
            g = (2 * hp) // (SSD_H // SSD_NG)
            xpair = xs[:, LANE * hp:LANE * (hp + 1)]
            s_prev = st_s[b, hp]
            y = None
            upd = None
            ecol = None
            etot = None
            for k in range(2):
                ci = doff + 2 * hp + k
                mine = (lane_c // SSD_P) == k
                col_b = jnp.broadcast_to(cs[:, ci:ci + 1], (CH, CH))
                row = cs_t[ci:ci + 1, :]
                lmat = jnp.where(mask, jnp.exp(jnp.minimum(col_b - row, 0.0)), 0.0)
                m = gmat[g] * lmat * dt_t[ci:ci + 1, :]
                xk = jnp.where(mine, xpair, 0.0).astype(BF16)
                yk = jnp.dot(m.astype(BF16), xk, preferred_element_type=F32)
                tot = cs[edge:edge + 1, ci:ci + 1]
                wrow = jnp.exp(tot - row) * dt_t[ci:ci + 1, :]
                uk = jnp.dot((bm_t[NS * g:NS * (g + 1), :] * wrow).astype(BF16), xk, preferred_element_type=F32)
                y = yk if k == 0 else y + yk
                upd = uk if k == 0 else upd + uk
                ecol = jnp.exp(col_b) if k == 0 else jnp.where(mine, jnp.exp(col_b), ecol)
                etot = jnp.exp(tot) if k == 0 else jnp.where(mine[0:1, :], jnp.exp(tot), etot)
            s2 = jnp.concatenate([s_prev, s_prev], axis=0).astype(BF16)
            y = y + ecol * jnp.dot(cmask[g], s2, preferred_element_type=F32)
            st_s[b, hp] = etot * s_prev + upd
            yd_s[b, :, LANE * hp:LANE * (hp + 1)] = y
        return 0

    lax.fori_loop(0, NB, per_b, 0)
    for b in range(NB):
        for k in range(WG // LANE):
            o_ref[k, pl.ds(b, CH, stride=NB), :] = yd_s[b, :, LANE * k:LANE * (k + 1)]
    if final:
        xs_all = jnp.concatenate([ext_s[k, 0:ROWS, :] for k in range(WG // LANE)], axis=1)
        y = (_cat(o_ref) + _cat(yp_ref) + drow_ref[...] * xs_all) * _silu(_cat(z_ref))
        ms = jnp.mean(y * y, axis=-1, keepdims=True)
        _put(o_ref, y * lax.rsqrt(ms + RMS_EPS) * ng_ref[...])


def _ssd_call(flags, proj, yprev, prm, *, reverse, final):
    first, last = flags
    R = proj.shape[1]
    nch = R // ROWS
    CW = WG + 2 * SSD_NG * SSD_N
    in_specs = [_main_spec(WG, COL_XS, nch, reverse), _prev_spec(WG, COL_XS, nch, reverse),
                _next_spec(WG, COL_XS, nch, reverse),
                _main_spec(256, COL_BC, nch, reverse), _prev_spec(256, COL_BC, nch, reverse),
                _next_spec(256, COL_BC, nch, reverse),
                _main_spec(128, COL_SM, nch, reverse)]
    args = [proj] * 7
    if final:
        in_specs += [_main_spec(WG, COL_Z, nch, reverse), _main_spec(WG, 0, nch, reverse)]
        args += [proj, yprev]
    in_specs += [_const_spec((4, CW)), _const_spec((1, CW)), _const_spec((1, 128)), _const_spec((1, 128))]
    args += [prm['ssd_conv_w'], prm['ssd_conv_b'], prm['ssd_dtb'], prm['ssd_arow']]
    if final:
        in_specs += [_const_spec((1, WG)), _const_spec((1, WG))]
        args += [prm['ssd_drow'], prm['ssd_norm_g']]
    gs = pltpu.PrefetchScalarGridSpec(
        num_scalar_prefetch=2, grid=(nch,), in_specs=in_specs,
        out_specs=_main_spec(WG, 0, nch, reverse),
        scratch_shapes=[pltpu.VMEM((CW // LANE, ROWS + 3 * NB, LANE), F32), pltpu.VMEM((NB, CH, CW), F32),
                        pltpu.VMEM((NB, CH, 128), F32), pltpu.VMEM((NB, CH, WG), F32),
                        pltpu.VMEM((NB, SSD_H // 2, SSD_N, 2 * SSD_P), F32)],
    )
    return pl.pallas_call(
        functools.partial(_ssd_kernel, nch=nch, reverse=reverse, final=final), grid_spec=gs,
        out_shape=jax.ShapeDtypeStruct((WG // LANE, R, LANE), F32),
        compiler_params=_cparams(("arbitrary",)),
    )(first, last, *args)


NEG = -1e30


def _mlstm_kernel(*refs, nch, reverse, final):
    first_ref, last_ref = refs[0], refs[1]
    if final:
        (q_ref, k_ref, v_ref, sm_ref, og_ref, hp_ref, gb_ref, ng_ref,
         o_ref, qd_s, kd_s, vd_s, smd_s, hd_s, c_s, n_s, m_s) = refs[2:]
    else:
        (q_ref, k_ref, v_ref, sm_ref, gb_ref,
         o_ref, qd_s, kd_s, vd_s, smd_s, hd_s, c_s, n_s, m_s) = refs[2:]
    c = pl.program_id(0)
    cc = nch - 1 - c if reverse else c
    start = (last_ref[cc] if reverse else first_ref[cc]) == 1

    @pl.when(start)
    def _():
        c_s[...] = jnp.zeros_like(c_s)
        n_s[...] = jnp.zeros_like(n_s)
        m_s[...] = jnp.zeros_like(m_s)

    for b in range(NB):
        for src, dst in ((q_ref, qd_s), (k_ref, kd_s), (v_ref, vd_s)):
            for k in range(src.shape[0]):
                dst[b, :, LANE * k:LANE * (k + 1)] = src[k, pl.ds(b, CH, stride=NB), :]
        smd_s[b] = sm_ref[0, pl.ds(b, CH, stride=NB), :]

    mask = _tri(reverse)
    tri = jnp.where(mask, 1.0, 0.0)
    goff = SM_GATE + (2 * ML_H if reverse else 0)
    edge = 0 if reverse else CH - 1
    ones_b = jnp.ones((CH, CH), BF16)
    lane_c = lax.broadcasted_iota(jnp.int32, (CH, CH), 1)

    def per_b(b, _):
        sm = smd_s[b] + gb_ref[...]
        logf = -_softplus(-sm)
        fc_all = jnp.dot(tri, logf, precision=HIGHEST, preferred_element_type=F32)
        fr_all = fc_all.T
        sm_t = sm.T
        qb = qd_s[b] * (ML_DQK ** -0.5)
        kb_t = kd_s[b].T
        vb = vd_s[b]
        for h in range(ML_H):
            ci = goff + h
            cf = goff + ML_H + h
            fc = fc_all[:, cf:cf + 1]
            ic = sm[:, ci:ci + 1]
            fr = fr_all[cf:cf + 1, :]
            ir = sm_t[ci:ci + 1, :]
            m_prev = m_s[b, h]
            u = jnp.where(mask, ir - fr, NEG)
            g = jnp.maximum(jnp.max(u, axis=1, keepdims=True), m_prev)
            g_b = jnp.broadcast_to(g, (CH, CH))
            wts = jnp.exp(u - g_b)
            w_inter = jnp.exp(m_prev - g_b)
            hp, k = h // 2, h % 2
            qh = jnp.where((lane_c // ML_DQK) == k, qb[:, LANE * hp:LANE * (hp + 1)], 0.0).astype(BF16)
            kh_t = kb_t[ML_DQK * h:ML_DQK * (h + 1), :]
            vh = vb[:, ML_DV * h:ML_DV * (h + 1)]
            c_pair = c_s[b, hp]
            n_pair = n_s[b, hp]
            ct_prev = c_pair[ML_DQK * k:ML_DQK * (k + 1), :]
            n_prev = n_pair[ML_DQK * k:ML_DQK * (k + 1), :]
            sb = (jnp.dot(qh, kb_t[LANE * hp:LANE * (hp + 1), :].astype(BF16), preferred_element_type=F32)
                  * wts).astype(BF16)
            num = _bdot(sb, vh) + w_inter * _bdot(qh, c_pair)
            den = jnp.dot(sb, ones_b, preferred_element_type=F32) + w_inter * _bdot(qh, n_pair)
            hd_s[b, :, ML_DV * h:ML_DV * (h + 1)] = num / jnp.maximum(jnp.abs(den), jnp.exp(-(fc + g)))
            f_end = fc_all[edge:edge + 1, cf:cf + 1]
            w_log = f_end - fc + ic
            m_new = jnp.maximum(f_end + m_prev, jnp.max(w_log, axis=0, keepdims=True))
            w_s = jnp.broadcast_to(jnp.exp(w_log - m_new), (CH, CH))
            scale = jnp.exp(f_end + m_prev - m_new)
            c_s[b, hp, ML_DQK * k:ML_DQK * (k + 1), :] = scale * ct_prev + _bdot(kh_t, vh * w_s)
            n_s[b, hp, ML_DQK * k:ML_DQK * (k + 1), :] = scale * n_prev + _bdot(kh_t, w_s)
            m_s[b, h] = m_new
        return 0

    lax.fori_loop(0, NB, per_b, 0)
    for b in range(NB):
        for k in range(WG // LANE):
            o_ref[k, pl.ds(b, CH, stride=NB), :] = hd_s[b, :, LANE * k:LANE * (k + 1)]
    if final:
        assert ML_DV == LANE
        for h in range(ML_H):
            hh = o_ref[h] + hp_ref[h]
            ms = jnp.mean(hh * hh, axis=-1, keepdims=True)
            o_ref[h] = hh * lax.rsqrt(ms + RMS_EPS) * ng_ref[:, LANE * h:LANE * (h + 1)] * _sigmoid(og_ref[h])


def _mlstm_call(flags, proj, hprev, prm, *, reverse, final):
    first, last = flags
    R = proj.shape[1]
    nch = R // ROWS
    in_specs = [_main_spec(256, COL_Q, nch, reverse), _main_spec(256, COL_K, nch, reverse),
                _main_spec(WG, COL_V, nch, reverse), _main_spec(128, COL_SM, nch, reverse)]
    args = [proj] * 4
    if final:
        in_specs += [_main_spec(WG, COL_O, nch, reverse), _main_spec(WG, 0, nch, reverse)]
        args += [proj, hprev]
    in_specs += [_const_spec((1, 128))]
    args += [prm['ml_gb']]
    if final:
        in_specs += [_const_spec((1, WG))]
        args += [prm['ml_norm_g']]
    gs = pltpu.PrefetchScalarGridSpec(
        num_scalar_prefetch=2, grid=(nch,), in_specs=in_specs,
        out_specs=_main_spec(WG, 0, nch, reverse),
        scratch_shapes=[pltpu.VMEM((NB, CH, 256), F32), pltpu.VMEM((NB, CH, 256), F32),
                        pltpu.VMEM((NB, CH, WG), F32), pltpu.VMEM((NB, CH, 128), F32),
                        pltpu.VMEM((NB, CH, WG), F32),
                        pltpu.VMEM((NB, ML_H // 2, 2 * ML_DQK, ML_DV), F32),
                        pltpu.VMEM((NB, ML_H // 2, 2 * ML_DQK, CH), F32),
                        pltpu.VMEM((NB, ML_H, 1, 1), F32)],
    )
    return pl.pallas_call(
        functools.partial(_mlstm_kernel, nch=nch, reverse=reverse, final=final), grid_spec=gs,
        out_shape=jax.ShapeDtypeStruct((WG // LANE, R, LANE), F32),
        compiler_params=_cparams(("arbitrary",)),
    )(first, last, *args)


def _outproj_kernel(gid_ref, ya_ref, yb_ref, yc_ref, yd_ref, x_ref, g1_ref, sc_ref, sh_ref, wo_ref,
                    lg_ref, lb_ref, rw_ref, rb_ref, x1_ref, h2_ref, ti_ref, tw_ref):
    o = _bdot(_cat(ya_ref), wo_ref[0:WG, :])
    o = o + _bdot(_cat(yb_ref), wo_ref[WG:2 * WG, :])
    o = o + _bdot(_cat(yc_ref), wo_ref[2 * WG:3 * WG, :])
    o = o + _bdot(_cat(yd_ref), wo_ref[3 * WG:4 * WG, :])
    v = ALPHA_DN * x_ref[...] + _per_slab(o, g1_ref[0])
    x1 = _layer_norm(v, lg_ref[...], lb_ref[...])
    x1_ref[...] = x1
    h2 = _per_slab_add(_per_slab(x1, 1.0 + sc_ref[0]), sh_ref[0])
    bits = lax.bitcast_convert_type(h2.astype(BF16).astype(F32), jnp.uint32)
    h2_ref[...] = (bits[:, 0:D // 2] & jnp.uint32(0xFFFF0000)) | (bits[:, D // 2:D] >> jnp.uint32(16))
    logit = _bdot(h2, rw_ref[...]) + rb_ref[...]
    tm = logit.shape[0]
    lane = lax.broadcasted_iota(jnp.int32, (tm, N_EXP), 1)
    lane8 = lax.broadcasted_iota(jnp.int32, (tm, 8), 1)
    idx8 = jnp.zeros((tm, 8), jnp.int32)
    val8 = jnp.zeros((tm, 8), F32)
    top0 = None
    den = None
    for k in range(TOP_K):
        mx = jnp.max(logit, axis=1, keepdims=True)
        sel = jnp.min(jnp.where(logit == mx, lane, N_EXP), axis=1, keepdims=True)
        if k == 0:
            top0 = mx
        ek = jnp.exp(mx - top0)
        den = ek if k == 0 else den + ek
        idx8 = jnp.where(lane8 == k, sel, idx8)
        val8 = jnp.where(lane8 == k, ek, val8)
        logit = jnp.where(lane == sel, -jnp.inf, logit)
    ti_ref[...] = idx8
    tw_ref[...] = val8 / den


def _outproj_call(gid3, ys, x, mod3, prm):
    R = x.shape[0]
    per = ROWS // K3_TM
    rspec = lambda w: pl.BlockSpec((K3_TM, w), lambda i, gid: (i, 0))
    yspec = pl.BlockSpec((WG // LANE, K3_TM, LANE), lambda i, gid: (0, i, 0))
    mspec = lambda k: pl.BlockSpec((1, NB, D), lambda i, gid: (gid[i // per], 0, k))
    cspec = lambda shape: pl.BlockSpec(shape, lambda i, gid: (0,) * len(shape))
    gs = pltpu.PrefetchScalarGridSpec(
        num_scalar_prefetch=1, grid=(R // K3_TM,),
        in_specs=[yspec, yspec, yspec, yspec, rspec(D), mspec(2), mspec(4), mspec(3),
                  cspec((4 * WG, D)), cspec((1, D)), cspec((1, D)), cspec((D, N_EXP)), cspec((1, N_EXP))],
        out_specs=[rspec(D), rspec(D // 2), rspec(8), rspec(8)],
    )
    return pl.pallas_call(
        _outproj_kernel, grid_spec=gs,
        out_shape=[jax.ShapeDtypeStruct((R, D), F32), jax.ShapeDtypeStruct((R, D // 2), jnp.uint32),
                   jax.ShapeDtypeStruct((R, 8), jnp.int32), jax.ShapeDtypeStruct((R, 8), F32)],
        compiler_params=_cparams(("arbitrary",)),
    )(gid3, *ys, x, mod3, mod3, mod3, prm['w_out'], prm['ln1_g'], prm['ln1_b'], prm['router_w'], prm['router_b'])


def _moe_gather(idx_ref, h2_ref, dst, blk):
    s0 = (blk + 1) * MOE_RB
    for r in range(MOE_RB):
        tok = idx_ref[0, 0, s0 + r] & 0xFFFF
        dst[r:r + 1, :] = h2_ref[pl.ds(tok, 1), :]


def _moe_scatter(idx_ref, gw_ref, o_ref, ysrc, blk):
    s0 = (blk + 1) * MOE_RB
    for q in range(MOE_RB // MOE_UNROLL):
        toks, vals = [], []
        for j in range(MOE_UNROLL):
            r = q * MOE_UNROLL + j
            tok = idx_ref[0, 0, s0 + r] >> 16
            w = gw_ref[0, 0, s0 + r]
            toks.append(tok)
            vals.append(o_ref[0, pl.ds(tok, 1), :] + w * ysrc[r:r + 1, :])
        for tok, val in zip(toks, vals):
            o_ref[0, pl.ds(tok, 1), :] = val


def _moe_ffn(lhs, wgu_ref, wd_ref, bgu_ref, bd_ref, ydst):
    pk = lhs[...]
    lo = lax.bitcast_convert_type(pk << jnp.uint32(16), F32)
    hi = lax.bitcast_convert_type(pk & jnp.uint32(0xFFFF0000), F32)
    xb = jnp.concatenate([hi, lo], axis=1).astype(BF16)
    acc = None
    for c in range(D_FF // MOE_FC):
        lo_c, hi_c = MOE_FC * c, MOE_FC * (c + 1)
        g = jnp.dot(xb, wgu_ref[0, :, lo_c:hi_c], preferred_element_type=F32) + bgu_ref[0, :, lo_c:hi_c]
        u = (jnp.dot(xb, wgu_ref[0, :, D_FF + lo_c:D_FF + hi_c], preferred_element_type=F32)
             + bgu_ref[0, :, D_FF + lo_c:D_FF + hi_c])
        g = jnp.minimum(g, SWIGLU_LIMIT)
        u = jnp.clip(u, -SWIGLU_LIMIT, SWIGLU_LIMIT)
        hdn = ((u + 1.0) * g * _sigmoid(SWIGLU_ALPHA * g)).astype(BF16)
        part = jnp.dot(hdn, wd_ref[0, lo_c:hi_c, :], preferred_element_type=F32)
        acc = part if c == 0 else acc + part
    ydst[...] = acc + bd_ref[0]


def _moe_kernel(bexp_ref, nblk_ref, idx_ref, gw_ref, h2_ref, wgu_ref, wd_ref, bgu_ref, bd_ref,
                o_ref, la_s, lb_s, ya_s, yb_s):
    i = pl.program_id(0)
    s = pl.program_id(1)

    @pl.when(s == 0)
    def _():
        o_ref[...] = jnp.zeros_like(o_ref)
        yb_s[...] = jnp.zeros_like(yb_s)
        _moe_gather(idx_ref, h2_ref, la_s, s)

    def step(cur_l, nxt_l, cur_y, prv_y):
        _moe_scatter(idx_ref, gw_ref, o_ref, prv_y, s - 1)
        _moe_gather(idx_ref, h2_ref, nxt_l, s + 1)
        _moe_ffn(cur_l, wgu_ref, wd_ref, bgu_ref, bd_ref, cur_y)

    active = s <= nblk_ref[i]

    @pl.when(active & (s % 2 == 0))
    def _():
        step(la_s, lb_s, ya_s, yb_s)

    @pl.when(active & (s % 2 == 1))
    def _():
        step(lb_s, la_s, yb_s, ya_s)


MOE_RUN = MOE_RB


def _moe_dims(ts):
    nbmax = ts * TOP_K // MOE_RB + N_EXP
    nsteps = nbmax + 1
    nba = nbmax + 3
    return nbmax, nsteps, nba


def _moe_call(bexp, nblk, idx, wp, h2p, prm, ts):
    R = h2p.shape[0]
    nst = R // ts
    nbmax, nsteps, nba = _moe_dims(ts)
    one = pl.Buffered(1)
    wspec = lambda shape: pl.BlockSpec(shape, lambda i, s, bexp, nblk: (bexp[i * nsteps + s], 0, 0))
    gs = pltpu.PrefetchScalarGridSpec(
        num_scalar_prefetch=2, grid=(nst, nsteps),
        in_specs=[pl.BlockSpec((1, 1, nba * MOE_RB), lambda i, s, *_: (i, 0, 0), memory_space=pltpu.SMEM),
                  pl.BlockSpec((1, 1, nba * MOE_RB), lambda i, s, *_: (i, 0, 0), memory_space=pltpu.SMEM),
                  pl.BlockSpec((ts, D // 2), lambda i, s, *_: (i, 0), pipeline_mode=one),
                  wspec((1, D, 2 * D_FF)), wspec((1, D_FF, D)), wspec((1, 1, 2 * D_FF)), wspec((1, 1, D))],
        out_specs=pl.BlockSpec((1, ts + MOE_UNROLL, D), lambda i, s, *_: (i, 0, 0), pipeline_mode=one),
        scratch_shapes=[pltpu.VMEM((MOE_RB, D // 2), jnp.uint32), pltpu.VMEM((MOE_RB, D // 2), jnp.uint32),
                        pltpu.VMEM((MOE_RB, D), F32), pltpu.VMEM((MOE_RB, D), F32)],
    )
    return pl.pallas_call(
        _moe_kernel, grid_spec=gs,
        out_shape=jax.ShapeDtypeStruct((nst, ts + MOE_UNROLL, D), F32),
        compiler_params=_cparams(("arbitrary", "arbitrary")),
    )(bexp, nblk, idx, wp, h2p, prm['moe_wgu'], prm['moe_wd'], prm['moe_bgu'], prm['moe_bd'])


def _moe_plan(topi, topw, ts):
    R = topi.shape[0]
    nst = R // ts
    na = ts * TOP_K
    nbmax, nsteps, nba = _moe_dims(ts)
    e_flat = topi[:, :TOP_K].reshape(nst, na)
    w_flat = topw[:, :TOP_K].reshape(nst, na)
    ex = jnp.arange(N_EXP, dtype=jnp.int32)
    counts = jnp.sum(e_flat[:, :, None] == ex[None, None, :], axis=1).astype(jnp.int32)
    padded = (counts + MOE_RUN - 1) // MOE_RUN * MOE_RUN
    pad_end = jnp.cumsum(padded, axis=1)
    total = pad_end[:, -1:]
    nblk = (total[:, 0] // MOE_RB).astype(jnp.int32)
    blk0 = jnp.minimum(jnp.arange(nsteps, dtype=jnp.int32)[None, :] * MOE_RUN, total - MOE_RUN)
    bexp = jnp.minimum(jnp.sum(blk0[:, :, None] >= pad_end[:, None, :], axis=2), N_EXP - 1).astype(jnp.int32)
    m = na + MOE_RUN
    unused = N_EXP * m
    key_real = e_flat * m + jnp.arange(na, dtype=jnp.int32)[None, :]
    d = jnp.arange(MOE_RUN - 1, dtype=jnp.int32)[None, None, :]
    key_pad = jnp.where(d < (padded - counts)[:, :, None], ex[None, :, None] * m + na + d, unused)
    keys = jnp.concatenate([key_real, key_pad.reshape(nst, -1)], axis=1)
    vals = jnp.concatenate([w_flat, jnp.zeros((nst, N_EXP * (MOE_RUN - 1)), F32)], axis=1)
    keys, vals = lax.sort((keys, vals), dimension=1, num_keys=1)
    n_tail = nba * MOE_RB - MOE_RB - keys.shape[1]
    assert n_tail >= 0
    keys = jnp.concatenate([jnp.full((nst, MOE_RB), unused, jnp.int32), keys,
                            jnp.full((nst, n_tail), unused, jnp.int32)], axis=1)
    wp = jnp.concatenate([jnp.zeros((nst, MOE_RB), F32), vals, jnp.zeros((nst, n_tail), F32)], axis=1)
    j = keys % m
    real = (keys < unused) & (j < na)
    tok = j // TOP_K
    spare = ts + (jnp.arange(nba * MOE_RB, dtype=jnp.int32)[None, :] & (MOE_UNROLL - 1))
    idx = jnp.where(real, tok | (tok << 16), spare << 16).astype(jnp.int32)
    wp = jnp.where(real, wp, 0.0)
    return bexp.reshape(-1), nblk, idx.reshape(nst, 1, -1), wp.reshape(nst, 1, -1)


def _wperm_kernel(w_ref, p_ref, o_ref):
    o_ref[0] = jnp.dot(w_ref[0].astype(BF16), p_ref[...], preferred_element_type=F32).astype(BF16)


def _wperm_call(wgu):
    rows = lax.broadcasted_iota(jnp.int32, (2 * D_FF, 2 * D_FF), 0)
    cols = lax.broadcasted_iota(jnp.int32, (2 * D_FF, 2 * D_FF), 1)
    perm = (rows == 2 * (cols % D_FF) + cols // D_FF).astype(BF16)
    return pl.pallas_call(
        _wperm_kernel,
        grid=(N_EXP, 2),
        in_specs=[pl.BlockSpec((1, D, 2 * D_FF), lambda e, h: (e, 0, 0)),
                  pl.BlockSpec((2 * D_FF, D_FF), lambda e, h: (0, h))],
        out_specs=pl.BlockSpec((1, D, D_FF), lambda e, h: (e, 0, h)),
        out_shape=jax.ShapeDtypeStruct((N_EXP, D, 2 * D_FF), BF16),
        compiler_params=_cparams(("arbitrary", "arbitrary")),
    )(wgu, perm)


def _ln2_kernel(gid_ref, x1_ref, y_ref, g2_ref, lg_ref, lb_ref, o_ref):
    v = ALPHA_DN * x1_ref[...] + _per_slab(y_ref[0], g2_ref[0])
    o_ref[...] = _layer_norm(v, lg_ref[...], lb_ref[...])


def _ln2_call(gid, x1, y, mod3, prm):
    R = x1.shape[0]
    per = (y.shape[1] - MOE_UNROLL) // ROWS
    rspec = pl.BlockSpec((ROWS, D), lambda i, gid: (i, 0))
    yspec = pl.BlockSpec((1, ROWS, D), lambda i, gid: (i // per, i % per, 0))
    cspec = pl.BlockSpec((1, D), lambda i, gid: (0, 0))
    gs = pltpu.PrefetchScalarGridSpec(
        num_scalar_prefetch=1, grid=(R // ROWS,),
        in_specs=[rspec, yspec, pl.BlockSpec((1, NB, D), lambda i, gid: (gid[i], 0, 5)), cspec, cspec],
        out_specs=rspec,
    )
    return pl.pallas_call(
        _ln2_kernel, grid_spec=gs, out_shape=jax.ShapeDtypeStruct((R, D), F32),
        compiler_params=_cparams(("arbitrary",)),
    )(gid, x1, y, mod3, prm['ln2_g'], prm['ln2_b'])


def _block_diag(blocks):
    n, r, c = blocks.shape
    eye = jnp.eye(n, dtype=blocks.dtype)
    return jnp.einsum('nrc,nm->nrmc', blocks, eye).reshape(n * r, n * c)


def _prep_layer(p, l):
    prm = {}
    w = p['w_in'][l]
    pieces = [w[:, 0:512], w[:, 512:1024], w[:, 1280:1792], w[:, 1808:2320], w[:, 2320:2832], w[:, 3344:3856],
              w[:, 3856:4368], w[:, 2832:3088], w[:, 3088:3344], w[:, 1024:1152], w[:, 1152:1280],
              w[:, 1792:1808], w[:, 4368:4384], jnp.zeros((D, 96), w.dtype)]
    prm['w_in'] = jnp.concatenate(pieces, axis=1).astype(BF16)
    b_re, b_im = p['s5_b_re'][l], p['s5_b_im'][l]
    bt = []
    for j in range(4):
        sl = slice(8 * j, 8 * j + 8)
        bre = _block_diag(jnp.swapaxes(b_re[sl], 1, 2))
        bim = _block_diag(jnp.swapaxes(b_im[sl], 1, 2))
        bt.append(jnp.concatenate([bre, bim], axis=1))
    prm['s5_bt'] = jnp.stack(bt).astype(BF16)
    lam = lax.complex(p['s5_lam_re'][l], p['s5_lam_im'][l])
    dt = jnp.exp(p['s5_log_dt'][l])[:, :, None]
    lam_bar = jnp.exp(lam * dt)
    f = (lam_bar - 1.0) / lam
    prm['s5_lr'] = jnp.real(lam_bar).reshape(2, 4, 1, WG)
    prm['s5_li'] = jnp.imag(lam_bar).reshape(2, 4, 1, WG)
    c_c = lax.complex(p['s5_c_re'][l], p['s5_c_im'][l])
    cms = []
    for d in range(2):
        e = c_c * f[d][:, None, :]
        er = jnp.swapaxes(jnp.real(e), 1, 2)
        ei = jnp.swapaxes(jnp.imag(e), 1, 2)
        cm = []
        for j in range(4):
            sl = slice(8 * j, 8 * j + 8)
            cm.append(jnp.concatenate([_block_diag(er[sl]), -_block_diag(ei[sl])], axis=0))
        cms.append(jnp.stack(cm))
    prm['s5_cm'] = jnp.stack(cms).astype(BF16)
    prm['s5_d'] = p['s5_d'][l].reshape(1, WG)
    prm['s5_glu_w'] = p['s5_glu_w'][l].astype(BF16)
    prm['s5_glu_b'] = p['s5_glu_b'][l].reshape(1, WG)
    prm['ssd_conv_w'] = p['ssd_conv_w'][l]
    prm['ssd_conv_b'] = p['ssd_conv_b'][l].reshape(1, -1)
    pad = lambda v: jnp.concatenate([v.reshape(-1), jnp.zeros((128 - v.size,), F32)]).reshape(1, 128)
    prm['ssd_dtb'] = pad(p['ssd_dt_bias'][l])
    prm['ssd_arow'] = pad(-jnp.exp(p['ssd_a_log'][l]))
    prm['ssd_drow'] = jnp.repeat(p['ssd_d'][l], SSD_P).reshape(1, WG)
    prm['ssd_norm_g'] = p['ssd_norm_g'][l].reshape(1, WG)
    prm['lru_conv_w'] = p['lru_conv_w'][l]
    prm['lru_conv_b'] = p['lru_conv_b'][l].reshape(1, WG)
    prm['lru_wa'] = jnp.stack([_block_diag(p['lru_wa'][l][d]) for d in range(2)]).astype(BF16)
    prm['lru_wx'] = jnp.stack([_block_diag(p['lru_wx'][l][d]) for d in range(2)]).astype(BF16)
    prm['lru_ba'] = p['lru_ba'][l].reshape(2, 1, WG)
    prm['lru_bx'] = p['lru_bx'][l].reshape(2, 1, WG)
    prm['lru_sp'] = jax.nn.softplus(-p['lru_lam'][l]).reshape(2, 1, WG)
    prm['ml_gb'] = jnp.concatenate([jnp.zeros((SM_GATE,), F32), p['ml_gate_b'][l].reshape(-1),
                                    jnp.zeros((128 - SM_GATE - 4 * ML_H,), F32)]).reshape(1, 128)
    prm['ml_norm_g'] = p['ml_norm_g'][l].reshape(1, WG)
    prm['w_out'] = p['w_out'][l].astype(BF16)
    for k in ('ln1_g', 'ln1_b', 'ln2_g', 'ln2_b'):
        prm[k] = p[k][l].reshape(1, D)
    prm['router_w'] = p['router_w'][l]
    prm['router_b'] = p['router_b'][l].reshape(1, N_EXP)
    prm['moe_wgu'] = _wperm_call(p['moe_w_gate_up'][l])
    bgu = p['moe_b_gate_up'][l]
    prm['moe_bgu'] = jnp.concatenate([bgu[:, 0::2], bgu[:, 1::2]], axis=1).reshape(N_EXP, 1, 2 * D_FF)
    prm['moe_wd'] = p['moe_w_down'][l].astype(BF16)
    prm['moe_bd'] = p['moe_b_down'][l].reshape(N_EXP, 1, D)
    prm['ada_w'] = p['ada_w'][l].astype(BF16)
    prm['ada_b'] = p['ada_b'][l].reshape(1, 6 * D)
    return prm


def _mixers(flags, proj, prm):
    ya = _s5_call(flags, proj, None, prm, 0, reverse=False, final=False)
    ya = _s5_call(flags, proj, ya, prm, 1, reverse=True, final=True)
    yb = _ssd_call(flags, proj, None, prm, reverse=False, final=False)
    yb = _ssd_call(flags, proj, yb, prm, reverse=True, final=True)
    yc = _lru_call(flags, proj, None, prm, 0, reverse=False, final=False)
    yc = _lru_call(flags, proj, yc, prm, 1, reverse=True, final=True)
    yd = _mlstm_call(flags, proj, None, prm, reverse=False, final=False)
    yd = _mlstm_call(flags, proj, yd, prm, reverse=True, final=True)
    return ya, yb, yc, yd


def _to_rows(x):
    b, L, _ = x.shape
    return x.reshape(b // NB, NB, L, D).transpose(0, 2, 1, 3).reshape(b * L, D)


def _from_rows(r, b, L):
    return r.reshape(b // NB, L, NB, D).transpose(0, 2, 1, 3).reshape(b, L, D)


def _trunk(xs, cs, p, depth, moe_ts):
    first, last, gid = [], [], []
    g = 0
    for x in xs:
        b, L, _ = x.shape
        assert b % NB == 0 and L % CH == 0
        n = L // CH
        for _ in range(b // NB):
            first += [1] + [0] * (n - 1)
            last += [0] * (n - 1) + [1]
            gid += [g] * n
            g += 1
    first = jnp.asarray(first, jnp.int32)
    last = jnp.asarray(last, jnp.int32)
    gid = jnp.asarray(gid, jnp.int32)
    flags = (first, last)
    x = jnp.concatenate([_to_rows(x.astype(F32)) for x in xs], axis=0)
    c_all = jnp.concatenate([c.astype(F32) for c in cs], axis=0)
    R = x.shape[0]
    ts = min(moe_ts, R)
    assert R % ts == 0
    prev = None
    for l in range(depth):
        prm = _prep_layer(p, l)
        mod3 = _mod_call(c_all, prm['ada_w'], prm['ada_b']).reshape(g, NB, 6 * D)
        if prev is None:
            proj = _inproj_call(gid, x, mod3, prm['w_in'])
        else:
            proj, x = _inproj_ln_call(gid, *prev, mod3, prm['w_in'])
        ys = _mixers(flags, proj, prm)
        x1, h2, topi, topw = _outproj_call(gid, ys, x, mod3, prm)
        bexp, nblk, idx, wp = _moe_plan(topi, topw, ts)
        y = _moe_call(bexp, nblk, idx, wp, h2, prm, ts)
        prev = (x1, y, mod3, prm)
    x = _ln2_call(gid, *prev)
    outs = []
    r0 = 0
    for xin in xs:
        b, L, _ = xin.shape
        outs.append(_from_rows(x[r0:r0 + b * L], b, L).astype(xin.dtype))
        r0 += b * L
    return outs


def kernel(x_prompt, x_sample, c_prompt, c_sample, ada_w, ada_b, w_in, s5_lam_re, s5_lam_im, s5_log_dt, s5_b_re, s5_b_im, s5_c_re, s5_c_im, s5_d, s5_glu_w, s5_glu_b, ssd_conv_w, ssd_conv_b, ssd_a_log, ssd_dt_bias, ssd_d, ssd_norm_g, lru_conv_w, lru_conv_b, lru_wa, lru_ba, lru_wx, lru_bx, lru_lam, ml_gate_b, ml_norm_g, w_out, ln1_g, ln1_b, router_w, router_b, moe_w_gate_up, moe_b_gate_up, moe_w_down, moe_b_down, ln2_g, ln2_b):
    p = {
        'ada_w': ada_w, 'ada_b': ada_b, 'w_in': w_in,
        's5_lam_re': s5_lam_re, 's5_lam_im': s5_lam_im, 's5_log_dt': s5_log_dt,
        's5_b_re': s5_b_re, 's5_b_im': s5_b_im, 's5_c_re': s5_c_re, 's5_c_im': s5_c_im,
        's5_d': s5_d, 's5_glu_w': s5_glu_w, 's5_glu_b': s5_glu_b,
        'ssd_conv_w': ssd_conv_w, 'ssd_conv_b': ssd_conv_b, 'ssd_a_log': ssd_a_log,
        'ssd_dt_bias': ssd_dt_bias, 'ssd_d': ssd_d, 'ssd_norm_g': ssd_norm_g,
        'lru_conv_w': lru_conv_w, 'lru_conv_b': lru_conv_b, 'lru_wa': lru_wa, 'lru_ba': lru_ba,
        'lru_wx': lru_wx, 'lru_bx': lru_bx, 'lru_lam': lru_lam,
        'ml_gate_b': ml_gate_b, 'ml_norm_g': ml_norm_g, 'w_out': w_out,
        'ln1_g': ln1_g, 'ln1_b': ln1_b, 'router_w': router_w, 'router_b': router_b,
        'moe_w_gate_up': moe_w_gate_up, 'moe_b_gate_up': moe_b_gate_up,
        'moe_w_down': moe_w_down, 'moe_b_down': moe_b_down, 'ln2_g': ln2_g, 'ln2_b': ln2_b,
    }
    y_prompt, y_sample = _trunk([x_prompt, x_sample], [c_prompt, c_sample], p, DEPTH, MOE_TS)
    return (y_prompt, y_sample)
```

```python
import functools
import math

import numpy as np
import jax
import jax.numpy as jnp
from jax import lax
from jax.experimental import pallas as pl
from jax.experimental.pallas import tpu as pltpu

F32 = jnp.float32
BF16 = jnp.bfloat16
HIGHEST = lax.Precision.HIGHEST

D = 1024
DEPTH = 4
WG = 512
S5_G, S5_CH, S5_N = 32, 16, 64
SSD_H, SSD_P, SSD_NG, SSD_N = 8, 64, 2, 64
LRU_NB, LRU_BD, LRU_C = 8, 64, 8.0
ML_H, ML_DQK, ML_DV = 4, 64, 128
N_EXP, TOP_K, D_FF = 32, 4, 1024
SWIGLU_LIMIT, SWIGLU_ALPHA = 7.0, 1.702
ALPHA_DN = (2.0 * DEPTH) ** 0.25
LN_EPS, RMS_EPS = 1e-5, 1e-6

LANE = 128
NB = 8
CH = 128
ROWS = CH * NB
PW = 4480
K1_TN = 640
K3_TM = 512
MOE_RB = 256
MOE_FC = 256
MOE_UNROLL = 8
MOE_TS = 4096
VMEM_LIMIT = 56 * 1024 * 1024

COL_U, COL_XS, COL_Z, COL_XL, COL_GL, COL_V, COL_O = 0, 1, 2, 3, 4, 5, 6
COL_Q, COL_K, COL_BC = 14, 15, 16
COL_SM = 34
SM_DT, SM_GATE = 0, 16


def _sigmoid(x):
    return 1.0 / (1.0 + jnp.exp(-x))


def _silu(x):
    return x * _sigmoid(x)


def _softplus(x):
    return jnp.maximum(x, 0.0) + jnp.log(1.0 + jnp.exp(-jnp.abs(x)))


def _gelu_tanh(x):
    return 0.5 * x * (1.0 + jnp.tanh(math.sqrt(2.0 / math.pi) * (x + 0.044715 * (x * x * x))))


def _layer_norm(v, g, b):
    mu = jnp.mean(v, axis=-1, keepdims=True)
    vc = v - mu
    var = jnp.mean(vc * vc, axis=-1, keepdims=True)
    return vc * lax.rsqrt(var + LN_EPS) * g + b


def _bdot(a, b):
    return jnp.dot(a.astype(BF16), b.astype(BF16), preferred_element_type=F32)


def _per_slab(x, m):
    n = x.shape[0] // NB
    return (x.reshape(n, NB, x.shape[1]) * m[None]).reshape(x.shape)


def _per_slab_add(x, m):
    n = x.shape[0] // NB
    return (x.reshape(n, NB, x.shape[1]) + m[None]).reshape(x.shape)


def _cparams(sem):
    return pltpu.CompilerParams(dimension_semantics=sem, vmem_limit_bytes=VMEM_LIMIT)


def _mod_kernel(c_ref, w_ref, b_ref, o_ref):
    o_ref[...] = _bdot(_silu(c_ref[...]), w_ref[...]) + b_ref[...]


def _mod_call(c_all, w, b):
    n = c_all.shape[0]
    return pl.pallas_call(
        _mod_kernel,
        grid=(6,),
        in_specs=[pl.BlockSpec((n, D), lambda j: (0, 0)),
                  pl.BlockSpec((D, D), lambda j: (0, j)),
                  pl.BlockSpec((1, D), lambda j: (0, j))],
        out_specs=pl.BlockSpec((n, D), lambda j: (0, j)),
        out_shape=jax.ShapeDtypeStruct((n, 6 * D), F32),
        compiler_params=_cparams(("arbitrary",)),
    )(c_all, w, b)


def _inproj_kernel(gid_ref, x_ref, sc_ref, sh_ref, w_ref, o_ref, h_s):
    @pl.when(pl.program_id(1) == 0)
    def _():
        h = _per_slab_add(_per_slab(x_ref[...], 1.0 + sc_ref[0]), sh_ref[0])
        h_s[...] = h.astype(BF16)

    res = jnp.dot(h_s[...], w_ref[...], preferred_element_type=F32)
    for k in range(K1_TN // LANE):
        o_ref[k] = res[:, LANE * k:LANE * (k + 1)]


def _inproj_call(gid, x, mod3, w_in_p):
    R = x.shape[0]
    nch = R // ROWS
    gs = pltpu.PrefetchScalarGridSpec(
        num_scalar_prefetch=1,
        grid=(nch, PW // K1_TN),
        in_specs=[pl.BlockSpec((ROWS, D), lambda i, j, gid: (i, 0)),
                  pl.BlockSpec((1, NB, D), lambda i, j, gid: (gid[i], 0, 1)),
                  pl.BlockSpec((1, NB, D), lambda i, j, gid: (gid[i], 0, 0)),
                  pl.BlockSpec((D, K1_TN), lambda i, j, gid: (0, j))],
        out_specs=pl.BlockSpec((K1_TN // LANE, ROWS, LANE), lambda i, j, gid: (j, i, 0)),
        scratch_shapes=[pltpu.VMEM((ROWS, D), BF16)],
    )
    return pl.pallas_call(
        _inproj_kernel, grid_spec=gs,
        out_shape=jax.ShapeDtypeStruct((PW // LANE, R, LANE), F32),
        compiler_params=_cparams(("arbitrary", "arbitrary")),
    )(gid, x, mod3, mod3, w_in_p)


def _chunk_idx(nch, reverse):
    return (lambda c: nch - 1 - c) if reverse else (lambda c: c)


def _main_spec(width, col, nch, reverse):
    ci = _chunk_idx(nch, reverse)
    return pl.BlockSpec((width // LANE, ROWS, LANE), lambda c, *_: (col, ci(c), 0))


def _prev_spec(width, col, nch, reverse):
    ci = _chunk_idx(nch, reverse)
    per = ROWS // (2 * NB)
    return pl.BlockSpec((width // LANE, 2 * NB, LANE), lambda c, *_: (col, jnp.maximum(ci(c) * per - 1, 0), 0))


def _next_spec(width, col, nch, reverse):
    ci = _chunk_idx(nch, reverse)
    per = ROWS // NB
    return pl.BlockSpec((width // LANE, NB, LANE),
                        lambda c, *_: (col, jnp.minimum((ci(c) + 1) * per, nch * per - 1), 0))


def _const_spec(shape):
    nd = len(shape)
    return pl.BlockSpec(shape, lambda c, *_: (0,) * nd)


def _cat(ref):
    return jnp.concatenate([ref[k] for k in range(ref.shape[0])], axis=1)


def _put(ref, val):
    for k in range(ref.shape[0]):
        ref[k] = val[:, LANE * k:LANE * (k + 1)]


def _fill_ext(ext_s, k0, x_ref, xp_ref, xn_ref, is_first, is_last):
    for k in range(x_ref.shape[0]):
        ext_s[k0 + k, 0:2 * NB, :] = jnp.where(is_first, 0.0, xp_ref[k])
        ext_s[k0 + k, 2 * NB:2 * NB + ROWS, :] = x_ref[k]
        ext_s[k0 + k, 2 * NB + ROWS:3 * NB + ROWS, :] = jnp.where(is_last, 0.0, xn_ref[k])


def _dwconv(ext_s, k, w_ref, b_ref):
    sl = slice(LANE * k, LANE * (k + 1))
    out = b_ref[:, sl] + w_ref[0:1, sl] * ext_s[k, 0:ROWS, :]
    for j in range(1, 4):
        out = out + w_ref[j:j + 1, sl] * ext_s[k, j * NB:j * NB + ROWS, :]
    return out


def _tri(reverse):
    r = lax.broadcasted_iota(jnp.int32, (CH, CH), 0)
    c = lax.broadcasted_iota(jnp.int32, (CH, CH), 1)
    return (c >= r) if reverse else (c <= r)


def _s5_kernel(*refs, nch, reverse, final):
    first_ref, last_ref = refs[0], refs[1]
    if final:
        (u_ref, yp_ref, bt_ref, lr_ref, li_ref, cm_ref, d_ref, gw_ref, gb_ref,
         o_ref, bu_s, st_s, carry_s) = refs[2:]
    else:
        u_ref, bt_ref, lr_ref, li_ref, cm_ref, o_ref, bu_s, st_s, carry_s = refs[2:]
    c = pl.program_id(0)
    cc = nch - 1 - c if reverse else c
    start = last_ref[cc] if reverse else first_ref[cc]

    @pl.when(start == 1)
    def _():
        carry_s[...] = jnp.zeros_like(carry_s)

    half = WG
    for j in range(4):
        bu_s[...] = jnp.dot(u_ref[j].astype(BF16), bt_ref[j], preferred_element_type=F32)
        lr = jnp.broadcast_to(lr_ref[j], (NB, half))
        li = jnp.broadcast_to(li_ref[j], (NB, half))

        def step(i, carry, lr=lr, li=li):
            sr, si = carry
            t = CH - 1 - i if reverse else i
            r0 = pl.multiple_of(t * NB, NB)
            br = bu_s[pl.ds(r0, NB), 0:half]
            bi = bu_s[pl.ds(r0, NB), half:2 * half]
            nr = lr * sr - li * si + br
            ni = lr * si + li * sr + bi
            st_s[pl.ds(r0, NB), 0:half] = nr
            st_s[pl.ds(r0, NB), half:2 * half] = ni
            return nr, ni

        lax.fori_loop(0, CH, step, (carry_s[j, :, 0:half], carry_s[j, :, half:2 * half]), unroll=8)
        e0 = (CH - 1) * NB if reverse else 0
        carry_s[j] = st_s[e0:e0 + NB, :]
        o_ref[j] = jnp.dot(st_s[...].astype(BF16), cm_ref[j], preferred_element_type=F32)

    if final:
        y = _cat(o_ref) + _cat(yp_ref) + d_ref[...] * _cat(u_ref)
        g = _gelu_tanh(y)
        _put(o_ref, g * _sigmoid(_bdot(g, gw_ref[...]) + gb_ref[...]))


def _s5_call(flags, proj, yprev, prm, d, *, reverse, final):
    first, last = flags
    R = proj.shape[1]
    nch = R // ROWS
    in_specs = [_main_spec(WG, COL_U, nch, reverse)]
    args = [proj]
    if final:
        in_specs.append(_main_spec(WG, 0, nch, reverse))
        args.append(yprev)
    in_specs += [_const_spec((4, 128, 2 * WG)), _const_spec((4, 1, WG)), _const_spec((4, 1, WG)),
                 _const_spec((4, 2 * WG, 128))]
    args += [prm['s5_bt'], prm['s5_lr'][d], prm['s5_li'][d], prm['s5_cm'][d]]
    if final:
        in_specs += [_const_spec((1, WG)), _const_spec((WG, WG)), _const_spec((1, WG))]
        args += [prm['s5_d'], prm['s5_glu_w'], prm['s5_glu_b']]
    gs = pltpu.PrefetchScalarGridSpec(
        num_scalar_prefetch=2, grid=(nch,), in_specs=in_specs,
        out_specs=_main_spec(WG, 0, nch, reverse),
        scratch_shapes=[pltpu.VMEM((ROWS, 2 * WG), F32), pltpu.VMEM((ROWS, 2 * WG), F32),
                        pltpu.VMEM((4, NB, 2 * WG), F32)],
    )
    return pl.pallas_call(
        functools.partial(_s5_kernel, nch=nch, reverse=reverse, final=final), grid_spec=gs,
        out_shape=jax.ShapeDtypeStruct((WG // LANE, R, LANE), F32),
        compiler_params=_cparams(("arbitrary",)),
    )(first, last, *args)


def _lru_kernel(*refs, nch, reverse, final):
    first_ref, last_ref = refs[0], refs[1]
    if final:
        (x_ref, xp_ref, xn_ref, gate_ref, hp_ref, cw_ref, cb_ref, wa_ref, ba_ref, wx_ref, bx_ref, sp_ref,
         o_ref, ext_s, a_s, inp_s, carry_s) = refs[2:]
    else:
        (x_ref, xp_ref, xn_ref, cw_ref, cb_ref, wa_ref, ba_ref, wx_ref, bx_ref, sp_ref,
         o_ref, ext_s, a_s, inp_s, carry_s) = refs[2:]
    c = pl.program_id(0)
    cc = nch - 1 - c if reverse else c
    is_first = first_ref[cc] == 1
    is_last = last_ref[cc] == 1
    start = is_last if reverse else is_first

    @pl.when(start)
    def _():
        carry_s[...] = jnp.zeros_like(carry_s)

    _fill_ext(ext_s, 0, x_ref, xp_ref, xn_ref, is_first, is_last)
    xc = jnp.concatenate([_dwconv(ext_s, k, cw_ref, cb_ref) for k in range(WG // LANE)], axis=1)
    xb = xc.astype(BF16)
    r = _sigmoid(jnp.dot(xb, wa_ref[...], preferred_element_type=F32) + ba_ref[...])
    i = _sigmoid(jnp.dot(xb, wx_ref[...], preferred_element_type=F32) + bx_ref[...])
    log_a = -LRU_C * r * sp_ref[...]
    a_s[...] = jnp.exp(log_a)
    inp_s[...] = jnp.sqrt(1.0 - jnp.exp(2.0 * log_a)) * (i * xc)

    def step(k, h):
        t = CH - 1 - k if reverse else k
        r0 = pl.multiple_of(t * NB, NB)
        h = a_s[pl.ds(r0, NB), :] * h + inp_s[pl.ds(r0, NB), :]
        for k in range(WG // LANE):
            o_ref[k, pl.ds(r0, NB), :] = h[:, LANE * k:LANE * (k + 1)]
        return h

    carry_s[...] = lax.fori_loop(0, CH, step, carry_s[...], unroll=8)
    if final:
        for k in range(WG // LANE):
            o_ref[k] = (o_ref[k] + hp_ref[k]) * _gelu_tanh(gate_ref[k])


def _lru_call(flags, proj, hprev, prm, d, *, reverse, final):
    first, last = flags
    R = proj.shape[1]
    nch = R // ROWS
    in_specs = [_main_spec(WG, COL_XL, nch, reverse), _prev_spec(WG, COL_XL, nch, reverse),
                _next_spec(WG, COL_XL, nch, reverse)]
    args = [proj, proj, proj]
    if final:
        in_specs += [_main_spec(WG, COL_GL, nch, reverse), _main_spec(WG, 0, nch, reverse)]
        args += [proj, hprev]
    in_specs += [_const_spec((4, WG)), _const_spec((1, WG)), _const_spec((WG, WG)), _const_spec((1, WG)),
                 _const_spec((WG, WG)), _const_spec((1, WG)), _const_spec((1, WG))]
    args += [prm['lru_conv_w'], prm['lru_conv_b'], prm['lru_wa'][d], prm['lru_ba'][d], prm['lru_wx'][d],
             prm['lru_bx'][d], prm['lru_sp'][d]]
    gs = pltpu.PrefetchScalarGridSpec(
        num_scalar_prefetch=2, grid=(nch,), in_specs=in_specs,
        out_specs=_main_spec(WG, 0, nch, reverse),
        scratch_shapes=[pltpu.VMEM((WG // LANE, ROWS + 3 * NB, LANE), F32), pltpu.VMEM((ROWS, WG), F32),
                        pltpu.VMEM((ROWS, WG), F32), pltpu.VMEM((NB, WG), F32)],
    )
    return pl.pallas_call(
        functools.partial(_lru_kernel, nch=nch, reverse=reverse, final=final), grid_spec=gs,
        out_shape=jax.ShapeDtypeStruct((WG // LANE, R, LANE), F32),
        compiler_params=_cparams(("arbitrary",)),
    )(first, last, *args)


def _ssd_kernel(*refs, nch, reverse, final):
    first_ref, last_ref = refs[0], refs[1]
    if final:
        (x_ref, xp_ref, xn_ref, bc_ref, bcp_ref, bcn_ref, sm_ref, z_ref, yp_ref,
         cw_ref, cb_ref, dtb_ref, arow_ref, drow_ref, ng_ref,
         o_ref, ext_s, xd_s, smd_s, yd_s, st_s) = refs[2:]
    else:
        (x_ref, xp_ref, xn_ref, bc_ref, bcp_ref, bcn_ref, sm_ref,
         cw_ref, cb_ref, dtb_ref, arow_ref,
         o_ref, ext_s, xd_s, smd_s, yd_s, st_s) = refs[2:]
    c = pl.program_id(0)
    cc = nch - 1 - c if reverse else c
    is_first = first_ref[cc] == 1
    is_last = last_ref[cc] == 1
    start = is_last if reverse else is_first

    @pl.when(start)
    def _():
        st_s[...] = jnp.zeros_like(st_s)

    CW = WG + 2 * SSD_NG * SSD_N
    _fill_ext(ext_s, 0, x_ref, xp_ref, xn_ref, is_first, is_last)
    _fill_ext(ext_s, WG // LANE, bc_ref, bcp_ref, bcn_ref, is_first, is_last)
    for k in range(CW // LANE):
        ext_s[k, 0:ROWS, :] = _silu(_dwconv(ext_s, k, cw_ref, cb_ref))
    for b in range(NB):
        for k in range(CW // LANE):
            xd_s[b, :, LANE * k:LANE * (k + 1)] = ext_s[k, pl.ds(b, CH, stride=NB), :]
        smd_s[b] = sm_ref[0, pl.ds(b, CH, stride=NB), :]

    mask = _tri(reverse)
    tri = jnp.where(mask, 1.0, 0.0)
    doff = SM_DT + (SSD_H if reverse else 0)
    edge = 0 if reverse else CH - 1
    NS = SSD_N
    lane_c = lax.broadcasted_iota(jnp.int32, (CH, CH), 1)

    def per_b(b, _):
        xb = xd_s[b]
        xs = xb[:, 0:WG]
        bm = xb[:, WG:WG + SSD_NG * NS]
        cm = xb[:, WG + SSD_NG * NS:CW]
        dtf = _softplus(smd_s[b] + dtb_ref[...])
        af = dtf * arow_ref[...]
        cs = jnp.dot(tri, af, precision=HIGHEST, preferred_element_type=F32)
        cs_t = cs.T
        dt_t = dtf.T
        bm_t = bm.T
        cmask = [jnp.where((lane_c // NS) == g, cm, 0.0).astype(BF16) for g in range(SSD_NG)]
        bm_tb = bm_t.astype(BF16)
        gmat = [jnp.dot(cmask[g], bm_tb, preferred_element_type=F32) for g in range(SSD_NG)]
        for hp in range(SSD_H // 2):
            g = (2 * hp) // (SSD_H // SSD_NG)
            xpair = xs[:, LANE * hp:LANE * (hp + 1)]
            s_prev = st_s[b, hp]
            y = None
            upd = None
            ecol = None
            etot = None
            for k in range(2):
                ci = doff + 2 * hp + k
                mine = (lane_c // SSD_P) == k
                col_b = jnp.broadcast_to(cs[:, ci:ci + 1], (CH, CH))
                row = cs_t[ci:ci + 1, :]
                lmat = jnp.where(mask, jnp.exp(jnp.minimum(col_b - row, 0.0)), 0.0)
                m = gmat[g] * lmat * dt_t[ci:ci + 1, :]
                xk = jnp.where(mine, xpair, 0.0).astype(BF16)
                yk = jnp.dot(m.astype(BF16), xk, preferred_element_type=F32)
                tot = cs[edge:edge + 1, ci:ci + 1]
                wrow = jnp.exp(tot - row) * dt_t[ci:ci + 1, :]
                uk = jnp.dot((bm_t[NS * g:NS * (g + 1), :] * wrow).astype(BF16), xk, preferred_element_type=F32)
                y = yk if k == 0 else y + yk
                upd = uk if k == 0 else upd + uk
                ecol = jnp.exp(col_b) if k == 0 else jnp.where(mine, jnp.exp(col_b), ecol)
                etot = jnp.exp(tot) if k == 0 else jnp.where(mine[0:1, :], jnp.exp(tot), etot)
            s2 = jnp.concatenate([s_prev, s_prev], axis=0).astype(BF16)
            y = y + ecol * jnp.dot(cmask[g], s2, preferred_element_type=F32)
            st_s[b, hp] = etot * s_prev + upd
            yd_s[b, :, LANE * hp:LANE * (hp + 1)] = y
        return 0

    lax.fori_loop(0, NB, per_b, 0)
    for b in range(NB):
        for k in range(WG // LANE):
            o_ref[k, pl.ds(b, CH, stride=NB), :] = yd_s[b, :, LANE * k:LANE * (k + 1)]
    if final:
        xs_all = jnp.concatenate([ext_s[k, 0:ROWS, :] for k in range(WG // LANE)], axis=1)
        y = (_cat(o_ref) + _cat(yp_ref) + drow_ref[...] * xs_all) * _silu(_cat(z_ref))
        ms = jnp.mean(y * y, axis=-1, keepdims=True)
        _put(o_ref, y * lax.rsqrt(ms + RMS_EPS) * ng_ref[...])


def _ssd_call(flags, proj, yprev, prm, *, reverse, final):
    first, last = flags
    R = proj.shape[1]
    nch = R // ROWS
    CW = WG + 2 * SSD_NG * SSD_N
    in_specs = [_main_spec(WG, COL_XS, nch, reverse), _prev_spec(WG, COL_XS, nch, reverse),
                _next_spec(WG, COL_XS, nch, reverse),
                _main_spec(256, COL_BC, nch, reverse), _prev_spec(256, COL_BC, nch, reverse),
                _next_spec(256, COL_BC, nch, reverse),
                _main_spec(128, COL_SM, nch, reverse)]
    args = [proj] * 7
    if final:
        in_specs += [_main_spec(WG, COL_Z, nch, reverse), _main_spec(WG, 0, nch, reverse)]
        args += [proj, yprev]
    in_specs += [_const_spec((4, CW)), _const_spec((1, CW)), _const_spec((1, 128)), _const_spec((1, 128))]
    args += [prm['ssd_conv_w'], prm['ssd_conv_b'], prm['ssd_dtb'], prm['ssd_arow']]
    if final:
        in_specs += [_const_spec((1, WG)), _const_spec((1, WG))]
        args += [prm['ssd_drow'], prm['ssd_norm_g']]
    gs = pltpu.PrefetchScalarGridSpec(
        num_scalar_prefetch=2, grid=(nch,), in_specs=in_specs,
        out_specs=_main_spec(WG, 0, nch, reverse),
        scratch_shapes=[pltpu.VMEM((CW // LANE, ROWS + 3 * NB, LANE), F32), pltpu.VMEM((NB, CH, CW), F32),
                        pltpu.VMEM((NB, CH, 128), F32), pltpu.VMEM((NB, CH, WG), F32),
                        pltpu.VMEM((NB, SSD_H // 2, SSD_N, 2 * SSD_P), F32)],
    )
    return pl.pallas_call(
        functools.partial(_ssd_kernel, nch=nch, reverse=reverse, final=final), grid_spec=gs,
        out_shape=jax.ShapeDtypeStruct((WG // LANE, R, LANE), F32),
        compiler_params=_cparams(("arbitrary",)),
    )(first, last, *args)


NEG = -1e30


def _mlstm_kernel(*refs, nch, reverse, final):
    first_ref, last_ref = refs[0], refs[1]
    if final:
        (q_ref, k_ref, v_ref, sm_ref, og_ref, hp_ref, gb_ref, ng_ref,
         o_ref, qd_s, kd_s, vd_s, smd_s, hd_s, c_s, n_s, m_s) = refs[2:]
    else:
        (q_ref, k_ref, v_ref, sm_ref, gb_ref,
         o_ref, qd_s, kd_s, vd_s, smd_s, hd_s, c_s, n_s, m_s) = refs[2:]
    c = pl.program_id(0)
    cc = nch - 1 - c if reverse else c
    start = (last_ref[cc] if reverse else first_ref[cc]) == 1

    @pl.when(start)
    def _():
        c_s[...] = jnp.zeros_like(c_s)
        n_s[...] = jnp.zeros_like(n_s)
        m_s[...] = jnp.zeros_like(m_s)

    for b in range(NB):
        for src, dst in ((q_ref, qd_s), (k_ref, kd_s), (v_ref, vd_s)):
            for k in range(src.shape[0]):
                dst[b, :, LANE * k:LANE * (k + 1)] = src[k, pl.ds(b, CH, stride=NB), :]
        smd_s[b] = sm_ref[0, pl.ds(b, CH, stride=NB), :]

    mask = _tri(reverse)
    tri = jnp.where(mask, 1.0, 0.0)
    goff = SM_GATE + (2 * ML_H if reverse else 0)
    edge = 0 if reverse else CH - 1
    ones_b = jnp.ones((CH, CH), BF16)
    lane_c = lax.broadcasted_iota(jnp.int32, (CH, CH), 1)

    def per_b(b, _):
        sm = smd_s[b] + gb_ref[...]
        logf = -_softplus(-sm)
        fc_all = jnp.dot(tri, logf, precision=HIGHEST, preferred_element_type=F32)
        fr_all = fc_all.T
        sm_t = sm.T
        qb = qd_s[b] * (ML_DQK ** -0.5)
        kb_t = kd_s[b].T
        vb = vd_s[b]
        for h in range(ML_H):
            ci = goff + h
            cf = goff + ML_H + h
            fc = fc_all[:, cf:cf + 1]
            ic = sm[:, ci:ci + 1]
            fr = fr_all[cf:cf + 1, :]
            ir = sm_t[ci:ci + 1, :]
            m_prev = m_s[b, h]
            u = jnp.where(mask, ir - fr, NEG)
            g = jnp.maximum(jnp.max(u, axis=1, keepdims=True), m_prev)
            g_b = jnp.broadcast_to(g, (CH, CH))
            wts = jnp.exp(u - g_b)
            w_inter = jnp.exp(m_prev - g_b)
            hp, k = h // 2, h % 2
            qh = jnp.where((lane_c // ML_DQK) == k, qb[:, LANE * hp:LANE * (hp + 1)], 0.0).astype(BF16)
            kh_t = kb_t[ML_DQK * h:ML_DQK * (h + 1), :]
            vh = vb[:, ML_DV * h:ML_DV * (h + 1)]
            c_pair = c_s[b, hp]
            n_pair = n_s[b, hp]
            ct_prev = c_pair[ML_DQK * k:ML_DQK * (k + 1), :]
            n_prev = n_pair[ML_DQK * k:ML_DQK * (k + 1), :]
            sb = (jnp.dot(qh, kb_t[LANE * hp:LANE * (hp + 1), :].astype(BF16), preferred_element_type=F32)
                  * wts).astype(BF16)
            num = _bdot(sb, vh) + w_inter * _bdot(qh, c_pair)
            den = jnp.dot(sb, ones_b, preferred_element_type=F32) + w_inter * _bdot(qh, n_pair)
            hd_s[b, :, ML_DV * h:ML_DV * (h + 1)] = num / jnp.maximum(jnp.abs(den), jnp.exp(-(fc + g)))
            f_end = fc_all[edge:edge + 1, cf:cf + 1]
            w_log = f_end - fc + ic
            m_new = jnp.maximum(f_end + m_prev, jnp.max(w_log, axis=0, keepdims=True))
            w_s = jnp.broadcast_to(jnp.exp(w_log - m_new), (CH, CH))
            scale = jnp.exp(f_end + m_prev - m_new)
            c_s[b, hp, ML_DQK * k:ML_DQK * (k + 1), :] = scale * ct_prev + _bdot(kh_t, vh * w_s)
            n_s[b, hp, ML_DQK * k:ML_DQK * (k + 1), :] = scale * n_prev + _bdot(kh_t, w_s)
            m_s[b, h] = m_new
        return 0

    lax.fori_loop(0, NB, per_b, 0)
    for b in range(NB):
        for k in range(WG // LANE):
            o_ref[k, pl.ds(b, CH, stride=NB), :] = hd_s[b, :, LANE * k:LANE * (k + 1)]
    if final:
        assert ML_DV == LANE
        for h in range(ML_H):
            hh = o_ref[h] + hp_ref[h]
            ms = jnp.mean(hh * hh, axis=-1, keepdims=True)
            o_ref[h] = hh * lax.rsqrt(ms + RMS_EPS) * ng_ref[:, LANE * h:LANE * (h + 1)] * _sigmoid(og_ref[h])


def _mlstm_call(flags, proj, hprev, prm, *, reverse, final):
    first, last = flags
    R = proj.shape[1]
    nch = R // ROWS
    in_specs = [_main_spec(256, COL_Q, nch, reverse), _main_spec(256, COL_K, nch, reverse),
                _main_spec(WG, COL_V, nch, reverse), _main_spec(128, COL_SM, nch, reverse)]
    args = [proj] * 4
    if final:
        in_specs += [_main_spec(WG, COL_O, nch, reverse), _main_spec(WG, 0, nch, reverse)]
        args += [proj, hprev]
    in_specs += [_const_spec((1, 128))]
    args += [prm['ml_gb']]
    if final:
        in_specs += [_const_spec((1, WG))]
        args += [prm['ml_norm_g']]
    gs = pltpu.PrefetchScalarGridSpec(
        num_scalar_prefetch=2, grid=(nch,), in_specs=in_specs,
        out_specs=_main_spec(WG, 0, nch, reverse),
        scratch_shapes=[pltpu.VMEM((NB, CH, 256), F32), pltpu.VMEM((NB, CH, 256), F32),
                        pltpu.VMEM((NB, CH, WG), F32), pltpu.VMEM((NB, CH, 128), F32),
                        pltpu.VMEM((NB, CH, WG), F32),
                        pltpu.VMEM((NB, ML_H // 2, 2 * ML_DQK, ML_DV), F32),
                        pltpu.VMEM((NB, ML_H // 2, 2 * ML_DQK, CH), F32),
                        pltpu.VMEM((NB, ML_H, 1, 1), F32)],
    )
    return pl.pallas_call(
        functools.partial(_mlstm_kernel, nch=nch, reverse=reverse, final=final), grid_spec=gs,
        out_shape=jax.ShapeDtypeStruct((WG // LANE, R, LANE), F32),
        compiler_params=_cparams(("arbitrary",)),
    )(first, last, *args)


def _outproj_kernel(gid_ref, ya_ref, yb_ref, yc_ref, yd_ref, x_ref, g1_ref, sc_ref, sh_ref, wo_ref,
                    lg_ref, lb_ref, rw_ref, rb_ref, x1_ref, h2_ref, ti_ref, tw_ref):
    o = _bdot(_cat(ya_ref), wo_ref[0:WG, :])
    o = o + _bdot(_cat(yb_ref), wo_ref[WG:2 * WG, :])
    o = o + _bdot(_cat(yc_ref), wo_ref[2 * WG:3 * WG, :])
    o = o + _bdot(_cat(yd_ref), wo_ref[3 * WG:4 * WG, :])
    v = ALPHA_DN * x_ref[...] + _per_slab(o, g1_ref[0])
    x1 = _layer_norm(v, lg_ref[...], lb_ref[...])
    x1_ref[...] = x1
    h2 = _per_slab_add(_per_slab(x1, 1.0 + sc_ref[0]), sh_ref[0])
    bits = lax.bitcast_convert_type(h2.astype(BF16).astype(F32), jnp.uint32)
    h2_ref[...] = (bits[:, 0:D // 2] & jnp.uint32(0xFFFF0000)) | (bits[:, D // 2:D] >> jnp.uint32(16))
    logit = _bdot(h2, rw_ref[...]) + rb_ref[...]
    tm = logit.shape[0]
    lane = lax.broadcasted_iota(jnp.int32, (tm, N_EXP), 1)
    lane8 = lax.broadcasted_iota(jnp.int32, (tm, 8), 1)
    idx8 = jnp.zeros((tm, 8), jnp.int32)
    val8 = jnp.zeros((tm, 8), F32)
    top0 = None
    den = None
    for k in range(TOP_K):
        mx = jnp.max(logit, axis=1, keepdims=True)
        sel = jnp.min(jnp.where(logit == mx, lane, N_EXP), axis=1, keepdims=True)
        if k == 0:
            top0 = mx
        ek = jnp.exp(mx - top0)
        den = ek if k == 0 else den + ek
        idx8 = jnp.where(lane8 == k, sel, idx8)
        val8 = jnp.where(lane8 == k, ek, val8)
        logit = jnp.where(lane == sel, -jnp.inf, logit)
    ti_ref[...] = idx8
    tw_ref[...] = val8 / den


def _outproj_call(gid3, ys, x, mod3, prm):
    R = x.shape[0]
    per = ROWS // K3_TM
    rspec = lambda w: pl.BlockSpec((K3_TM, w), lambda i, gid: (i, 0))
    yspec = pl.BlockSpec((WG // LANE, K3_TM, LANE), lambda i, gid: (0, i, 0))
    mspec = lambda k: pl.BlockSpec((1, NB, D), lambda i, gid: (gid[i // per], 0, k))
    cspec = lambda shape: pl.BlockSpec(shape, lambda i, gid: (0,) * len(shape))
    gs = pltpu.PrefetchScalarGridSpec(
        num_scalar_prefetch=1, grid=(R // K3_TM,),
        in_specs=[yspec, yspec, yspec, yspec, rspec(D), mspec(2), mspec(4), mspec(3),
                  cspec((4 * WG, D)), cspec((1, D)), cspec((1, D)), cspec((D, N_EXP)), cspec((1, N_EXP))],
        out_specs=[rspec(D), rspec(D // 2), rspec(8), rspec(8)],
    )
    return pl.pallas_call(
        _outproj_kernel, grid_spec=gs,
        out_shape=[jax.ShapeDtypeStruct((R, D), F32), jax.ShapeDtypeStruct((R, D // 2), jnp.uint32),
                   jax.ShapeDtypeStruct((R, 8), jnp.int32), jax.ShapeDtypeStruct((R, 8), F32)],
        compiler_params=_cparams(("arbitrary",)),
    )(gid3, *ys, x, mod3, mod3, mod3, prm['w_out'], prm['ln1_g'], prm['ln1_b'], prm['router_w'], prm['router_b'])


def _moe_gather(idx_ref, h2_ref, dst, blk):
    s0 = (blk + 1) * MOE_RB
    for r in range(MOE_RB):
        tok = idx_ref[0, 0, s0 + r] & 0xFFFF
        dst[r:r + 1, :] = h2_ref[pl.ds(tok, 1), :]


def _moe_scatter(idx_ref, gw_ref, o_ref, ysrc, blk):
    s0 = (blk + 1) * MOE_RB
    for q in range(MOE_RB // MOE_UNROLL):
        toks, vals = [], []
        for j in range(MOE_UNROLL):
            r = q * MOE_UNROLL + j
            tok = idx_ref[0, 0, s0 + r] >> 16
            w = gw_ref[0, 0, s0 + r]
            toks.append(tok)
            vals.append(o_ref[0, pl.ds(tok, 1), :] + w * ysrc[r:r + 1, :])
        for tok, val in zip(toks, vals):
            o_ref[0, pl.ds(tok, 1), :] = val


def _moe_ffn(lhs, wgu_ref, wd_ref, bgu_ref, bd_ref, ydst):
    pk = lhs[...]
    lo = lax.bitcast_convert_type(pk << jnp.uint32(16), F32)
    hi = lax.bitcast_convert_type(pk & jnp.uint32(0xFFFF0000), F32)
    xb = jnp.concatenate([hi, lo], axis=1).astype(BF16)
    acc = None
    for c in range(D_FF // MOE_FC):
        lo_c, hi_c = MOE_FC * c, MOE_FC * (c + 1)
        g = jnp.dot(xb, wgu_ref[0, :, lo_c:hi_c], preferred_element_type=F32) + bgu_ref[0, :, lo_c:hi_c]
        u = (jnp.dot(xb, wgu_ref[0, :, D_FF + lo_c:D_FF + hi_c], preferred_element_type=F32)
             + bgu_ref[0, :, D_FF + lo_c:D_FF + hi_c])
        g = jnp.minimum(g, SWIGLU_LIMIT)
        u = jnp.clip(u, -SWIGLU_LIMIT, SWIGLU_LIMIT)
        hdn = ((u + 1.0) * g * _sigmoid(SWIGLU_ALPHA * g)).astype(BF16)
        part = jnp.dot(hdn, wd_ref[0, lo_c:hi_c, :], preferred_element_type=F32)
        acc = part if c == 0 else acc + part
    ydst[...] = acc + bd_ref[0]


def _moe_kernel(bexp_ref, nblk_ref, idx_ref, gw_ref, h2_ref, wgu_ref, wd_ref, bgu_ref, bd_ref,
                o_ref, la_s, lb_s, ya_s, yb_s):
    i = pl.program_id(0)
    s = pl.program_id(1)

    @pl.when(s == 0)
    def _():
        o_ref[...] = jnp.zeros_like(o_ref)
        yb_s[...] = jnp.zeros_like(yb_s)
        _moe_gather(idx_ref, h2_ref, la_s, s)

    def step(cur_l, nxt_l, cur_y, prv_y):
        _moe_scatter(idx_ref, gw_ref, o_ref, prv_y, s - 1)
        _moe_gather(idx_ref, h2_ref, nxt_l, s + 1)
        _moe_ffn(cur_l, wgu_ref, wd_ref, bgu_ref, bd_ref, cur_y)

    active = s <= nblk_ref[i]

    @pl.when(active & (s % 2 == 0))
    def _():
        step(la_s, lb_s, ya_s, yb_s)

    @pl.when(active & (s % 2 == 1))
    def _():
        step(lb_s, la_s, yb_s, ya_s)


MOE_RUN = MOE_RB


def _moe_dims(ts):
    nbmax = ts * TOP_K // MOE_RB + N_EXP
    nsteps = nbmax + 1
    nba = nbmax + 3
    return nbmax, nsteps, nba


def _moe_call(bexp, nblk, idx, wp, h2p, prm, ts):
    R = h2p.shape[0]
    nst = R // ts
    nbmax, nsteps, nba = _moe_dims(ts)
    one = pl.Buffered(1)
    wspec = lambda shape: pl.BlockSpec(shape, lambda i, s, bexp, nblk: (bexp[i * nsteps + s], 0, 0))
    gs = pltpu.PrefetchScalarGridSpec(
        num_scalar_prefetch=2, grid=(nst, nsteps),
        in_specs=[pl.BlockSpec((1, 1, nba * MOE_RB), lambda i, s, *_: (i, 0, 0), memory_space=pltpu.SMEM),
                  pl.BlockSpec((1, 1, nba * MOE_RB), lambda i, s, *_: (i, 0, 0), memory_space=pltpu.SMEM),
                  pl.BlockSpec((ts, D // 2), lambda i, s, *_: (i, 0), pipeline_mode=one),
                  wspec((1, D, 2 * D_FF)), wspec((1, D_FF, D)), wspec((1, 1, 2 * D_FF)), wspec((1, 1, D))],
        out_specs=pl.BlockSpec((1, ts + MOE_UNROLL, D), lambda i, s, *_: (i, 0, 0), pipeline_mode=one),
        scratch_shapes=[pltpu.VMEM((MOE_RB, D // 2), jnp.uint32), pltpu.VMEM((MOE_RB, D // 2), jnp.uint32),
                        pltpu.VMEM((MOE_RB, D), F32), pltpu.VMEM((MOE_RB, D), F32)],
    )
    return pl.pallas_call(
        _moe_kernel, grid_spec=gs,
        out_shape=jax.ShapeDtypeStruct((nst, ts + MOE_UNROLL, D), F32),
        compiler_params=_cparams(("arbitrary", "arbitrary")),
    )(bexp, nblk, idx, wp, h2p, prm['moe_wgu'], prm['moe_wd'], prm['moe_bgu'], prm['moe_bd'])


def _moe_plan(topi, topw, ts):
    R = topi.shape[0]
    nst = R // ts
    na = ts * TOP_K
    nbmax, nsteps, nba = _moe_dims(ts)
    e_flat = topi[:, :TOP_K].reshape(nst, na)
    w_flat = topw[:, :TOP_K].reshape(nst, na)
    ex = jnp.arange(N_EXP, dtype=jnp.int32)
    counts = jnp.sum(e_flat[:, :, None] == ex[None, None, :], axis=1).astype(jnp.int32)
    padded = (counts + MOE_RUN - 1) // MOE_RUN * MOE_RUN
    pad_end = jnp.cumsum(padded, axis=1)
    total = pad_end[:, -1:]
    nblk = (total[:, 0] // MOE_RB).astype(jnp.int32)
    blk0 = jnp.minimum(jnp.arange(nsteps, dtype=jnp.int32)[None, :] * MOE_RUN, total - MOE_RUN)
    bexp = jnp.minimum(jnp.sum(blk0[:, :, None] >= pad_end[:, None, :], axis=2), N_EXP - 1).astype(jnp.int32)
    m = na + MOE_RUN
    unused = N_EXP * m
    key_real = e_flat * m + jnp.arange(na, dtype=jnp.int32)[None, :]
    d = jnp.arange(MOE_RUN - 1, dtype=jnp.int32)[None, None, :]
    key_pad = jnp.where(d < (padded - counts)[:, :, None], ex[None, :, None] * m + na + d, unused)
    keys = jnp.concatenate([key_real, key_pad.reshape(nst, -1)], axis=1)
    vals = jnp.concatenate([w_flat, jnp.zeros((nst, N_EXP * (MOE_RUN - 1)), F32)], axis=1)
    keys, vals = lax.sort((keys, vals), dimension=1, num_keys=1)
    n_tail = nba * MOE_RB - MOE_RB - keys.shape[1]
    assert n_tail >= 0
    keys = jnp.concatenate([jnp.full((nst, MOE_RB), unused, jnp.int32), keys,
                            jnp.full((nst, n_tail), unused, jnp.int32)], axis=1)
    wp = jnp.concatenate([jnp.zeros((nst, MOE_RB), F32), vals, jnp.zeros((nst, n_tail), F32)], axis=1)
    j = keys % m
    real = (keys < unused) & (j < na)
    tok = j // TOP_K
    spare = ts + (jnp.arange(nba * MOE_RB, dtype=jnp.int32)[None, :] & (MOE_UNROLL - 1))
    idx = jnp.where(real, tok | (tok << 16), spare << 16).astype(jnp.int32)
    wp = jnp.where(real, wp, 0.0)
    return bexp.reshape(-1), nblk, idx.reshape(nst, 1, -1), wp.reshape(nst, 1, -1)


def _wperm_kernel(w_ref, p_ref, o_ref):
    for j in range(D_FF // LANE):
        blk = jnp.dot(w_ref[0, :, 2 * LANE * j:2 * LANE * (j + 1)].astype(BF16), p_ref[...],
                      preferred_element_type=F32).astype(BF16)
        o_ref[0, :, LANE * j:LANE * (j + 1)] = blk[:, 0:LANE]
        o_ref[0, :, D_FF + LANE * j:D_FF + LANE * (j + 1)] = blk[:, LANE:2 * LANE]


def _wperm_call(wgu):
    rows = lax.broadcasted_iota(jnp.int32, (2 * LANE, 2 * LANE), 0)
    cols = lax.broadcasted_iota(jnp.int32, (2 * LANE, 2 * LANE), 1)
    perm = (rows == 2 * (cols % LANE) + cols // LANE).astype(BF16)
    return pl.pallas_call(
        _wperm_kernel,
        grid=(N_EXP,),
        in_specs=[pl.BlockSpec((1, D, 2 * D_FF), lambda e: (e, 0, 0)),
                  pl.BlockSpec((2 * LANE, 2 * LANE), lambda e: (0, 0))],
        out_specs=pl.BlockSpec((1, D, 2 * D_FF), lambda e: (e, 0, 0)),
        out_shape=jax.ShapeDtypeStruct((N_EXP, D, 2 * D_FF), BF16),
        compiler_params=_cparams(("arbitrary",)),
    )(wgu, perm)


def _ln2_kernel(gid_ref, x1_ref, y_ref, g2_ref, lg_ref, lb_ref, o_ref):
    v = ALPHA_DN * x1_ref[...] + _per_slab(y_ref[0], g2_ref[0])
    o_ref[...] = _layer_norm(v, lg_ref[...], lb_ref[...])


def _ln2_call(gid, x1, y, mod3, prm):
    R = x1.shape[0]
    per = (y.shape[1] - MOE_UNROLL) // ROWS
    rspec = pl.BlockSpec((ROWS, D), lambda i, gid: (i, 0))
    yspec = pl.BlockSpec((1, ROWS, D), lambda i, gid: (i // per, i % per, 0))
    cspec = pl.BlockSpec((1, D), lambda i, gid: (0, 0))
    gs = pltpu.PrefetchScalarGridSpec(
        num_scalar_prefetch=1, grid=(R // ROWS,),
        in_specs=[rspec, yspec, pl.BlockSpec((1, NB, D), lambda i, gid: (gid[i], 0, 5)), cspec, cspec],
        out_specs=rspec,
    )
    return pl.pallas_call(
        _ln2_kernel, grid_spec=gs, out_shape=jax.ShapeDtypeStruct((R, D), F32),
        compiler_params=_cparams(("arbitrary",)),
    )(gid, x1, y, mod3, prm['ln2_g'], prm['ln2_b'])


def _block_diag(blocks):
    n, r, c = blocks.shape
    eye = jnp.eye(n, dtype=blocks.dtype)
    return jnp.einsum('nrc,nm->nrmc', blocks, eye).reshape(n * r, n * c)


def _prep_layer(p, l):
    prm = {}
    w = p['w_in'][l]
    pieces = [w[:, 0:512], w[:, 512:1024], w[:, 1280:1792], w[:, 1808:2320], w[:, 2320:2832], w[:, 3344:3856],
              w[:, 3856:4368], w[:, 2832:3088], w[:, 3088:3344], w[:, 1024:1152], w[:, 1152:1280],
              w[:, 1792:1808], w[:, 4368:4384], jnp.zeros((D, 96), w.dtype)]
    prm['w_in'] = jnp.concatenate(pieces, axis=1).astype(BF16)
    b_re, b_im = p['s5_b_re'][l], p['s5_b_im'][l]
    bt = []
    for j in range(4):
        sl = slice(8 * j, 8 * j + 8)
        bre = _block_diag(jnp.swapaxes(b_re[sl], 1, 2))
        bim = _block_diag(jnp.swapaxes(b_im[sl], 1, 2))
        bt.append(jnp.concatenate([bre, bim], axis=1))
    prm['s5_bt'] = jnp.stack(bt).astype(BF16)
    lam = lax.complex(p['s5_lam_re'][l], p['s5_lam_im'][l])
    dt = jnp.exp(p['s5_log_dt'][l])[:, :, None]
    lam_bar = jnp.exp(lam * dt)
    f = (lam_bar - 1.0) / lam
    prm['s5_lr'] = jnp.real(lam_bar).reshape(2, 4, 1, WG)
    prm['s5_li'] = jnp.imag(lam_bar).reshape(2, 4, 1, WG)
    c_c = lax.complex(p['s5_c_re'][l], p['s5_c_im'][l])
    cms = []
    for d in range(2):
        e = c_c * f[d][:, None, :]
        er = jnp.swapaxes(jnp.real(e), 1, 2)
        ei = jnp.swapaxes(jnp.imag(e), 1, 2)
        cm = []
        for j in range(4):
            sl = slice(8 * j, 8 * j + 8)
            cm.append(jnp.concatenate([_block_diag(er[sl]), -_block_diag(ei[sl])], axis=0))
        cms.append(jnp.stack(cm))
    prm['s5_cm'] = jnp.stack(cms).astype(BF16)
    prm['s5_d'] = p['s5_d'][l].reshape(1, WG)
    prm['s5_glu_w'] = p['s5_glu_w'][l].astype(BF16)
    prm['s5_glu_b'] = p['s5_glu_b'][l].reshape(1, WG)
    prm['ssd_conv_w'] = p['ssd_conv_w'][l]
    prm['ssd_conv_b'] = p['ssd_conv_b'][l].reshape(1, -1)
    pad = lambda v: jnp.concatenate([v.reshape(-1), jnp.zeros((128 - v.size,), F32)]).reshape(1, 128)
    prm['ssd_dtb'] = pad(p['ssd_dt_bias'][l])
    prm['ssd_arow'] = pad(-jnp.exp(p['ssd_a_log'][l]))
    prm['ssd_drow'] = jnp.repeat(p['ssd_d'][l], SSD_P).reshape(1, WG)
    prm['ssd_norm_g'] = p['ssd_norm_g'][l].reshape(1, WG)
    prm['lru_conv_w'] = p['lru_conv_w'][l]
    prm['lru_conv_b'] = p['lru_conv_b'][l].reshape(1, WG)
    prm['lru_wa'] = jnp.stack([_block_diag(p['lru_wa'][l][d]) for d in range(2)]).astype(BF16)
    prm['lru_wx'] = jnp.stack([_block_diag(p['lru_wx'][l][d]) for d in range(2)]).astype(BF16)
    prm['lru_ba'] = p['lru_ba'][l].reshape(2, 1, WG)
    prm['lru_bx'] = p['lru_bx'][l].reshape(2, 1, WG)
    prm['lru_sp'] = jax.nn.softplus(-p['lru_lam'][l]).reshape(2, 1, WG)
    prm['ml_gb'] = jnp.concatenate([jnp.zeros((SM_GATE,), F32), p['ml_gate_b'][l].reshape(-1),
                                    jnp.zeros((128 - SM_GATE - 4 * ML_H,), F32)]).reshape(1, 128)
    prm['ml_norm_g'] = p['ml_norm_g'][l].reshape(1, WG)
    prm['w_out'] = p['w_out'][l].astype(BF16)
    for k in ('ln1_g', 'ln1_b', 'ln2_g', 'ln2_b'):
        prm[k] = p[k][l].reshape(1, D)
    prm['router_w'] = p['router_w'][l]
    prm['router_b'] = p['router_b'][l].reshape(1, N_EXP)
    prm['moe_wgu'] = _wperm_call(p['moe_w_gate_up'][l])
    bgu = p['moe_b_gate_up'][l]
    prm['moe_bgu'] = jnp.concatenate([bgu[:, 0::2], bgu[:, 1::2]], axis=1).reshape(N_EXP, 1, 2 * D_FF)
    prm['moe_wd'] = p['moe_w_down'][l].astype(BF16)
    prm['moe_bd'] = p['moe_b_down'][l].reshape(N_EXP, 1, D)
    prm['ada_w'] = p['ada_w'][l].astype(BF16)
    prm['ada_b'] = p['ada_b'][l].reshape(1, 6 * D)
    return prm


def _mixers(flags, proj, prm):
    ya = _s5_call(flags, proj, None, prm, 0, reverse=False, final=False)
    ya = _s5_call(flags, proj, ya, prm, 1, reverse=True, final=True)
    yb = _ssd_call(flags, proj, None, prm, reverse=False, final=False)
    yb = _ssd_call(flags, proj, yb, prm, reverse=True, final=True)
    yc = _lru_call(flags, proj, None, prm, 0, reverse=False, final=False)
    yc = _lru_call(flags, proj, yc, prm, 1, reverse=True, final=True)
    yd = _mlstm_call(flags, proj, None, prm, reverse=False, final=False)
    yd = _mlstm_call(flags, proj, yd, prm, reverse=True, final=True)
    return ya, yb, yc, yd


def _to_rows(x):
    b, L, _ = x.shape
    return x.reshape(b // NB, NB, L, D).transpose(0, 2, 1, 3).reshape(b * L, D)


def _from_rows(r, b, L):
    return r.reshape(b // NB, L, NB, D).transpose(0, 2, 1, 3).reshape(b, L, D)


def _trunk(xs, cs, p, depth, moe_ts):
    first, last, gid = [], [], []
    g = 0
    for x in xs:
        b, L, _ = x.shape
        assert b % NB == 0 and L % CH == 0
        n = L // CH
        for _ in range(b // NB):
            first += [1] + [0] * (n - 1)
            last += [0] * (n - 1) + [1]
            gid += [g] * n
            g += 1
    first = jnp.asarray(first, jnp.int32)
    last = jnp.asarray(last, jnp.int32)
    gid = jnp.asarray(gid, jnp.int32)
    flags = (first, last)
    x = jnp.concatenate([_to_rows(x.astype(F32)) for x in xs], axis=0)
    c_all = jnp.concatenate([c.astype(F32) for c in cs], axis=0)
    R = x.shape[0]
    ts = min(moe_ts, R)
    assert R % ts == 0
    for l in range(depth):
        prm = _prep_layer(p, l)
        mod3 = _mod_call(c_all, prm['ada_w'], prm['ada_b']).reshape(g, NB, 6 * D)
        proj = _inproj_call(gid, x, mod3, prm['w_in'])
        ys = _mixers(flags, proj, prm)
        x1, h2, topi, topw = _outproj_call(gid, ys, x, mod3, prm)
        bexp, nblk, idx, wp = _moe_plan(topi, topw, ts)
        y = _moe_call(bexp, nblk, idx, wp, h2, prm, ts)
        x = _ln2_call(gid, x1, y, mod3, prm)
    outs = []
    r0 = 0
    for xin in xs:
        b, L, _ = xin.shape
        outs.append(_from_rows(x[r0:r0 + b * L], b, L).astype(xin.dtype))
        r0 += b * L
    return outs


def kernel(x_prompt, x_sample, c_prompt, c_sample, ada_w, ada_b, w_in, s5_lam_re, s5_lam_im, s5_log_dt, s5_b_re, s5_b_im, s5_c_re, s5_c_im, s5_d, s5_glu_w, s5_glu_b, ssd_conv_w, ssd_conv_b, ssd_a_log, ssd_dt_bias, ssd_d, ssd_norm_g, lru_conv_w, lru_conv_b, lru_wa, lru_ba, lru_wx, lru_bx, lru_lam, ml_gate_b, ml_norm_g, w_out, ln1_g, ln1_b, router_w, router_b, moe_w_gate_up, moe_b_gate_up, moe_w_down, moe_b_down, ln2_g, ln2_b):
    p = {
        'ada_w': ada_w, 'ada_b': ada_b, 'w_in': w_in,
        's5_lam_re': s5_lam_re, 's5_lam_im': s5_lam_im, 's5_log_dt': s5_log_dt,
        's5_b_re': s5_b_re, 's5_b_im': s5_b_im, 's5_c_re': s5_c_re, 's5_c_im': s5_c_im,
        's5_d': s5_d, 's5_glu_w': s5_glu_w, 's5_glu_b': s5_glu_b,
        'ssd_conv_w': ssd_conv_w, 'ssd_conv_b': ssd_conv_b, 'ssd_a_log': ssd_a_log,
        'ssd_dt_bias': ssd_dt_bias, 'ssd_d': ssd_d, 'ssd_norm_g': ssd_norm_g,
        'lru_conv_w': lru_conv_w, 'lru_conv_b': lru_conv_b, 'lru_wa': lru_wa, 'lru_ba': lru_ba,
        'lru_wx': lru_wx, 'lru_bx': lru_bx, 'lru_lam': lru_lam,
        'ml_gate_b': ml_gate_b, 'ml_norm_g': ml_norm_g, 'w_out': w_out,
        'ln1_g': ln1_g, 'ln1_b': ln1_b, 'router_w': router_w, 'router_b': router_b,
        'moe_w_gate_up': moe_w_gate_up, 'moe_b_gate_up': moe_b_gate_up,
        'moe_w_down': moe_w_down, 'moe_b_down': moe_b_down, 'ln2_g': ln2_g, 'ln2_b': ln2_b,
    }
    y_prompt, y_sample = _trunk([x_prompt, x_sample], [c_prompt, c_sample], p, DEPTH, MOE_TS)
    return (y_prompt, y_sample)
```
